```python
import math
import jax, jax.numpy as jnp
from jax import lax
import numpy as np

D_MODEL = 1024
BATCH = 8
SEQ = 4096
DEPTH = 2

DEEPNORM_ALPHA = (2 * DEPTH) ** 0.25
DEEPNORM_BETA = (8 * DEPTH) ** -0.25
LN_EPS = 1e-5
N_EVEN = (DEPTH + 1) // 2
N_ODD = DEPTH // 2

GDN_HEADS = D_MODEL // 256
GDN_KDIM = 128
GDN_VDIM = 128
GDN_W = GDN_HEADS * GDN_VDIM
CONV_K = 4
GDN_CHUNK = 64

S5_W = D_MODEL // 2
S5_GROUP_CH = 16
S5_GROUPS = S5_W // S5_GROUP_CH
S5_STATE = 64

MIX_W = GDN_W + S5_W
IN_AB = 4 * GDN_W + 2 * GDN_HEADS + S5_W

ATTN_HEAD_DIM = 64
ATTN_HEADS = D_MODEL // ATTN_HEAD_DIM
DILATED_CONFIGS = ((128, 1), (512, 4), (2048, 16))

MOE_GROUPS = 4
EXPERTS_PER_GROUP = 8
N_EXPERTS = MOE_GROUPS * EXPERTS_PER_GROUP
EXPERT_FF = D_MODEL // 2
MOE_TOP_K = 2
MOE_BLOCK = 128

kernel_name = "hybrid_gdn_s5_dilated_hmoe_trunk"


def _layer_norm(x, g, b):
    xf = x.astype(jnp.float32)
    mu = xf.mean(-1, keepdims=True)
    var = jnp.square(xf - mu).mean(-1, keepdims=True)
    return ((xf - mu) * lax.rsqrt(var + LN_EPS) * g.astype(jnp.float32) + b.astype(jnp.float32)).astype(x.dtype)


def _l2norm(t):
    return t * lax.rsqrt(jnp.sum(t * t, axis=-1, keepdims=True) + 1e-6)


def _causal_depthwise_conv(x, w):
    k = w.shape[0]
    return lax.conv_general_dilated(x, w[:, None, :], window_strides=(1,), padding=((k - 1, 0),),
                                    dimension_numbers=('NWC', 'WIO', 'NWC'), feature_group_count=x.shape[-1])


def _chunked_gated_delta_rule(q, k, v, g, beta):
    bsz, s, h, dk = q.shape
    dv = v.shape[-1]
    n = s // GDN_CHUNK

    def to_chunks(t):
        return t.reshape(bsz, n, GDN_CHUNK, h, t.shape[-1]).transpose(0, 3, 1, 2, 4)

    q = to_chunks(q) * (dk ** -0.5)
    k = to_chunks(k)
    v = to_chunks(v)
    g = to_chunks(g[..., None])[..., 0]
    beta = to_chunks(beta[..., None])[..., 0]
    gc = jnp.cumsum(g, axis=-1)
    idx = jnp.arange(GDN_CHUNK)
    causal = idx[:, None] >= idx[None, :]
    strict = idx[:, None] > idx[None, :]
    decay = jnp.exp(jnp.where(causal, gc[..., :, None] - gc[..., None, :], -jnp.inf))
    kb = k * beta[..., None]
    lower = jnp.where(strict, jnp.einsum('bhnik,bhnjk->bhnij', kb, k) * decay, 0.0)
    eye = jnp.eye(GDN_CHUNK, dtype=jnp.float32)
    rhs = jnp.concatenate([v * beta[..., None], kb * jnp.exp(gc)[..., None]], axis=-1)
    sol = lax.linalg.triangular_solve(lower + eye, rhs, left_side=True, lower=True, unit_diagonal=True)
    u, w = sol[..., :dv], sol[..., dv:]
    qk = jnp.where(causal, jnp.einsum('bhnik,bhnjk->bhnij', q, k) * decay, 0.0)
    q_dec = q * jnp.exp(gc)[..., None]
    k_dec = k * jnp.exp(gc[..., -1:] - gc)[..., None]
    g_last = jnp.exp(gc[..., -1])
    xs = (jnp.moveaxis(u, 2, 0), jnp.moveaxis(w, 2, 0), jnp.moveaxis(qk, 2, 0),
          jnp.moveaxis(q_dec, 2, 0), jnp.moveaxis(k_dec, 2, 0), jnp.moveaxis(g_last, 2, 0))

    def step(state, inp):
        u_n, w_n, qk_n, qd_n, kd_n, gl_n = inp
        v_new = u_n - jnp.einsum('bhck,bhkv->bhcv', w_n, state)
        o_n = jnp.einsum('bhck,bhkv->bhcv', qd_n, state) + jnp.einsum('bhij,bhjv->bhiv', qk_n, v_new)
        state = state * gl_n[..., None, None] + jnp.einsum('bhck,bhcv->bhkv', kd_n, v_new)
        return state, o_n

    state0 = jnp.zeros((bsz, h, dk, dv), jnp.float32)
    _, o = lax.scan(step, state0, xs)
    return o.transpose(1, 0, 3, 2, 4).reshape(bsz, s, h, dv)


def _gated_deltanet(qkv_in, z, b_logit, a_logit, conv_w, a_log, dt_bias, norm_w):
    bsz, s, _ = qkv_in.shape
    qkv = jax.nn.silu(_causal_depthwise_conv(qkv_in, conv_w)).astype(jnp.float32)
    q, k, v = jnp.split(qkv, 3, axis=-1)
    q = _l2norm(q.reshape(bsz, s, GDN_HEADS, GDN_KDIM))
    k = _l2norm(k.reshape(bsz, s, GDN_HEADS, GDN_KDIM))
    v = v.reshape(bsz, s, GDN_HEADS, GDN_VDIM)
    beta = jax.nn.sigmoid(b_logit.astype(jnp.float32))
    g = -jnp.exp(a_log.astype(jnp.float32)) * jax.nn.softplus(a_logit.astype(jnp.float32) + dt_bias.astype(jnp.float32))
    o = _chunked_gated_delta_rule(q, k, v, g, beta)
    zh = z.astype(jnp.float32).reshape(bsz, s, GDN_HEADS, GDN_VDIM)
    o = o * lax.rsqrt(jnp.mean(o * o, axis=-1, keepdims=True) + 1e-6) * norm_w.astype(jnp.float32) * jax.nn.silu(zh)
    return o.reshape(bsz, s, GDN_W)


def _s5_layer(u, lam_re, lam_im, log_dt, b_re, b_im, c_re, c_im, d_skip, w_glu):
    bsz, s, _ = u.shape
    uf = u.astype(jnp.float32)
    ug = uf.reshape(bsz, s, S5_GROUPS, S5_GROUP_CH)
    dt = jnp.exp(log_dt.astype(jnp.float32))[:, None]
    lr, li = lam_re.astype(jnp.float32), lam_im.astype(jnp.float32)
    mag = jnp.exp(lr * dt)
    a_re, a_im = mag * jnp.cos(li * dt), mag * jnp.sin(li * dt)
    den = lr * lr + li * li
    nr, ni = a_re - 1.0, a_im
    cr = (nr * lr + ni * li) / den
    ci = (ni * lr - nr * li) / den
    br, bi = b_re.astype(jnp.float32), b_im.astype(jnp.float32)
    bb_re = cr[..., None] * br - ci[..., None] * bi
    bb_im = cr[..., None] * bi + ci[..., None] * br
    bu_re = jnp.einsum('bsgh,gph->bsgp', ug, bb_re)
    bu_im = jnp.einsum('bsgh,gph->bsgp', ug, bb_im)
    a_re_t = jnp.broadcast_to(a_re, (1, s) + a_re.shape)
    a_im_t = jnp.broadcast_to(a_im, (1, s) + a_im.shape)

    def combine(e1, e2):
        a1r, a1i, b1r, b1i = e1
        a2r, a2i, b2r, b2i = e2
        return (a2r * a1r - a2i * a1i, a2r * a1i + a2i * a1r,
                a2r * b1r - a2i * b1i + b2r, a2r * b1i + a2i * b1r + b2i)

    _, _, h_re, h_im = lax.associative_scan(combine, (a_re_t, a_im_t, bu_re, bu_im), axis=1)
    y = (jnp.einsum('gkp,bsgp->bsgk', c_re.astype(jnp.float32), h_re)
         - jnp.einsum('gkp,bsgp->bsgk', c_im.astype(jnp.float32), h_im))
    y = y.reshape(bsz, s, S5_W) + d_skip.astype(jnp.float32) * uf
    y = jax.nn.gelu(y)
    return y * jax.nn.sigmoid(y @ w_glu.astype(jnp.float32))


def _mixer_ab(x, w_in, conv_w, a_log, dt_bias, norm_w, lam_re, lam_im, log_dt,
              b_re, b_im, c_re, c_im, d_skip, w_glu, w_out):
    proj = x @ w_in
    qkv, z, b_logit, a_logit, u = jnp.split(
        proj, [3 * GDN_W, 4 * GDN_W, 4 * GDN_W + GDN_HEADS, 4 * GDN_W + 2 * GDN_HEADS], axis=-1)
    ya = _gated_deltanet(qkv, z, b_logit, a_logit, conv_w, a_log, dt_bias, norm_w)
    yb = _s5_layer(u, lam_re, lam_im, log_dt, b_re, b_im, c_re, c_im, d_skip, w_glu)
    y = jnp.concatenate([ya, yb], axis=-1).astype(x.dtype)
    return y @ w_out


def _dilated_branch(q, k, v, dilation, steps):
    bsz, s, h, e = q.shape
    length = s // dilation
    nb = -(-length // steps)
    lp = nb * steps

    def by_residue(t):
        t = t.reshape(bsz, length, dilation, h, e).transpose(0, 2, 1, 3, 4)
        t = jnp.pad(t, ((0, 0), (0, 0), (0, lp - length), (0, 0), (0, 0)))
        return t.reshape(bsz, dilation, nb, steps, h, e)

    def with_prev(t):
        prev = jnp.pad(t, ((0, 0), (0, 0), (1, 0), (0, 0), (0, 0), (0, 0)))[:, :, :-1]
        return jnp.concatenate([prev, t], axis=3)

    qb = by_residue(q)
    kk = with_prev(by_residue(k))
    vv = with_prev(by_residue(v))
    scores = jnp.einsum('bdnqhe,bdnkhe->bdhnqk', qb, kk).astype(jnp.float32) * (e ** -0.5)
    i = jnp.arange(steps)[:, None]
    j = jnp.arange(2 * steps)[None, :]
    dist = steps + i - j
    blk = jnp.arange(nb)[:, None, None]
    valid = (dist >= 0) & (dist <= steps) & (blk * steps + j - steps >= 0)
    scores = jnp.where(valid, scores, -1e30)
    m = jnp.max(scores, axis=-1, keepdims=True)
    p = jnp.exp(scores - m)
    ssum = jnp.sum(p, axis=-1, keepdims=True)
    out = jnp.einsum('bdhnqk,bdnkhe->bdnqhe', p / ssum, vv.astype(jnp.float32))
    lse = (m + jnp.log(ssum))[..., 0]
    out = out.reshape(bsz, dilation, lp, h, e)[:, :, :length].transpose(0, 2, 1, 3, 4).reshape(bsz, s, h, e)
    lse = lse.transpose(0, 1, 3, 4, 2).reshape(bsz, dilation, lp, h)[:, :, :length]
    lse = lse.transpose(0, 2, 1, 3).reshape(bsz, s, h)
    return out, lse


def _mixer_c(x, w_qkv, w_out):
    bsz, s, _ = x.shape
    qkv = (x @ w_qkv).reshape(bsz, s, 3, ATTN_HEADS, ATTN_HEAD_DIM)
    q, k, v = qkv[:, :, 0], qkv[:, :, 1], qkv[:, :, 2]
    outs, lses = [], []
    for window, dilation in DILATED_CONFIGS:
        o, l = _dilated_branch(q, k, v, dilation, window // dilation)
        outs.append(o)
        lses.append(l)
    wts = jax.nn.softmax(jnp.stack(lses, axis=0), axis=0)
    o = jnp.einsum('cbsh,cbshe->bshe', wts, jnp.stack(outs, axis=0))
    return o.reshape(bsz, s, D_MODEL).astype(x.dtype) @ w_out


def _hier_moe(x, wg_r, bg_r, we_r, be_r, w_gate, w_up, w_down):
    bsz, s, d = x.shape
    n = bsz * s
    xt = x.reshape(n, d)
    group_prob = jax.nn.softmax((xt @ wg_r + bg_r).astype(jnp.float32), axis=-1)
    p_group, group = lax.top_k(group_prob, 1)
    expert_logits = (xt @ we_r + be_r).astype(jnp.float32).reshape(n, MOE_GROUPS, EXPERTS_PER_GROUP)
    in_group = jnp.einsum('ng,nge->ne', jax.nn.one_hot(group[:, 0], MOE_GROUPS, dtype=jnp.float32), expert_logits)
    top_logit, top_local = lax.top_k(in_group, MOE_TOP_K)
    gate = p_group * jax.nn.softmax(top_logit, axis=-1)
    expert = group * EXPERTS_PER_GROUP + top_local
    nk = n * MOE_TOP_K
    flat_expert = expert.reshape(nk)
    flat_token = jnp.arange(nk, dtype=jnp.int32) // MOE_TOP_K
    flat_gate = gate.reshape(nk)
    order = jnp.argsort(flat_expert)
    sorted_expert = flat_expert[order]
    counts = jnp.bincount(flat_expert, length=N_EXPERTS)
    padded = (counts + MOE_BLOCK - 1) // MOE_BLOCK * MOE_BLOCK
    start = jnp.cumsum(counts) - counts
    padded_end = jnp.cumsum(padded)
    padded_start = padded_end - padded
    slot = padded_start[sorted_expert] + jnp.arange(nk) - start[sorted_expert]
    n_slots = (-(-nk // MOE_BLOCK) + N_EXPERTS) * MOE_BLOCK
    n_blocks = n_slots // MOE_BLOCK
    slot_token = jnp.zeros((n_slots,), jnp.int32).at[slot].set(flat_token[order])
    slot_gate = jnp.zeros((n_slots,), jnp.float32).at[slot].set(flat_gate[order])
    block_expert = jnp.minimum(
        jnp.searchsorted(padded_end, jnp.arange(n_blocks) * MOE_BLOCK, side='right'), N_EXPERTS - 1)
    xb = xt[slot_token].reshape(n_blocks, MOE_BLOCK, d)

    def expert_block(args):
        xi, e = args
        hdn = jax.nn.silu(xi @ w_gate[e]) * (xi @ w_up[e])
        return hdn @ w_down[e]

    yb = lax.map(expert_block, (xb, block_expert))
    y = yb.reshape(n_slots, d) * slot_gate[:, None].astype(yb.dtype)
    out = jnp.zeros((n, d), y.dtype).at[slot_token].add(y)
    return out.reshape(bsz, s, d)


def setup_inputs(seed: int = 0) -> dict:
    key = jax.random.key(seed)
    ks = iter(jax.random.split(key, 48))
    f32 = jnp.float32

    def nrm(shape, scale):
        return jax.random.normal(next(ks), shape, f32) * scale

    def unif(shape, lo, hi):
        return jax.random.uniform(next(ks), shape, f32, lo, hi)

    x = nrm((BATCH, SEQ, D_MODEL), 1.0)
    w_in_ab = nrm((N_EVEN, D_MODEL, IN_AB), D_MODEL ** -0.5)
    conv_qkv = nrm((N_EVEN, CONV_K, 3 * GDN_W), CONV_K ** -0.5)
    gdn_a_log = jnp.log(unif((N_EVEN, GDN_HEADS), 1.0, 16.0))
    dt0 = jnp.exp(unif((N_EVEN, GDN_HEADS), math.log(1e-3), math.log(1e-1)))
    gdn_dt_bias = dt0 + jnp.log(-jnp.expm1(-dt0))
    gdn_norm = 1.0 + nrm((N_EVEN, GDN_VDIM), 0.02)
    s5_lam_re = -0.5 + nrm((N_EVEN, S5_GROUPS, S5_STATE), 0.01)
    s5_lam_im = (math.pi * jnp.arange(S5_STATE, dtype=f32))[None, None, :] + nrm((N_EVEN, S5_GROUPS, S5_STATE), 0.01)
    s5_log_dt = unif((N_EVEN, S5_GROUPS), math.log(1e-3), math.log(1e-1))
    s5_b_re = nrm((N_EVEN, S5_GROUPS, S5_STATE, S5_GROUP_CH), (2 * S5_GROUP_CH) ** -0.5)
    s5_b_im = nrm((N_EVEN, S5_GROUPS, S5_STATE, S5_GROUP_CH), (2 * S5_GROUP_CH) ** -0.5)
    s5_c_re = nrm((N_EVEN, S5_GROUPS, S5_GROUP_CH, S5_STATE), S5_STATE ** -0.5)
    s5_c_im = nrm((N_EVEN, S5_GROUPS, S5_GROUP_CH, S5_STATE), S5_STATE ** -0.5)
    s5_d = nrm((N_EVEN, S5_W), 1.0)
    s5_w_glu = nrm((N_EVEN, S5_W, S5_W), S5_W ** -0.5)
    w_out_ab = nrm((N_EVEN, MIX_W, D_MODEL), MIX_W ** -0.5 * DEEPNORM_BETA)
    w_qkv_c = nrm((N_ODD, D_MODEL, 3 * D_MODEL), D_MODEL ** -0.5)
    w_out_c = nrm((N_ODD, D_MODEL, D_MODEL), D_MODEL ** -0.5 * DEEPNORM_BETA)
    ln_mix_g = 1.0 + nrm((DEPTH, D_MODEL), 0.02)
    ln_mix_b = nrm((DEPTH, D_MODEL), 0.02)
    router_group_w = nrm((DEPTH, D_MODEL, MOE_GROUPS), D_MODEL ** -0.5)
    router_group_b = nrm((DEPTH, MOE_GROUPS), 0.01)
    router_expert_w = nrm((DEPTH, D_MODEL, N_EXPERTS), D_MODEL ** -0.5)
    router_expert_b = nrm((DEPTH, N_EXPERTS), 0.01)
    moe_w_gate = nrm((DEPTH, N_EXPERTS, D_MODEL, EXPERT_FF), D_MODEL ** -0.5)
    moe_w_up = nrm((DEPTH, N_EXPERTS, D_MODEL, EXPERT_FF), D_MODEL ** -0.5)
    moe_w_down = nrm((DEPTH, N_EXPERTS, EXPERT_FF, D_MODEL), EXPERT_FF ** -0.5 * DEEPNORM_BETA)
    ln_ffn_g = 1.0 + nrm((DEPTH, D_MODEL), 0.02)
    ln_ffn_b = nrm((DEPTH, D_MODEL), 0.02)
    return {
        "x": x, "w_in_ab": w_in_ab, "conv_qkv": conv_qkv, "gdn_a_log": gdn_a_log,
        "gdn_dt_bias": gdn_dt_bias, "gdn_norm": gdn_norm, "s5_lam_re": s5_lam_re,
        "s5_lam_im": s5_lam_im, "s5_log_dt": s5_log_dt, "s5_b_re": s5_b_re, "s5_b_im": s5_b_im,
        "s5_c_re": s5_c_re, "s5_c_im": s5_c_im, "s5_d": s5_d, "s5_w_glu": s5_w_glu,
        "w_out_ab": w_out_ab, "w_qkv_c": w_qkv_c, "w_out_c": w_out_c,
        "ln_mix_g": ln_mix_g, "ln_mix_b": ln_mix_b,
        "router_group_w": router_group_w, "router_group_b": router_group_b,
        "router_expert_w": router_expert_w, "router_expert_b": router_expert_b,
        "moe_w_gate": moe_w_gate, "moe_w_up": moe_w_up, "moe_w_down": moe_w_down,
        "ln_ffn_g": ln_ffn_g, "ln_ffn_b": ln_ffn_b,
    }


def reference(x, w_in_ab, conv_qkv, gdn_a_log, gdn_dt_bias, gdn_norm, s5_lam_re, s5_lam_im,
              s5_log_dt, s5_b_re, s5_b_im, s5_c_re, s5_c_im, s5_d, s5_w_glu, w_out_ab,
              w_qkv_c, w_out_c, ln_mix_g, ln_mix_b, router_group_w, router_group_b,
              router_expert_w, router_expert_b, moe_w_gate, moe_w_up, moe_w_down,
              ln_ffn_g, ln_ffn_b):
    h = x
    for layer in range(DEPTH):
        i = layer // 2
        if layer % 2 == 0:
            mix = _mixer_ab(h, w_in_ab[i], conv_qkv[i], gdn_a_log[i], gdn_dt_bias[i], gdn_norm[i],
                            s5_lam_re[i], s5_lam_im[i], s5_log_dt[i], s5_b_re[i], s5_b_im[i],
                            s5_c_re[i], s5_c_im[i], s5_d[i], s5_w_glu[i], w_out_ab[i])
        else:
            mix = _mixer_c(h, w_qkv_c[i], w_out_c[i])
        h = _layer_norm(DEEPNORM_ALPHA * h + mix, ln_mix_g[layer], ln_mix_b[layer])
        ffn = _hier_moe(h, router_group_w[layer], router_group_b[layer], router_expert_w[layer],
                        router_expert_b[layer], moe_w_gate[layer], moe_w_up[layer], moe_w_down[layer])
        h = _layer_norm(DEEPNORM_ALPHA * h + ffn, ln_ffn_g[layer], ln_ffn_b[layer])
    return h
```

```python
import functools
import math

import jax
import jax.numpy as jnp
from jax import lax
from jax.experimental import pallas as pl
from jax.experimental.pallas import tpu as pltpu

F32 = jnp.float32
BF16 = jnp.bfloat16
HI = lax.Precision.HIGHEST

D_MODEL = 1024
DEPTH = 2
ALPHA = (2 * DEPTH) ** 0.25
LN_EPS = 1e-5

GDN_HEADS = 4
GDN_DIM = 128
GDN_W = GDN_HEADS * GDN_DIM
CONV_K = 4
GDN_CHUNK = 64

S5_W = 512
S5_GROUP_CH = 16
S5_GROUPS = 32
S5_STATE = 64
S5_NSTATE = S5_GROUPS * S5_STATE

ATTN_HEADS = 16
ATTN_DIM = 64
ATTN_STEPS = 128
DILATIONS = (1, 4, 16)

MOE_GROUPS = 4
EXPERTS_PER_GROUP = 8
N_EXPERTS = 32
EXPERT_FF = 512
EXPERT_LANE0 = MOE_GROUPS

LANES = 128
SUBLANES = 8
VMEM_LIMIT = 56 * 1024 * 1024

ROW_TILE = 512
S5_STEPS = 128
MOE_ROWS = 256
GATHER_TILE = 256

NEG = -1e30
NT_DIMS = (((1,), (1,)), ((), ()))
TN_DIMS = (((0,), (0,)), ((), ()))


def _params(*sem):
    return pltpu.CompilerParams(dimension_semantics=sem, vmem_limit_bytes=VMEM_LIMIT)


def _sigmoid(x):
    return 1.0 / (1.0 + jnp.exp(-x))


def _softplus(x):
    return jnp.maximum(x, 0.0) + jnp.log(1.0 + jnp.exp(-jnp.abs(x)))


def _layer_norm(r, g, b):
    mu = jnp.mean(r, axis=-1, keepdims=True)
    c = r - mu
    var = jnp.mean(c * c, axis=-1, keepdims=True)
    return c * lax.rsqrt(var + LN_EPS) * g + b


def _dot(a, b, precision=None):
    return jnp.dot(a, b, precision=precision, preferred_element_type=F32)


IN_COLS = 3 * GDN_W + GDN_W + S5_W + LANES


def _inproj_kernel(x_ref, w_ref, qkv_ref, z_ref, u_ref, ba_ref):
    x = x_ref[...].astype(BF16)
    qkv_ref[...] = _dot(x, w_ref[:, 0:1536])
    z_ref[...] = _dot(x, w_ref[:, 1536:2048])
    u_ref[...] = _dot(x, w_ref[:, 2048:2560])
    ba_ref[...] = _dot(x, w_ref[:, 2560:2688])


def _inproj(x2, w_cat):
    n = x2.shape[0]
    tm = ROW_TILE
    row = lambda i: (i, 0)
    return pl.pallas_call(
        _inproj_kernel,
        grid=(n // tm,),
        in_specs=[pl.BlockSpec((tm, D_MODEL), row), pl.BlockSpec((D_MODEL, IN_COLS), lambda i: (0, 0))],
        out_specs=[pl.BlockSpec((tm, 3 * GDN_W), row), pl.BlockSpec((tm, GDN_W), row),
                   pl.BlockSpec((tm, S5_W), row), pl.BlockSpec((tm, LANES), row)],
        out_shape=[jax.ShapeDtypeStruct((n, 3 * GDN_W), F32), jax.ShapeDtypeStruct((n, GDN_W), F32),
                   jax.ShapeDtypeStruct((n, S5_W), F32), jax.ShapeDtypeStruct((n, LANES), F32)],
        compiler_params=_params("arbitrary"),
    )(x2, w_cat)


def _l2norm(t):
    return t * lax.rsqrt(jnp.sum(t * t, axis=-1, keepdims=True) + 1e-6)


def _unit_lower_inverse(low, ri, ci, eye):
    same16 = (ri // 16) == (ci // 16)
    same32 = (ri // 32) == (ci // 32)
    d = jnp.where(same16, low, 0.0)
    t = eye - d
    p = d
    for _ in range(3):
        p = _dot(p, p, HI)
        t = _dot(t, eye + p, HI)
    c32 = jnp.where(same32 & jnp.logical_not(same16), low, 0.0)
    t = t - _dot(_dot(t, c32, HI), t, HI)
    c64 = jnp.where(same32, 0.0, low)
    t = t - _dot(_dot(t, c64, HI), t, HI)
    return t


def _gdn_kernel(qkv_ref, z_ref, ba_ref, convw_ref, alog_ref, dtb_ref, normw_ref, tri_ref, o_ref,
                state_ref, tail_ref):
    c = GDN_CHUNK

    @pl.when(pl.program_id(1) == 0)
    def _():
        state_ref[...] = jnp.zeros_like(state_ref)
        tail_ref[...] = jnp.zeros_like(tail_ref)

    x = qkv_ref[...]
    tail = tail_ref[...]
    w = convw_ref[...]
    row = lax.broadcasted_iota(jnp.int32, (c, 1), 0)
    zpad = jnp.zeros((c - SUBLANES, x.shape[1]), F32)
    conv = x * w[CONV_K - 1:CONV_K, :]
    for j in range(1, CONV_K):
        cur = pltpu.roll(x, j, axis=0)
        prev = jnp.concatenate([pltpu.roll(tail, j, axis=0), zpad], axis=0)
        conv = conv + jnp.where(row < j, prev, cur) * w[CONV_K - 1 - j:CONV_K - j, :]
    tail_ref[...] = x[c - SUBLANES:, :]
    qkv = conv * _sigmoid(conv)

    ba = ba_ref[...]
    beta_all = _sigmoid(ba)
    g_all = -jnp.exp(alog_ref[...]) * _softplus(ba + dtb_ref[...])
    gc_all = _dot(tri_ref[...], g_all, HI)
    egc_all = jnp.exp(gc_all)

    lane = lax.broadcasted_iota(jnp.int32, (c, LANES), 1)
    ri = lax.broadcasted_iota(jnp.int32, (c, c), 0)
    ci = lax.broadcasted_iota(jnp.int32, (c, c), 1)
    causal = ri >= ci
    strict = ri > ci
    eye = (ri == ci).astype(F32)
    normw = normw_ref[...]

    for h in range(GDN_HEADS):
        hs = slice(h * GDN_DIM, (h + 1) * GDN_DIM)
        q = _l2norm(qkv[:, hs]) * (GDN_DIM ** -0.5)
        k = _l2norm(qkv[:, GDN_W + h * GDN_DIM:GDN_W + (h + 1) * GDN_DIM])
        v = qkv[:, 2 * GDN_W + h * GDN_DIM:2 * GDN_W + (h + 1) * GDN_DIM]
        gl = GDN_HEADS + h
        beta = beta_all[:, h:h + 1]
        gc = gc_all[:, gl:gl + 1]
        egc = egc_all[:, gl:gl + 1]
        gc_last = gc_all[c - 1:c, gl:gl + 1]
        sel = (lane == gl).astype(F32)
        gc_row = lax.dot_general(sel, gc_all, NT_DIMS, precision=HI, preferred_element_type=F32)
        decay = jnp.exp(jnp.where(causal, gc - gc_row, NEG))

        kb = k * beta
        low = jnp.where(strict, lax.dot_general(kb, k, NT_DIMS, precision=HI, preferred_element_type=F32) * decay, 0.0)
        t = _unit_lower_inverse(low, ri, ci, eye)
        u = _dot(t, v * beta, HI)
        wk = _dot(t, kb * egc, HI)
        qk = jnp.where(causal, lax.dot_general(q, k, NT_DIMS, precision=HI, preferred_element_type=F32) * decay, 0.0)
        k_dec = k * jnp.exp(gc_last - gc)

        s = state_ref[h]
        v_new = u - _dot(wk, s, HI)
        o = _dot(q * egc, s, HI) + _dot(qk, v_new, HI)
        state_ref[h] = s * jnp.exp(gc_last) + lax.dot_general(k_dec, v_new, TN_DIMS, precision=HI,
                                                               preferred_element_type=F32)
        zh = z_ref[:, hs]
        o = o * lax.rsqrt(jnp.mean(o * o, axis=-1, keepdims=True) + 1e-6) * normw * (zh * _sigmoid(zh))
        o_ref[:, hs] = o


def _gdn(qkv, z, ba, conv_w, a_log, dt_bias, norm_w, bsz, seq):
    c = GDN_CHUNK
    nc = seq // c
    convw = jnp.zeros((SUBLANES, 3 * GDN_W), F32).at[:CONV_K].set(conv_w)
    alog = jnp.zeros((1, LANES), F32).at[0, GDN_HEADS:2 * GDN_HEADS].set(a_log)
    dtb = jnp.zeros((1, LANES), F32).at[0, GDN_HEADS:2 * GDN_HEADS].set(dt_bias)
    tri = jnp.tril(jnp.ones((c, c), F32))
    row = lambda b, i: (b * nc + i, 0)
    const = lambda b, i: (0, 0)
    return pl.pallas_call(
        _gdn_kernel,
        grid=(bsz, nc),
        in_specs=[pl.BlockSpec((c, 3 * GDN_W), row), pl.BlockSpec((c, GDN_W), row), pl.BlockSpec((c, LANES), row),
                  pl.BlockSpec((SUBLANES, 3 * GDN_W), const), pl.BlockSpec((1, LANES), const),
                  pl.BlockSpec((1, LANES), const), pl.BlockSpec((1, GDN_DIM), const), pl.BlockSpec((c, c), const)],
        out_specs=pl.BlockSpec((c, GDN_W), row),
        out_shape=jax.ShapeDtypeStruct((bsz * seq, GDN_W), F32),
        scratch_shapes=[pltpu.VMEM((GDN_HEADS, GDN_DIM, GDN_DIM), F32), pltpu.VMEM((SUBLANES, 3 * GDN_W), F32)],
        compiler_params=_params("arbitrary", "arbitrary"),
    )(qkv, z, ba, convw, alog, dtb, norm_w.reshape(1, GDN_DIM), tri)


S5_LANE_CHUNK = 512


def _gelu_tanh(x):
    return x * (0.5 * (1.0 + jnp.tanh(math.sqrt(2.0 / math.pi) * (x + 0.044715 * (x * x * x)))))


def _s5_kernel(u_ref, wb_ref, a_ref, wc_ref, d_ref, wglu_ref, y_ref, bu_ref, h_ref, *, steps):
    @pl.when(pl.program_id(0) == 0)
    def _():
        h_ref[...] = jnp.zeros_like(h_ref)

    u = u_ref[...]
    nb = h_ref.shape[0]
    bu_ref[...] = _dot(u.astype(BF16), wb_ref[...])

    for lo in range(0, S5_NSTATE, S5_LANE_CHUNK):
        re = slice(lo, lo + S5_LANE_CHUNK)
        im = slice(S5_NSTATE + lo, S5_NSTATE + lo + S5_LANE_CHUNK)
        ar = jnp.broadcast_to(a_ref[0:1, re], (nb, S5_LANE_CHUNK))
        ai = jnp.broadcast_to(a_ref[1:2, re], (nb, S5_LANE_CHUNK))

        def body(t, carry, re=re, im=im, ar=ar, ai=ai):
            hr, hi = carry
            rows = pl.ds(pl.multiple_of(t * nb, nb), nb)
            nr = ar * hr - ai * hi + bu_ref[rows, re]
            ni = ar * hi + ai * hr + bu_ref[rows, im]
            bu_ref[rows, re] = nr
            bu_ref[rows, im] = ni
            return nr, ni

        hr, hi = lax.fori_loop(0, steps, body, (h_ref[:, re], h_ref[:, im]), unroll=4)
        h_ref[:, re] = hr
        h_ref[:, im] = hi

    y = _dot(bu_ref[...].astype(BF16), wc_ref[...]) + d_ref[...] * u
    y = _gelu_tanh(y)
    y_ref[...] = y * _sigmoid(_dot(y.astype(BF16), wglu_ref[...]))


def _s5(u_tb, lam_re, lam_im, log_dt, b_re, b_im, c_re, c_im, d_skip, w_glu, bsz):
    dt = jnp.exp(log_dt)[:, None]
    mag = jnp.exp(lam_re * dt)
    a_re, a_im = mag * jnp.cos(lam_im * dt), mag * jnp.sin(lam_im * dt)
    den = lam_re * lam_re + lam_im * lam_im
    nr, ni = a_re - 1.0, a_im
    cr = (nr * lam_re + ni * lam_im) / den
    ci = (ni * lam_re - nr * lam_im) / den
    bb_re = cr[..., None] * b_re - ci[..., None] * b_im
    bb_im = cr[..., None] * b_im + ci[..., None] * b_re
    eye = jnp.eye(S5_GROUPS, dtype=F32)

    def in_blockdiag(t):
        return jnp.einsum('gph,gk->ghkp', t, eye).reshape(S5_W, S5_NSTATE)

    def out_blockdiag(t):
        return jnp.einsum('gkp,gj->gpjk', t, eye).reshape(S5_NSTATE, S5_W)

    wb = jnp.concatenate([in_blockdiag(bb_re), in_blockdiag(bb_im)], axis=1).astype(BF16)
    wc = jnp.concatenate([out_blockdiag(c_re), -out_blockdiag(c_im)], axis=0).astype(BF16)
    a = jnp.stack([a_re.reshape(S5_NSTATE), a_im.reshape(S5_NSTATE)], axis=0)

    rows_total = u_tb.shape[0]
    rows = S5_STEPS * bsz
    row = lambda i: (i, 0)
    const = lambda i: (0, 0)
    return pl.pallas_call(
        functools.partial(_s5_kernel, steps=S5_STEPS),
        grid=(rows_total // rows,),
        in_specs=[pl.BlockSpec((rows, S5_W), row), pl.BlockSpec((S5_W, 2 * S5_NSTATE), const),
                  pl.BlockSpec((2, S5_NSTATE), const), pl.BlockSpec((2 * S5_NSTATE, S5_W), const),
                  pl.BlockSpec((1, S5_W), const), pl.BlockSpec((S5_W, S5_W), const)],
        out_specs=pl.BlockSpec((rows, S5_W), row),
        out_shape=jax.ShapeDtypeStruct((rows_total, S5_W), F32),
        scratch_shapes=[pltpu.VMEM((rows, 2 * S5_NSTATE), F32), pltpu.VMEM((bsz, 2 * S5_NSTATE), F32)],
        compiler_params=_params("arbitrary"),
    )(u_tb, wb, a, wc, d_skip.reshape(1, S5_W), w_glu.astype(BF16))


def _route(h, wr_ref, br_ref, tri_ref, run_ref, route_ref, cnt_ref):
    @pl.when(pl.program_id(0) == 0)
    def _():
        run_ref[...] = jnp.zeros_like(run_ref)

    tm = h.shape[0]
    logits = _dot(h, wr_ref[...], HI) + br_ref[...]
    lane = lax.broadcasted_iota(jnp.int32, (tm, LANES), 1)
    lanef = lane.astype(F32)
    big = float(LANES)

    gmask = lane < MOE_GROUPS
    gmax = jnp.max(jnp.where(gmask, logits, NEG), axis=-1, keepdims=True)
    gsum = jnp.sum(jnp.where(gmask, jnp.exp(logits - gmax), 0.0), axis=-1, keepdims=True)
    p_group = 1.0 / gsum
    gidx = jnp.min(jnp.where(gmask & (logits == gmax), lanef, big), axis=-1, keepdims=True)

    lo = EXPERT_LANE0 + EXPERTS_PER_GROUP * gidx
    emask = (lanef >= lo) & (lanef < lo + EXPERTS_PER_GROUP)
    el = jnp.where(emask, logits, NEG)
    t1 = jnp.max(el, axis=-1, keepdims=True)
    i1 = jnp.min(jnp.where(emask & (el == t1), lanef, big), axis=-1, keepdims=True)
    emask2 = emask & (lanef != i1)
    el2 = jnp.where(emask2, logits, NEG)
    t2 = jnp.max(el2, axis=-1, keepdims=True)
    i2 = jnp.min(jnp.where(emask2 & (el2 == t2), lanef, big), axis=-1, keepdims=True)
    e2 = jnp.exp(t2 - t1)
    gate1 = p_group / (1.0 + e2)
    gate2 = p_group * e2 / (1.0 + e2)

    oh1 = lanef == i1
    oh2 = lanef == i2
    oh = jnp.where(oh1 | oh2, 1.0, 0.0)
    tot = _dot(tri_ref[...], oh.astype(BF16)) + run_ref[...]
    rank1 = jnp.sum(jnp.where(oh1, tot, 0.0), axis=-1, keepdims=True)
    rank2 = jnp.sum(jnp.where(oh2, tot, 0.0), axis=-1, keepdims=True)
    run_ref[...] = run_ref[...] + jnp.sum(oh, axis=0, keepdims=True)
    cnt_ref[...] = run_ref[...]

    route = jnp.where(lane == 0, i1 - EXPERT_LANE0, 0.0)
    route = jnp.where(lane == 1, i2 - EXPERT_LANE0, route)
    route = jnp.where(lane == 2, gate1, route)
    route = jnp.where(lane == 3, gate2, route)
    route = jnp.where(lane == 4, rank1, route)
    route = jnp.where(lane == 5, rank2, route)
    route_ref[...] = route


def _mix_ab_kernel(ya_ref, yb_ref, x_ref, w_ref, g_ref, b_ref, wr_ref, br_ref, tri_ref,
                   h_ref, route_ref, cnt_ref, run_ref):
    mix = _dot(ya_ref[...].astype(BF16), w_ref[0:GDN_W, :]) + _dot(yb_ref[...].astype(BF16), w_ref[GDN_W:, :])
    h = _layer_norm(ALPHA * x_ref[...] + mix, g_ref[...], b_ref[...])
    h_ref[...] = h
    _route(h, wr_ref, br_ref, tri_ref, run_ref, route_ref, cnt_ref)


def _mix_c_kernel(o1_ref, o2_ref, o3_ref, l1_ref, l2_ref, l3_ref, expand_ref, x_ref, w_ref, g_ref, b_ref,
                  wr_ref, br_ref, tri_ref, h_ref, route_ref, cnt_ref, run_ref):
    l1, l2, l3 = l1_ref[...], l2_ref[...], l3_ref[...]
    m = jnp.maximum(jnp.maximum(l1, l2), l3)
    e1, e2, e3 = jnp.exp(l1 - m), jnp.exp(l2 - m), jnp.exp(l3 - m)
    inv = 1.0 / (e1 + e2 + e3)
    ex = expand_ref[...]
    o = (_dot(e1 * inv, ex, HI) * o1_ref[...] + _dot(e2 * inv, ex, HI) * o2_ref[...]
         + _dot(e3 * inv, ex, HI) * o3_ref[...])
    mix = _dot(o.astype(BF16), w_ref[...])
    h = _layer_norm(ALPHA * x_ref[...] + mix, g_ref[...], b_ref[...])
    h_ref[...] = h
    _route(h, wr_ref, br_ref, tri_ref, run_ref, route_ref, cnt_ref)


def _router_operands(wg, bg, we, be):
    wr = jnp.zeros((D_MODEL, LANES), F32).at[:, :MOE_GROUPS].set(wg).at[:, EXPERT_LANE0:EXPERT_LANE0 + N_EXPERTS].set(we)
    br = jnp.zeros((1, LANES), F32).at[0, :MOE_GROUPS].set(bg).at[0, EXPERT_LANE0:EXPERT_LANE0 + N_EXPERTS].set(be)
    tri = jnp.tril(jnp.ones((ROW_TILE, ROW_TILE), F32), -1).astype(BF16)
    return wr, br, tri


def _mix_call(kernel_fn, acts, act_widths, x2, w_out, ln_g, ln_b, router, extra=()):
    n = x2.shape[0]
    tm = ROW_TILE
    wr, br, tri = _router_operands(*router)
    row = lambda i: (i, 0)
    const = lambda i: (0, 0)
    in_specs = [pl.BlockSpec((tm, wd), row) for wd in act_widths]
    in_specs += [pl.BlockSpec(e.shape, const) for e in extra]
    in_specs += [pl.BlockSpec((tm, D_MODEL), row), pl.BlockSpec(w_out.shape, const),
                 pl.BlockSpec((1, D_MODEL), const), pl.BlockSpec((1, D_MODEL), const),
                 pl.BlockSpec((D_MODEL, LANES), const), pl.BlockSpec((1, LANES), const),
                 pl.BlockSpec((tm, tm), const)]
    return pl.pallas_call(
        kernel_fn,
        grid=(n // tm,),
        in_specs=in_specs,
        out_specs=[pl.BlockSpec((tm, D_MODEL), row), pl.BlockSpec((tm, LANES), row), pl.BlockSpec((1, LANES), const)],
        out_shape=[jax.ShapeDtypeStruct((n, D_MODEL), F32), jax.ShapeDtypeStruct((n, LANES), F32),
                   jax.ShapeDtypeStruct((1, LANES), F32)],
        scratch_shapes=[pltpu.VMEM((1, LANES), F32)],
        compiler_params=_params("arbitrary"),
    )(*acts, *extra, x2, w_out.astype(BF16), ln_g.reshape(1, D_MODEL), ln_b.reshape(1, D_MODEL), wr, br, tri)


def _slot_tables(route, cnt, n_blocks):
    e1 = route[:, 0].astype(jnp.int32)
    e2 = route[:, 1].astype(jnp.int32)
    rank1 = route[:, 4].astype(jnp.int32)
    rank2 = route[:, 5].astype(jnp.int32)
    counts = cnt[0, EXPERT_LANE0:EXPERT_LANE0 + N_EXPERTS].astype(jnp.int32)
    padded = (counts + MOE_ROWS - 1) // MOE_ROWS * MOE_ROWS
    padded_end = jnp.cumsum(padded)
    padded_start = padded_end - padded
    slot1 = padded_start[e1] + rank1
    slot2 = padded_start[e2] + rank2
    block_expert = jnp.minimum(
        jnp.searchsorted(padded_end, jnp.arange(n_blocks, dtype=jnp.int32) * MOE_ROWS, side='right'),
        N_EXPERTS - 1).astype(jnp.int32)
    n_used = (padded_end[-1:] // MOE_ROWS).astype(jnp.int32)
    return slot1, slot2, block_expert, n_used


def _dispatch_kernel(s1_ref, s2_ref, h_ref, xs_in_ref, xs_ref, sem):
    del xs_in_ref
    tm = h_ref.shape[0]
    base = pl.program_id(0) * tm

    def copy(t, slot_ref):
        return pltpu.make_async_copy(h_ref.at[pl.ds(t, 1)], xs_ref.at[pl.ds(slot_ref[base + t], 1)], sem)

    def issue(t, carry):
        copy(t, s1_ref).start()
        copy(t, s2_ref).start()
        return carry

    def drain(t, carry):
        copy(t, s1_ref).wait()
        copy(t, s2_ref).wait()
        return carry

    lax.fori_loop(0, tm, issue, 0)
    lax.fori_loop(0, tm, drain, 0)


def _dispatch(slot1, slot2, h, n_slots):
    n = h.shape[0]
    tm = GATHER_TILE
    xs0 = jnp.zeros((n_slots, D_MODEL), F32)
    grid_spec = pltpu.PrefetchScalarGridSpec(
        num_scalar_prefetch=2,
        grid=(n // tm,),
        in_specs=[pl.BlockSpec((tm, D_MODEL), lambda i, s1, s2: (i, 0)), pl.BlockSpec(memory_space=pl.ANY)],
        out_specs=pl.BlockSpec(memory_space=pl.ANY),
        scratch_shapes=[pltpu.SemaphoreType.DMA],
    )
    return pl.pallas_call(
        _dispatch_kernel,
        grid_spec=grid_spec,
        out_shape=jax.ShapeDtypeStruct((n_slots, D_MODEL), F32),
        input_output_aliases={3: 0},
        compiler_params=_params("arbitrary"),
    )(slot1, slot2, h, xs0)


def _expert_kernel(be_ref, nu_ref, xs_ref, wg_ref, wu_ref, wd_ref, ys_ref, wgb_ref, wub_ref, wdb_ref):
    i = pl.program_id(0)

    @pl.when((i == 0) | (be_ref[i] != be_ref[jnp.maximum(i - 1, 0)]))
    def _():
        wgb_ref[...] = wg_ref[...].astype(BF16)
        wub_ref[...] = wu_ref[...].astype(BF16)
        wdb_ref[...] = wd_ref[...].astype(BF16)

    @pl.when(i < nu_ref[0])
    def _():
        x = xs_ref[...].astype(BF16)
        hg = _dot(x, wgb_ref[...])
        hu = _dot(x, wub_ref[...])
        hdn = hg * _sigmoid(hg) * hu
        ys_ref[...] = _dot(hdn.astype(BF16), wdb_ref[...])

    @pl.when(i >= nu_ref[0])
    def _():
        ys_ref[...] = jnp.zeros_like(ys_ref)


def _experts(block_expert, n_used, xs, w_gate, w_up, w_down, layer):
    n_slots = xs.shape[0]
    blk = MOE_ROWS
    row = lambda i, be, nu: (i, 0)
    wsel = lambda i, be, nu: (layer, be[i], 0, 0)
    grid_spec = pltpu.PrefetchScalarGridSpec(
        num_scalar_prefetch=2,
        grid=(n_slots // blk,),
        in_specs=[pl.BlockSpec((blk, D_MODEL), row),
                  pl.BlockSpec((None, None, D_MODEL, EXPERT_FF), wsel),
                  pl.BlockSpec((None, None, D_MODEL, EXPERT_FF), wsel),
                  pl.BlockSpec((None, None, EXPERT_FF, D_MODEL), wsel)],
        out_specs=pl.BlockSpec((blk, D_MODEL), row),
        scratch_shapes=[pltpu.VMEM((D_MODEL, EXPERT_FF), BF16), pltpu.VMEM((D_MODEL, EXPERT_FF), BF16),
                        pltpu.VMEM((EXPERT_FF, D_MODEL), BF16)],
    )
    return pl.pallas_call(
        _expert_kernel,
        grid_spec=grid_spec,
        out_shape=jax.ShapeDtypeStruct((n_slots, D_MODEL), F32),
        compiler_params=_params("arbitrary"),
    )(block_expert, n_used, xs, w_gate, w_up, w_down)


def _combine_kernel(s1_ref, s2_ref, h_ref, route_ref, ys_ref, g_ref, b_ref, o_ref, buf1_ref, buf2_ref, sem):
    tm = h_ref.shape[0]
    base = pl.program_id(0) * tm

    def copy(t, slot_ref, buf_ref, k):
        return pltpu.make_async_copy(ys_ref.at[pl.ds(slot_ref[base + t], 1)], buf_ref.at[pl.ds(t, 1)], sem.at[k])

    def issue(t, carry):
        copy(t, s1_ref, buf1_ref, 0).start()
        copy(t, s2_ref, buf2_ref, 1).start()
        return carry

    def drain(t, carry):
        copy(t, s1_ref, buf1_ref, 0).wait()
        copy(t, s2_ref, buf2_ref, 1).wait()
        return carry

    lax.fori_loop(0, tm, issue, 0)
    lax.fori_loop(0, tm, drain, 0)
    r = route_ref[...]
    ffn = buf1_ref[...] * r[:, 2:3] + buf2_ref[...] * r[:, 3:4]
    o_ref[...] = _layer_norm(ALPHA * h_ref[...] + ffn, g_ref[...], b_ref[...])


def _combine(slot1, slot2, h, route, ys, ln_g, ln_b):
    n = h.shape[0]
    tm = GATHER_TILE
    row = lambda i, s1, s2: (i, 0)
    const = lambda i, s1, s2: (0, 0)
    grid_spec = pltpu.PrefetchScalarGridSpec(
        num_scalar_prefetch=2,
        grid=(n // tm,),
        in_specs=[pl.BlockSpec((tm, D_MODEL), row), pl.BlockSpec((tm, LANES), row), pl.BlockSpec(memory_space=pl.ANY),
                  pl.BlockSpec((1, D_MODEL), const), pl.BlockSpec((1, D_MODEL), const)],
        out_specs=pl.BlockSpec((tm, D_MODEL), row),
        scratch_shapes=[pltpu.VMEM((tm, D_MODEL), F32), pltpu.VMEM((tm, D_MODEL), F32), pltpu.SemaphoreType.DMA((2,))],
    )
    return pl.pallas_call(
        _combine_kernel,
        grid_spec=grid_spec,
        out_shape=jax.ShapeDtypeStruct((n, D_MODEL), F32),
        compiler_params=_params("arbitrary"),
    )(slot1, slot2, h, route, ys, ln_g.reshape(1, D_MODEL), ln_b.reshape(1, D_MODEL))


def _moe(h, route, cnt, w_gate, w_up, w_down, ln_g, ln_b, layer):
    n = h.shape[0]
    n_slots = n * 2 + N_EXPERTS * MOE_ROWS
    slot1, slot2, block_expert, n_used = _slot_tables(route, cnt, n_slots // MOE_ROWS)
    xs = _dispatch(slot1, slot2, h, n_slots)
    ys = _experts(block_expert, n_used, xs, w_gate, w_up, w_down, layer)
    return _combine(slot1, slot2, h, route, ys, ln_g, ln_b)


def _qkv_kernel(x_ref, w_ref, o_ref):
    o_ref[...] = _dot(x_ref[...].astype(BF16), w_ref[...]).astype(BF16)


def _qkv_proj(h, w_qkv):
    n = h.shape[0]
    tm = ROW_TILE
    return pl.pallas_call(
        _qkv_kernel,
        grid=(n // tm,),
        in_specs=[pl.BlockSpec((tm, D_MODEL), lambda i: (i, 0)), pl.BlockSpec((D_MODEL, 3 * D_MODEL), lambda i: (0, 0))],
        out_specs=pl.BlockSpec((tm, 3 * D_MODEL), lambda i: (i, 0)),
        out_shape=jax.ShapeDtypeStruct((n, 3 * D_MODEL), BF16),
        compiler_params=_params("arbitrary"),
    )(h, w_qkv.astype(BF16))


def _attn_kernel(q_ref, kc_ref, vc_ref, kp_ref, vp_ref, o_ref, lse_ref):
    t = ATTN_STEPS
    qi = lax.broadcasted_iota(jnp.int32, (t, 2 * t), 0)
    kj = lax.broadcasted_iota(jnp.int32, (t, 2 * t), 1)
    dist = t + qi - kj
    valid = (dist >= 0) & (dist <= t) & ((kj >= t) | (pl.program_id(2) > 0))
    lane = lax.broadcasted_iota(jnp.int32, (t, LANES), 1)
    upper = lane >= ATTN_DIM
    lse_blk = jnp.zeros((t, LANES), F32)
    lanes_per_head = LANES // ATTN_HEADS
    for hp in range(ATTN_HEADS // 2):
        cs = slice(hp * LANES, (hp + 1) * LANES)
        q2 = q_ref[:, cs]
        k2 = jnp.concatenate([kp_ref[:, cs], kc_ref[:, cs]], axis=0)
        v2 = jnp.concatenate([vp_ref[:, cs], vc_ref[:, cs]], axis=0)
        halves = []
        for sub in range(2):
            mine = upper if sub else jnp.logical_not(upper)
            qm = jnp.where(mine, q2, jnp.zeros_like(q2))
            s = lax.dot_general(qm, k2, NT_DIMS, preferred_element_type=F32) * (ATTN_DIM ** -0.5)
            s = jnp.where(valid, s, NEG)
            m = jnp.max(s, axis=-1, keepdims=True)
            p = jnp.exp(s - m)
            l = jnp.sum(p, axis=-1, keepdims=True)
            halves.append(_dot(p.astype(BF16), v2) / l)
            head = 2 * hp + sub
            lse_blk = jnp.where(lane // lanes_per_head == head, m + jnp.log(l), lse_blk)
        o_ref[:, cs] = jnp.where(upper, halves[1], halves[0])
    lse_ref[...] = lse_blk


def _attn_branch(qkv, bsz, seq, dil):
    length = seq // dil
    t = ATTN_STEPS
    nq = length // t
    qkv_v = qkv.reshape(bsz, length, dil * 3 * D_MODEL)
    cur = lambda part: (lambda b, r, i: (b, i, 3 * r + part))
    prev = lambda part: (lambda b, r, i: (b, jnp.maximum(i - 1, 0), 3 * r + part))
    blk = (None, t, D_MODEL)
    o, lse = pl.pallas_call(
        _attn_kernel,
        grid=(bsz, dil, nq),
        in_specs=[pl.BlockSpec(blk, cur(0)), pl.BlockSpec(blk, cur(1)), pl.BlockSpec(blk, cur(2)),
                  pl.BlockSpec(blk, prev(1)), pl.BlockSpec(blk, prev(2))],
        out_specs=[pl.BlockSpec(blk, lambda b, r, i: (b, i, r)), pl.BlockSpec((None, t, LANES), lambda b, r, i: (b, i, r))],
        out_shape=[jax.ShapeDtypeStruct((bsz, length, dil * D_MODEL), F32),
                   jax.ShapeDtypeStruct((bsz, length, dil * LANES), F32)],
        compiler_params=_params("arbitrary", "arbitrary", "arbitrary"),
    )(qkv_v, qkv_v, qkv_v, qkv_v, qkv_v)
    return o.reshape(bsz * seq, D_MODEL), lse.reshape(bsz * seq, LANES)


def kernel(x, w_in_ab, conv_qkv, gdn_a_log, gdn_dt_bias, gdn_norm, s5_lam_re, s5_lam_im, s5_log_dt, s5_b_re, s5_b_im, s5_c_re, s5_c_im, s5_d, s5_w_glu, w_out_ab, w_qkv_c, w_out_c, ln_mix_g, ln_mix_b, router_group_w, router_group_b, router_expert_w, router_expert_b, moe_w_gate, moe_w_up, moe_w_down, ln_ffn_g, ln_ffn_b):
    bsz, seq, d = x.shape
    n = bsz * seq
    x2 = x.reshape(n, d)

    w_in = w_in_ab[0]
    nq = 4 * GDN_W
    ba_cols = jnp.zeros((d, LANES), F32).at[:, :2 * GDN_HEADS].set(w_in[:, nq:nq + 2 * GDN_HEADS])
    w_cat = jnp.concatenate([w_in[:, :nq], w_in[:, nq + 2 * GDN_HEADS:], ba_cols], axis=1).astype(BF16)
    qkv, z, u, ba = _inproj(x2, w_cat)
    ya = _gdn(qkv, z, ba, conv_qkv[0], gdn_a_log[0], gdn_dt_bias[0], gdn_norm[0], bsz, seq)
    u_tb = u.reshape(bsz, seq, S5_W).transpose(1, 0, 2).reshape(n, S5_W)
    yb_tb = _s5(u_tb, s5_lam_re[0], s5_lam_im[0], s5_log_dt[0], s5_b_re[0], s5_b_im[0], s5_c_re[0], s5_c_im[0],
                s5_d[0], s5_w_glu[0], bsz)
    yb = yb_tb.reshape(seq, bsz, S5_W).transpose(1, 0, 2).reshape(n, S5_W)
    router0 = (router_group_w[0], router_group_b[0], router_expert_w[0], router_expert_b[0])
    h, route, cnt = _mix_call(_mix_ab_kernel, (ya, yb), (GDN_W, S5_W), x2, w_out_ab[0], ln_mix_g[0], ln_mix_b[0],
                              router0)
    h = _moe(h, route, cnt, moe_w_gate, moe_w_up, moe_w_down, ln_ffn_g[0], ln_ffn_b[0], 0)

    qkv_c = _qkv_proj(h, w_qkv_c[0])
    outs, lses = [], []
    for dil in DILATIONS:
        o, lse = _attn_branch(qkv_c, bsz, seq, dil)
        outs.append(o)
        lses.append(lse)
    lanes_per_head = LANES // ATTN_HEADS
    expand = (jnp.arange(LANES)[:, None] == (jnp.arange(D_MODEL)[None, :] // ATTN_DIM) * lanes_per_head).astype(F32)
    router1 = (router_group_w[1], router_group_b[1], router_expert_w[1], router_expert_b[1])
    h, route, cnt = _mix_call(_mix_c_kernel, (*outs, *lses), (D_MODEL,) * 3 + (LANES,) * 3, h, w_out_c[0],
                              ln_mix_g[1], ln_mix_b[1], router1, extra=(expand,))
    h = _moe(h, route, cnt, moe_w_gate, moe_w_up, moe_w_down, ln_ffn_g[1], ln_ffn_b[1], 1)
    return h.reshape(bsz, seq, d)
```

```python
import functools
import math

import jax
import jax.numpy as jnp
from jax import lax
from jax.experimental import pallas as pl
from jax.experimental.pallas import tpu as pltpu

F32 = jnp.float32
BF16 = jnp.bfloat16
HI = lax.Precision.HIGHEST

D_MODEL = 1024
DEPTH = 2
ALPHA = (2 * DEPTH) ** 0.25
LN_EPS = 1e-5

GDN_HEADS = 4
GDN_DIM = 128
GDN_W = GDN_HEADS * GDN_DIM
CONV_K = 4
GDN_CHUNK = 64
GDN_BATCH = 4

S5_W = 512
S5_GROUP_CH = 16
S5_GROUPS = 32
S5_STATE = 64
S5_SUPER = 4
S5_SUPER_CH = S5_W // S5_SUPER
S5_SUPER_ST = S5_GROUPS * S5_STATE // S5_SUPER

ATTN_HEADS = 16
ATTN_DIM = 64
ATTN_STEPS = 128
DILATIONS = (1, 4, 16)

MOE_GROUPS = 4
EXPERTS_PER_GROUP = 8
N_EXPERTS = 32
EXPERT_FF = 512
EXPERT_LANE0 = MOE_GROUPS

LANES = 128
SUBLANES = 8
VMEM_LIMIT = 56 * 1024 * 1024

ROW_TILE = 512
S5_STEPS = 128
MOE_ROWS = 512
GATHER_TILE = 256

NEG = -1e30
NT_DIMS = (((1,), (1,)), ((), ()))
TN_DIMS = (((0,), (0,)), ((), ()))


def _params(*sem):
    return pltpu.CompilerParams(dimension_semantics=sem, vmem_limit_bytes=VMEM_LIMIT)


def _sigmoid(x):
    return 1.0 / (1.0 + jnp.exp(-x))


def _softplus(x):
    return jnp.maximum(x, 0.0) + jnp.log(1.0 + jnp.exp(-jnp.abs(x)))


def _layer_norm(r, g, b):
    mu = jnp.mean(r, axis=-1, keepdims=True)
    c = r - mu
    var = jnp.mean(c * c, axis=-1, keepdims=True)
    return c * lax.rsqrt(var + LN_EPS) * g + b


def _dot(a, b, precision=None):
    return jnp.dot(a, b, precision=precision, preferred_element_type=F32)


def _mm(a, b, dims=None):
    a, b = a.astype(BF16), b.astype(BF16)
    if dims is None:
        return jnp.dot(a, b, preferred_element_type=F32)
    return lax.dot_general(a, b, dims, preferred_element_type=F32)


def _split_bf16(x):
    hi = x.astype(BF16)
    return hi, (x - hi.astype(F32)).astype(BF16)


IN_COLS = 3 * GDN_W + GDN_W + S5_W + LANES


def _inproj_kernel(x_ref, w_ref, qkv_ref, z_ref, u_ref, ba_ref):
    x = x_ref[...].astype(BF16)
    qkv_ref[...] = _dot(x, w_ref[:, 0:1536])
    z_ref[...] = _dot(x, w_ref[:, 1536:2048])
    u_ref[...] = _dot(x, w_ref[:, 2048:2560])
    ba_ref[...] = _dot(x, w_ref[:, 2560:2688])


def _inproj(x2, w_cat):
    n = x2.shape[0]
    tm = ROW_TILE
    row = lambda i: (i, 0)
    return pl.pallas_call(
        _inproj_kernel,
        grid=(n // tm,),
        in_specs=[pl.BlockSpec((tm, D_MODEL), row), pl.BlockSpec((D_MODEL, IN_COLS), lambda i: (0, 0))],
        out_specs=[pl.BlockSpec((tm, 3 * GDN_W), row), pl.BlockSpec((tm, GDN_W), row),
                   pl.BlockSpec((tm, S5_W), row), pl.BlockSpec((tm, LANES), row)],
        out_shape=[jax.ShapeDtypeStruct((n, 3 * GDN_W), F32), jax.ShapeDtypeStruct((n, GDN_W), F32),
                   jax.ShapeDtypeStruct((n, S5_W), F32), jax.ShapeDtypeStruct((n, LANES), F32)],
        compiler_params=_params("arbitrary"),
    )(x2, w_cat)


def _l2norm(t):
    return t * lax.rsqrt(jnp.sum(t * t, axis=-1, keepdims=True) + 1e-6)


def _each(fn, *lists):
    return [fn(*args) for args in zip(*lists)]


def _unit_lower_inverse(lows, same16, cross32, cross64, eye):
    ps = [jnp.where(same16, low, 0.0) for low in lows]
    ts = [eye - p for p in ps]
    for _ in range(3):
        ps = _each(lambda p: _mm(p, p), ps)
        ts = _each(lambda t, p: _mm(t, eye + p), ts, ps)
    for cross in (cross32, cross64):
        tc = _each(lambda t, low: _mm(t, jnp.where(cross, low, 0.0)), ts, lows)
        ts = _each(lambda t, x: t - _mm(x, t), ts, tc)
    return ts


def _gdn_kernel(qkv_ref, z_ref, ba_ref, convw_ref, alog_ref, dtb_ref, normw_ref, tri_ref, o_ref,
                state_ref, tail_ref):
    c = GDN_CHUNK

    @pl.when(pl.program_id(1) == 0)
    def _():
        state_ref[...] = jnp.zeros_like(state_ref)
        tail_ref[...] = jnp.zeros_like(tail_ref)

    w = convw_ref[...]
    row = lax.broadcasted_iota(jnp.int32, (c, 1), 0)
    ri = lax.broadcasted_iota(jnp.int32, (c, c), 0)
    ci = lax.broadcasted_iota(jnp.int32, (c, c), 1)
    causal = ri >= ci
    strict = ri > ci
    eye = (ri == ci).astype(F32)
    same16 = (ri // 16) == (ci // 16)
    same32 = (ri // 32) == (ci // 32)
    cross32 = same32 & jnp.logical_not(same16)
    cross64 = jnp.logical_not(same32)
    normw = normw_ref[...]
    a_neg = -jnp.exp(alog_ref[...])
    zpad = jnp.zeros((c - SUBLANES, 3 * GDN_W), F32)
    head_rows = (lax.broadcasted_iota(jnp.int32, (SUBLANES, LANES), 1)
                 == lax.broadcasted_iota(jnp.int32, (SUBLANES, LANES), 0) + GDN_HEADS).astype(F32)

    chains, qs, ks, kbs, vbs, kws, decays, egcs, kdecs, glasts = [], [], [], [], [], [], [], [], [], []
    for bi in range(GDN_BATCH):
        x = qkv_ref[bi]
        tail = tail_ref[bi]
        conv = x * w[CONV_K - 1:CONV_K, :]
        for j in range(1, CONV_K):
            cur = pltpu.roll(x, j, axis=0)
            prev = jnp.concatenate([pltpu.roll(tail, j, axis=0), zpad], axis=0)
            conv = conv + jnp.where(row < j, prev, cur) * w[CONV_K - 1 - j:CONV_K - j, :]
        tail_ref[bi] = x[c - SUBLANES:, :]
        qkv = conv * _sigmoid(conv)

        ba = ba_ref[bi]
        beta_all = _sigmoid(ba)
        g_all = a_neg * _softplus(ba + dtb_ref[...])
        gc_all = _dot(tri_ref[...], g_all, HI)
        egc_all = jnp.exp(gc_all)
        gc_rows = lax.dot_general(head_rows, gc_all, NT_DIMS, precision=HI, preferred_element_type=F32)

        for h in range(GDN_HEADS):
            q = _l2norm(qkv[:, h * GDN_DIM:(h + 1) * GDN_DIM]) * (GDN_DIM ** -0.5)
            k = _l2norm(qkv[:, GDN_W + h * GDN_DIM:GDN_W + (h + 1) * GDN_DIM])
            v = qkv[:, 2 * GDN_W + h * GDN_DIM:2 * GDN_W + (h + 1) * GDN_DIM]
            gl = GDN_HEADS + h
            beta = beta_all[:, h:h + 1]
            gc = gc_all[:, gl:gl + 1]
            egc = egc_all[:, gl:gl + 1]
            gc_last = gc_all[c - 1:c, gl:gl + 1]
            kb = k * beta
            chains.append((bi, h))
            qs.append(q)
            ks.append(k)
            kbs.append(kb)
            vbs.append(v * beta)
            kws.append(kb * egc)
            decays.append(jnp.exp(jnp.where(causal, gc - gc_rows[h:h + 1, :], NEG)))
            egcs.append(egc)
            kdecs.append(k * jnp.exp(gc_last - gc))
            glasts.append(jnp.exp(gc_last))

    kk = _each(lambda kb, q, k: _mm(jnp.concatenate([kb, q], axis=0), k, NT_DIMS), kbs, qs, ks)
    lows = _each(lambda a, dec: jnp.where(strict, a[:c] * dec, 0.0), kk, decays)
    qks = _each(lambda a, dec: jnp.where(causal, a[c:] * dec, 0.0), kk, decays)
    ts = _unit_lower_inverse(lows, same16, cross32, cross64, eye)
    sols = _each(lambda t, vb, kw: _mm(t, jnp.concatenate([vb, kw], axis=1)), ts, vbs, kws)
    states = [state_ref[bi, h] for bi, h in chains]
    wss = _each(lambda sol, q, egc, s: _mm(jnp.concatenate([sol[:, GDN_DIM:], q * egc], axis=0), s),
                sols, qs, egcs, states)
    v_news = _each(lambda sol, ws: sol[:, :GDN_DIM] - ws[:c], sols, wss)
    outs = _each(lambda ws, qk, v_new: ws[c:] + _mm(qk, v_new), wss, qks, v_news)
    upds = _each(lambda kd, v_new: _mm(kd, v_new, TN_DIMS), kdecs, v_news)
    for (bi, h), s, gl, upd, o in zip(chains, states, glasts, upds, outs):
        hs = slice(h * GDN_DIM, (h + 1) * GDN_DIM)
        state_ref[bi, h] = s * gl + upd
        zh = z_ref[bi, :, hs]
        o_ref[bi, :, hs] = o * lax.rsqrt(jnp.mean(o * o, axis=-1, keepdims=True) + 1e-6) * normw * (zh * _sigmoid(zh))


def _gdn(qkv, z, ba, conv_w, a_log, dt_bias, norm_w, bsz, seq):
    c = GDN_CHUNK
    nb = GDN_BATCH
    convw = jnp.zeros((SUBLANES, 3 * GDN_W), F32).at[:CONV_K].set(conv_w)
    alog = jnp.zeros((1, LANES), F32).at[0, GDN_HEADS:2 * GDN_HEADS].set(a_log)
    dtb = jnp.zeros((1, LANES), F32).at[0, GDN_HEADS:2 * GDN_HEADS].set(dt_bias)
    tri = jnp.tril(jnp.ones((c, c), F32))
    row = lambda b, i: (b, i, 0)
    const = lambda b, i: (0, 0)
    out = pl.pallas_call(
        _gdn_kernel,
        grid=(bsz // nb, seq // c),
        in_specs=[pl.BlockSpec((nb, c, 3 * GDN_W), row), pl.BlockSpec((nb, c, GDN_W), row),
                  pl.BlockSpec((nb, c, LANES), row),
                  pl.BlockSpec((SUBLANES, 3 * GDN_W), const), pl.BlockSpec((1, LANES), const),
                  pl.BlockSpec((1, LANES), const), pl.BlockSpec((1, GDN_DIM), const), pl.BlockSpec((c, c), const)],
        out_specs=pl.BlockSpec((nb, c, GDN_W), row),
        out_shape=jax.ShapeDtypeStruct((bsz, seq, GDN_W), F32),
        scratch_shapes=[pltpu.VMEM((nb, GDN_HEADS, GDN_DIM, GDN_DIM), F32),
                        pltpu.VMEM((nb, SUBLANES, 3 * GDN_W), F32)],
        compiler_params=_params("arbitrary", "arbitrary"),
    )(qkv.reshape(bsz, seq, 3 * GDN_W), z.reshape(bsz, seq, GDN_W), ba.reshape(bsz, seq, LANES),
      convw, alog, dtb, norm_w.reshape(1, GDN_DIM), tri)
    return out.reshape(bsz * seq, GDN_W)


def _gelu_tanh(x):
    return x * (0.5 * (1.0 + jnp.tanh(math.sqrt(2.0 / math.pi) * (x + 0.044715 * (x * x * x)))))


def _s5_kernel(u_ref, wb_ref, a_ref, wc_ref, d_ref, wglu_ref, y_ref, utb_ref, ytb_ref, bu_ref, h_ref, *, steps):
    @pl.when(pl.program_id(0) == 0)
    def _():
        h_ref[...] = jnp.zeros_like(h_ref)

    nb = h_ref.shape[0]
    nlc = S5_W // LANES
    for b in range(nb):
        for lc in range(nlc):
            utb_ref[lc, pl.ds(b, steps, stride=nb), :] = u_ref[b, :, lc * LANES:(lc + 1) * LANES]
    u = jnp.concatenate([utb_ref[lc] for lc in range(nlc)], axis=1)
    ub = u.astype(BF16)
    st = S5_SUPER_ST
    for j in range(S5_SUPER):
        bu_ref[:, 2 * st * j:2 * st * (j + 1)] = _dot(ub[:, S5_SUPER_CH * j:S5_SUPER_CH * (j + 1)], wb_ref[j])

    for j in range(S5_SUPER):
        re = slice(2 * st * j, 2 * st * j + st)
        im = slice(2 * st * j + st, 2 * st * (j + 1))
        ar = jnp.broadcast_to(a_ref[0:1, st * j:st * (j + 1)], (nb, st))
        ai = jnp.broadcast_to(a_ref[1:2, st * j:st * (j + 1)], (nb, st))

        def body(t, carry, re=re, im=im, ar=ar, ai=ai):
            hr, hi = carry
            rows = pl.ds(pl.multiple_of(t * nb, nb), nb)
            nr = ar * hr - ai * hi + bu_ref[rows, re]
            ni = ar * hi + ai * hr + bu_ref[rows, im]
            bu_ref[rows, re] = nr
            bu_ref[rows, im] = ni
            return nr, ni

        hr, hi = lax.fori_loop(0, steps, body, (h_ref[:, re], h_ref[:, im]), unroll=4)
        h_ref[:, re] = hr
        h_ref[:, im] = hi

    y = jnp.concatenate(
        [_dot(bu_ref[:, 2 * st * j:2 * st * (j + 1)].astype(BF16), wc_ref[j]) for j in range(S5_SUPER)], axis=1)
    y = _gelu_tanh(y + d_ref[...] * u)
    y = y * _sigmoid(_dot(y.astype(BF16), wglu_ref[...]))
    for lc in range(nlc):
        ytb_ref[lc] = y[:, lc * LANES:(lc + 1) * LANES]
    for b in range(nb):
        for lc in range(nlc):
            y_ref[b, :, lc * LANES:(lc + 1) * LANES] = ytb_ref[lc, pl.ds(b, steps, stride=nb), :]


def _s5(u3, lam_re, lam_im, log_dt, b_re, b_im, c_re, c_im, d_skip, w_glu):
    bsz, seq, _ = u3.shape
    dt = jnp.exp(log_dt)[:, None]
    mag = jnp.exp(lam_re * dt)
    a_re, a_im = mag * jnp.cos(lam_im * dt), mag * jnp.sin(lam_im * dt)
    den = lam_re * lam_re + lam_im * lam_im
    nr, ni = a_re - 1.0, a_im
    cr = (nr * lam_re + ni * lam_im) / den
    ci = (ni * lam_re - nr * lam_im) / den
    bb_re = cr[..., None] * b_re - ci[..., None] * b_im
    bb_im = cr[..., None] * b_im + ci[..., None] * b_re
    gps = S5_GROUPS // S5_SUPER
    eye = jnp.eye(gps, dtype=F32)

    def in_blockdiag(t):
        t = t.reshape(S5_SUPER, gps, S5_STATE, S5_GROUP_CH)
        return jnp.einsum('jgph,gk->jghkp', t, eye).reshape(S5_SUPER, S5_SUPER_CH, S5_SUPER_ST)

    def out_blockdiag(t):
        t = t.reshape(S5_SUPER, gps, S5_GROUP_CH, S5_STATE)
        return jnp.einsum('jgkp,gm->jgpmk', t, eye).reshape(S5_SUPER, S5_SUPER_ST, S5_SUPER_CH)

    wb = jnp.concatenate([in_blockdiag(bb_re), in_blockdiag(bb_im)], axis=2).astype(BF16)
    wc = jnp.concatenate([out_blockdiag(c_re), -out_blockdiag(c_im)], axis=1).astype(BF16)
    nstate = S5_GROUPS * S5_STATE
    a = jnp.stack([a_re.reshape(nstate), a_im.reshape(nstate)], axis=0)

    steps = S5_STEPS
    rows = steps * bsz
    blk = lambda i: (0, i, 0)
    const2 = lambda i: (0, 0)
    const3 = lambda i: (0, 0, 0)
    return pl.pallas_call(
        functools.partial(_s5_kernel, steps=steps),
        grid=(seq // steps,),
        in_specs=[pl.BlockSpec((bsz, steps, S5_W), blk), pl.BlockSpec(wb.shape, const3),
                  pl.BlockSpec((2, nstate), const2), pl.BlockSpec(wc.shape, const3),
                  pl.BlockSpec((1, S5_W), const2), pl.BlockSpec((S5_W, S5_W), const2)],
        out_specs=pl.BlockSpec((bsz, steps, S5_W), blk),
        out_shape=jax.ShapeDtypeStruct((bsz, seq, S5_W), F32),
        scratch_shapes=[pltpu.VMEM((S5_W // LANES, rows, LANES), F32), pltpu.VMEM((S5_W // LANES, rows, LANES), F32),
                        pltpu.VMEM((rows, 2 * nstate), F32), pltpu.VMEM((bsz, 2 * nstate), F32)],
        compiler_params=_params("arbitrary"),
    )(u3, wb, a, wc, d_skip.reshape(1, S5_W), w_glu.astype(BF16))


def _route(h, wr_ref, br_ref, tri_ref, run_ref, route_ref, cnt_ref):
    @pl.when(pl.program_id(0) == 0)
    def _():
        run_ref[...] = jnp.zeros_like(run_ref)

    tm = h.shape[0]
    h_hi, h_lo = _split_bf16(h)
    both = _dot(h_hi, wr_ref[...])
    logits = both[:, :LANES] + both[:, LANES:] + _dot(h_lo, wr_ref[:, :LANES]) + br_ref[...]
    lane = lax.broadcasted_iota(jnp.int32, (tm, LANES), 1)
    lanef = lane.astype(F32)
    big = float(LANES)

    gmask = lane < MOE_GROUPS
    gmax = jnp.max(jnp.where(gmask, logits, NEG), axis=-1, keepdims=True)
    gsum = jnp.sum(jnp.where(gmask, jnp.exp(logits - gmax), 0.0), axis=-1, keepdims=True)
    p_group = 1.0 / gsum
    gidx = jnp.min(jnp.where(gmask & (logits == gmax), lanef, big), axis=-1, keepdims=True)

    lo = EXPERT_LANE0 + EXPERTS_PER_GROUP * gidx
    emask = (lanef >= lo) & (lanef < lo + EXPERTS_PER_GROUP)
    el = jnp.where(emask, logits, NEG)
    t1 = jnp.max(el, axis=-1, keepdims=True)
    i1 = jnp.min(jnp.where(emask & (el == t1), lanef, big), axis=-1, keepdims=True)
    emask2 = emask & (lanef != i1)
    el2 = jnp.where(emask2, logits, NEG)
    t2 = jnp.max(el2, axis=-1, keepdims=True)
    i2 = jnp.min(jnp.where(emask2 & (el2 == t2), lanef, big), axis=-1, keepdims=True)
    e2 = jnp.exp(t2 - t1)
    gate1 = p_group / (1.0 + e2)
    gate2 = p_group * e2 / (1.0 + e2)

    oh1 = lanef == i1
    oh2 = lanef == i2
    oh = jnp.where(oh1 | oh2, 1.0, 0.0)
    tot = _dot(tri_ref[...], oh.astype(BF16)) + run_ref[...]
    rank1 = jnp.sum(jnp.where(oh1, tot, 0.0), axis=-1, keepdims=True)
    rank2 = jnp.sum(jnp.where(oh2, tot, 0.0), axis=-1, keepdims=True)
    run_ref[...] = run_ref[...] + jnp.sum(oh, axis=0, keepdims=True)
    cnt_ref[...] = run_ref[...]

    route = jnp.where(lane == 0, i1 - EXPERT_LANE0, 0.0)
    route = jnp.where(lane == 1, i2 - EXPERT_LANE0, route)
    route = jnp.where(lane == 2, gate1, route)
    route = jnp.where(lane == 3, gate2, route)
    route = jnp.where(lane == 4, rank1, route)
    route = jnp.where(lane == 5, rank2, route)
    route_ref[...] = route


def _mix_ab_kernel(ya_ref, yb_ref, x_ref, w_ref, g_ref, b_ref, wr_ref, br_ref, tri_ref,
                   h_ref, route_ref, cnt_ref, run_ref):
    mix = _dot(ya_ref[...].astype(BF16), w_ref[0:GDN_W, :]) + _dot(yb_ref[...].astype(BF16), w_ref[GDN_W:, :])
    h = _layer_norm(ALPHA * x_ref[...] + mix, g_ref[...], b_ref[...])
    h_ref[...] = h
    _route(h, wr_ref, br_ref, tri_ref, run_ref, route_ref, cnt_ref)


def _to_natural(view_ref, nat_ref, dil):
    rows = view_ref.shape[0]
    nlc = nat_ref.shape[0]
    for r in range(dil):
        for lc in range(nlc):
            col = (r * nlc + lc) * LANES
            nat_ref[lc, pl.ds(r, rows, stride=dil), :] = view_ref[:, col:col + LANES]
    return jnp.concatenate([nat_ref[lc] for lc in range(nlc)], axis=1)


def _mix_c_kernel(o1_ref, o4_ref, o16_ref, l1_ref, l4_ref, l16_ref, expand_ref, x_ref, w_ref, g_ref, b_ref,
                  wr_ref, br_ref, tri_ref, h_ref, route_ref, cnt_ref, run_ref, on4_ref, on16_ref, ln4_ref, ln16_ref):
    o4 = _to_natural(o4_ref, on4_ref, DILATIONS[1])
    o16 = _to_natural(o16_ref, on16_ref, DILATIONS[2])
    l2 = _to_natural(l4_ref, ln4_ref, DILATIONS[1])
    l3 = _to_natural(l16_ref, ln16_ref, DILATIONS[2])
    l1 = l1_ref[...]
    m = jnp.maximum(jnp.maximum(l1, l2), l3)
    e1, e2, e3 = jnp.exp(l1 - m), jnp.exp(l2 - m), jnp.exp(l3 - m)
    inv = 1.0 / (e1 + e2 + e3)
    ex = expand_ref[...]

    def spread(wt):
        hi, lo = _split_bf16(wt)
        return _dot(hi, ex) + _dot(lo, ex)

    o = spread(e1 * inv) * o1_ref[...] + spread(e2 * inv) * o4 + spread(e3 * inv) * o16
    mix = _dot(o.astype(BF16), w_ref[...])
    h = _layer_norm(ALPHA * x_ref[...] + mix, g_ref[...], b_ref[...])
    h_ref[...] = h
    _route(h, wr_ref, br_ref, tri_ref, run_ref, route_ref, cnt_ref)


def _router_operands(wg, bg, we, be):
    wr = jnp.zeros((D_MODEL, LANES), F32).at[:, :MOE_GROUPS].set(wg).at[:, EXPERT_LANE0:EXPERT_LANE0 + N_EXPERTS].set(we)
    wr_hi = wr.astype(BF16)
    wr_lo = (wr - wr_hi.astype(F32)).astype(BF16)
    br = jnp.zeros((1, LANES), F32).at[0, :MOE_GROUPS].set(bg).at[0, EXPERT_LANE0:EXPERT_LANE0 + N_EXPERTS].set(be)
    tri = jnp.tril(jnp.ones((ROW_TILE, ROW_TILE), F32), -1).astype(BF16)
    return jnp.concatenate([wr_hi, wr_lo], axis=1), br, tri


def _mix_call(kernel_fn, acts, x2, w_out, ln_g, ln_b, router, extra=(), scratch=()):
    n = x2.shape[0]
    tm = ROW_TILE
    wr, br, tri = _router_operands(*router)
    row = lambda i: (i, 0)
    const = lambda i: (0, 0)
    in_specs = [pl.BlockSpec((rows, a.shape[1]), row) for a, rows in acts]
    in_specs += [pl.BlockSpec(e.shape, const) for e in extra]
    in_specs += [pl.BlockSpec((tm, D_MODEL), row), pl.BlockSpec(w_out.shape, const),
                 pl.BlockSpec((1, D_MODEL), const), pl.BlockSpec((1, D_MODEL), const),
                 pl.BlockSpec((D_MODEL, 2 * LANES), const), pl.BlockSpec((1, LANES), const),
                 pl.BlockSpec((tm, tm), const)]
    return pl.pallas_call(
        kernel_fn,
        grid=(n // tm,),
        in_specs=in_specs,
        out_specs=[pl.BlockSpec((tm, D_MODEL), row), pl.BlockSpec((tm, LANES), row), pl.BlockSpec((1, LANES), const)],
        out_shape=[jax.ShapeDtypeStruct((n, D_MODEL), F32), jax.ShapeDtypeStruct((n, LANES), F32),
                   jax.ShapeDtypeStruct((1, LANES), F32)],
        scratch_shapes=[pltpu.VMEM((1, LANES), F32), *scratch],
        compiler_params=_params("arbitrary"),
    )(*[a for a, _ in acts], *extra, x2, w_out.astype(BF16), ln_g.reshape(1, D_MODEL), ln_b.reshape(1, D_MODEL),
      wr, br, tri)


def _slot_kernel(route_ref, pstart_ref, o_ref):
    r = route_ref[...]
    tm = r.shape[0]
    lane = lax.broadcasted_iota(jnp.int32, (tm, LANES), 1)
    lanef = lane.astype(F32)
    ps = pstart_ref[...]
    s1 = jnp.sum(jnp.where(lanef == r[:, 0:1] + EXPERT_LANE0, ps, 0.0), axis=-1, keepdims=True) + r[:, 4:5]
    s2 = jnp.sum(jnp.where(lanef == r[:, 1:2] + EXPERT_LANE0, ps, 0.0), axis=-1, keepdims=True) + r[:, 5:6]
    vals = jnp.where(lane == 0, s1, jnp.where(lane == 1, s2, 0.0))
    pick = (lax.broadcasted_iota(jnp.int32, (SUBLANES, LANES), 0)
            == lax.broadcasted_iota(jnp.int32, (SUBLANES, LANES), 1)).astype(F32)
    o_ref[...] = lax.dot_general(pick, vals, NT_DIMS, precision=HI, preferred_element_type=F32).astype(jnp.int32)


def _slot_tables(route, cnt, n_blocks):
    n = route.shape[0]
    tm = ROW_TILE
    counts = cnt[0, EXPERT_LANE0:EXPERT_LANE0 + N_EXPERTS].astype(jnp.int32)
    padded = (counts + MOE_ROWS - 1) // MOE_ROWS * MOE_ROWS
    padded_end = jnp.cumsum(padded)
    padded_start = padded_end - padded
    pstart = jnp.zeros((1, LANES), F32).at[0, EXPERT_LANE0:EXPERT_LANE0 + N_EXPERTS].set(padded_start.astype(F32))
    slots = pl.pallas_call(
        _slot_kernel,
        grid=(n // tm,),
        in_specs=[pl.BlockSpec((tm, LANES), lambda i: (i, 0)), pl.BlockSpec((1, LANES), lambda i: (0, 0))],
        out_specs=pl.BlockSpec((SUBLANES, tm), lambda i: (0, i)),
        out_shape=jax.ShapeDtypeStruct((SUBLANES, n), jnp.int32),
        compiler_params=_params("arbitrary"),
    )(route, pstart)
    block_start = jnp.arange(n_blocks, dtype=jnp.int32) * MOE_ROWS
    block_expert = jnp.minimum(jnp.sum((padded_end[None, :] <= block_start[:, None]).astype(jnp.int32), axis=1),
                               N_EXPERTS - 1)
    n_used = (padded_end[-1:] // MOE_ROWS).astype(jnp.int32)
    return slots[0], slots[1], block_expert, n_used


def _dispatch_kernel(s1_ref, s2_ref, h_ref, xs_in_ref, xs_ref, sem):
    del xs_in_ref
    tm = h_ref.shape[0]
    base = pl.program_id(0) * tm

    def copy(t, slot_ref):
        return pltpu.make_async_copy(h_ref.at[pl.ds(t, 1)], xs_ref.at[pl.ds(slot_ref[base + t], 1)], sem)

    def issue(t, carry):
        copy(t, s1_ref).start()
        copy(t, s2_ref).start()
        return carry

    def drain(t, carry):
        copy(t, s1_ref).wait()
        copy(t, s2_ref).wait()
        return carry

    lax.fori_loop(0, tm, issue, 0)
    lax.fori_loop(0, tm, drain, 0)


def _dispatch(slot1, slot2, h, n_slots):
    n = h.shape[0]
    tm = GATHER_TILE
    xs0 = jnp.zeros((n_slots, D_MODEL), F32)
    grid_spec = pltpu.PrefetchScalarGridSpec(
        num_scalar_prefetch=2,
        grid=(n // tm,),
        in_specs=[pl.BlockSpec((tm, D_MODEL), lambda i, s1, s2: (i, 0)), pl.BlockSpec(memory_space=pl.ANY)],
        out_specs=pl.BlockSpec(memory_space=pl.ANY),
        scratch_shapes=[pltpu.SemaphoreType.DMA],
    )
    return pl.pallas_call(
        _dispatch_kernel,
        grid_spec=grid_spec,
        out_shape=jax.ShapeDtypeStruct((n_slots, D_MODEL), F32),
        input_output_aliases={3: 0},
        compiler_params=_params("arbitrary"),
    )(slot1, slot2, h, xs0)


def _expert_kernel(be_ref, nu_ref, xs_ref, wg_ref, wu_ref, wd_ref, ys_ref, wgb_ref, wub_ref, wdb_ref):
    i = pl.program_id(0)

    @pl.when((i == 0) | (be_ref[i] != be_ref[jnp.maximum(i - 1, 0)]))
    def _():
        wgb_ref[...] = wg_ref[...].astype(BF16)
        wub_ref[...] = wu_ref[...].astype(BF16)
        wdb_ref[...] = wd_ref[...].astype(BF16)

    @pl.when(i < nu_ref[0])
    def _():
        x = xs_ref[...].astype(BF16)
        hg = _dot(x, wgb_ref[...])
        hu = _dot(x, wub_ref[...])
        hdn = hg * _sigmoid(hg) * hu
        ys_ref[...] = _dot(hdn.astype(BF16), wdb_ref[...])

    @pl.when(i >= nu_ref[0])
    def _():
        ys_ref[...] = jnp.zeros_like(ys_ref)


def _experts(block_expert, n_used, xs, w_gate, w_up, w_down, layer):
    n_slots = xs.shape[0]
    blk = MOE_ROWS
    row = lambda i, be, nu: (i, 0)
    wsel = lambda i, be, nu: (layer, be[i], 0, 0)
    grid_spec = pltpu.PrefetchScalarGridSpec(
        num_scalar_prefetch=2,
        grid=(n_slots // blk,),
        in_specs=[pl.BlockSpec((blk, D_MODEL), row),
                  pl.BlockSpec((None, None, D_MODEL, EXPERT_FF), wsel),
                  pl.BlockSpec((None, None, D_MODEL, EXPERT_FF), wsel),
                  pl.BlockSpec((None, None, EXPERT_FF, D_MODEL), wsel)],
        out_specs=pl.BlockSpec((blk, D_MODEL), row),
        scratch_shapes=[pltpu.VMEM((D_MODEL, EXPERT_FF), BF16), pltpu.VMEM((D_MODEL, EXPERT_FF), BF16),
                        pltpu.VMEM((EXPERT_FF, D_MODEL), BF16)],
    )
    return pl.pallas_call(
        _expert_kernel,
        grid_spec=grid_spec,
        out_shape=jax.ShapeDtypeStruct((n_slots, D_MODEL), F32),
        compiler_params=_params("arbitrary"),
    )(block_expert, n_used, xs, w_gate, w_up, w_down)


def _combine_kernel(s1_ref, s2_ref, h_ref, route_ref, ys_ref, g_ref, b_ref, o_ref, buf1_ref, buf2_ref, sem):
    tm = h_ref.shape[0]
    base = pl.program_id(0) * tm

    def copy(t, slot_ref, buf_ref, k):
        return pltpu.make_async_copy(ys_ref.at[pl.ds(slot_ref[base + t], 1)], buf_ref.at[pl.ds(t, 1)], sem.at[k])

    def issue(t, carry):
        copy(t, s1_ref, buf1_ref, 0).start()
        copy(t, s2_ref, buf2_ref, 1).start()
        return carry

    def drain(t, carry):
        copy(t, s1_ref, buf1_ref, 0).wait()
        copy(t, s2_ref, buf2_ref, 1).wait()
        return carry

    lax.fori_loop(0, tm, issue, 0)
    lax.fori_loop(0, tm, drain, 0)
    r = route_ref[...]
    ffn = buf1_ref[...] * r[:, 2:3] + buf2_ref[...] * r[:, 3:4]
    o_ref[...] = _layer_norm(ALPHA * h_ref[...] + ffn, g_ref[...], b_ref[...])


def _combine(slot1, slot2, h, route, ys, ln_g, ln_b):
    n = h.shape[0]
    tm = GATHER_TILE
    row = lambda i, s1, s2: (i, 0)
    const = lambda i, s1, s2: (0, 0)
    grid_spec = pltpu.PrefetchScalarGridSpec(
        num_scalar_prefetch=2,
        grid=(n // tm,),
        in_specs=[pl.BlockSpec((tm, D_MODEL), row), pl.BlockSpec((tm, LANES), row), pl.BlockSpec(memory_space=pl.ANY),
                  pl.BlockSpec((1, D_MODEL), const), pl.BlockSpec((1, D_MODEL), const)],
        out_specs=pl.BlockSpec((tm, D_MODEL), row),
        scratch_shapes=[pltpu.VMEM((tm, D_MODEL), F32), pltpu.VMEM((tm, D_MODEL), F32), pltpu.SemaphoreType.DMA((2,))],
    )
    return pl.pallas_call(
        _combine_kernel,
        grid_spec=grid_spec,
        out_shape=jax.ShapeDtypeStruct((n, D_MODEL), F32),
        compiler_params=_params("arbitrary"),
    )(slot1, slot2, h, route, ys, ln_g.reshape(1, D_MODEL), ln_b.reshape(1, D_MODEL))


def _moe(h, route, cnt, w_gate, w_up, w_down, ln_g, ln_b, layer):
    n = h.shape[0]
    n_slots = n * 2 + N_EXPERTS * MOE_ROWS
    slot1, slot2, block_expert, n_used = _slot_tables(route, cnt, n_slots // MOE_ROWS)
    xs = _dispatch(slot1, slot2, h, n_slots)
    ys = _experts(block_expert, n_used, xs, w_gate, w_up, w_down, layer)
    return _combine(slot1, slot2, h, route, ys, ln_g, ln_b)


def _qkv_kernel(x_ref, w_ref, o1_ref, o4_ref, o16_ref, acc_ref):
    acc = _dot(x_ref[...].astype(BF16), w_ref[...])
    o1_ref[...] = acc.astype(BF16)
    nlc = acc_ref.shape[0]
    for lc in range(nlc):
        acc_ref[lc] = acc[:, lc * LANES:(lc + 1) * LANES]
    for o_ref, dil in ((o4_ref, DILATIONS[1]), (o16_ref, DILATIONS[2])):
        rows = o_ref.shape[0]
        for r in range(dil):
            for lc in range(nlc):
                col = (r * nlc + lc) * LANES
                o_ref[:, col:col + LANES] = acc_ref[lc, pl.ds(r, rows, stride=dil), :].astype(BF16)


def _qkv_proj(h, w_qkv):
    n = h.shape[0]
    tm = ROW_TILE
    wd = 3 * D_MODEL
    d4, d16 = DILATIONS[1], DILATIONS[2]
    row = lambda i: (i, 0)
    return pl.pallas_call(
        _qkv_kernel,
        grid=(n // tm,),
        in_specs=[pl.BlockSpec((tm, D_MODEL), row), pl.BlockSpec((D_MODEL, wd), lambda i: (0, 0))],
        out_specs=[pl.BlockSpec((tm, wd), row), pl.BlockSpec((tm // d4, d4 * wd), row),
                   pl.BlockSpec((tm // d16, d16 * wd), row)],
        out_shape=[jax.ShapeDtypeStruct((n, wd), BF16), jax.ShapeDtypeStruct((n // d4, d4 * wd), BF16),
                   jax.ShapeDtypeStruct((n // d16, d16 * wd), BF16)],
        scratch_shapes=[pltpu.VMEM((wd // LANES, tm, LANES), F32)],
        compiler_params=_params("arbitrary"),
    )(h, w_qkv.astype(BF16))


def _attn_kernel(q_ref, kc_ref, vc_ref, kp_ref, vp_ref, o_ref, lse_ref):
    t = ATTN_STEPS
    qi = lax.broadcasted_iota(jnp.int32, (t, 2 * t), 0)
    kj = lax.broadcasted_iota(jnp.int32, (t, 2 * t), 1)
    dist = t + qi - kj
    valid = (dist >= 0) & (dist <= t) & ((kj >= t) | (pl.program_id(2) > 0))
    lane = lax.broadcasted_iota(jnp.int32, (t, LANES), 1)
    upper = lane >= ATTN_DIM
    lse_blk = jnp.zeros((t, LANES), F32)
    lanes_per_head = LANES // ATTN_HEADS
    for hp in range(ATTN_HEADS // 2):
        cs = slice(hp * LANES, (hp + 1) * LANES)
        q2 = q_ref[:, cs]
        k2 = jnp.concatenate([kp_ref[:, cs], kc_ref[:, cs]], axis=0)
        v2 = jnp.concatenate([vp_ref[:, cs], vc_ref[:, cs]], axis=0)
        halves = []
        for sub in range(2):
            mine = upper if sub else jnp.logical_not(upper)
            qm = jnp.where(mine, q2, jnp.zeros_like(q2))
            s = lax.dot_general(qm, k2, NT_DIMS, preferred_element_type=F32) * (ATTN_DIM ** -0.5)
            s = jnp.where(valid, s, NEG)
            m = jnp.max(s, axis=-1, keepdims=True)
            p = jnp.exp(s - m)
            l = jnp.sum(p, axis=-1, keepdims=True)
            halves.append(_dot(p.astype(BF16), v2) / l)
            head = 2 * hp + sub
            lse_blk = jnp.where(lane // lanes_per_head == head, m + jnp.log(l), lse_blk)
        o_ref[:, cs] = jnp.where(upper, halves[1], halves[0])
    lse_ref[...] = lse_blk


def _attn_branch(qkv_view, bsz, seq, dil):
    length = seq // dil
    t = ATTN_STEPS
    nq = length // t
    qkv_v = qkv_view.reshape(bsz, length, dil * 3 * D_MODEL)
    cur = lambda part: (lambda b, r, i: (b, i, 3 * r + part))
    prev = lambda part: (lambda b, r, i: (b, jnp.maximum(i - 1, 0), 3 * r + part))
    blk = (None, t, D_MODEL)
    o, lse = pl.pallas_call(
        _attn_kernel,
        grid=(bsz, dil, nq),
        in_specs=[pl.BlockSpec(blk, cur(0)), pl.BlockSpec(blk, cur(1)), pl.BlockSpec(blk, cur(2)),
                  pl.BlockSpec(blk, prev(1)), pl.BlockSpec(blk, prev(2))],
        out_specs=[pl.BlockSpec(blk, lambda b, r, i: (b, i, r)), pl.BlockSpec((None, t, LANES), lambda b, r, i: (b, i, r))],
        out_shape=[jax.ShapeDtypeStruct((bsz, length, dil * D_MODEL), F32),
                   jax.ShapeDtypeStruct((bsz, length, dil * LANES), F32)],
        compiler_params=_params("arbitrary", "arbitrary", "arbitrary"),
    )(qkv_v, qkv_v, qkv_v, qkv_v, qkv_v)
    return o.reshape(bsz * length, dil * D_MODEL), lse.reshape(bsz * length, dil * LANES)


def kernel(x, w_in_ab, conv_qkv, gdn_a_log, gdn_dt_bias, gdn_norm, s5_lam_re, s5_lam_im, s5_log_dt, s5_b_re, s5_b_im, s5_c_re, s5_c_im, s5_d, s5_w_glu, w_out_ab, w_qkv_c, w_out_c, ln_mix_g, ln_mix_b, router_group_w, router_group_b, router_expert_w, router_expert_b, moe_w_gate, moe_w_up, moe_w_down, ln_ffn_g, ln_ffn_b):
    bsz, seq, d = x.shape
    n = bsz * seq
    x2 = x.reshape(n, d)
    tm = ROW_TILE

    w_in = w_in_ab[0]
    nq = 4 * GDN_W
    ba_cols = jnp.zeros((d, LANES), F32).at[:, :2 * GDN_HEADS].set(w_in[:, nq:nq + 2 * GDN_HEADS])
    w_cat = jnp.concatenate([w_in[:, :nq], w_in[:, nq + 2 * GDN_HEADS:], ba_cols], axis=1).astype(BF16)
    qkv, z, u, ba = _inproj(x2, w_cat)
    ya = _gdn(qkv, z, ba, conv_qkv[0], gdn_a_log[0], gdn_dt_bias[0], gdn_norm[0], bsz, seq)
    yb = _s5(u.reshape(bsz, seq, S5_W), s5_lam_re[0], s5_lam_im[0], s5_log_dt[0], s5_b_re[0], s5_b_im[0],
             s5_c_re[0], s5_c_im[0], s5_d[0], s5_w_glu[0]).reshape(n, S5_W)
    router0 = (router_group_w[0], router_group_b[0], router_expert_w[0], router_expert_b[0])
    h, route, cnt = _mix_call(_mix_ab_kernel, ((ya, tm), (yb, tm)), x2, w_out_ab[0], ln_mix_g[0], ln_mix_b[0], router0)
    h = _moe(h, route, cnt, moe_w_gate, moe_w_up, moe_w_down, ln_ffn_g[0], ln_ffn_b[0], 0)

    acts = []
    for qkv_view, dil in zip(_qkv_proj(h, w_qkv_c[0]), DILATIONS):
        acts.append(_attn_branch(qkv_view, bsz, seq, dil))
    lanes_per_head = LANES // ATTN_HEADS
    expand = (jnp.arange(LANES)[:, None] == (jnp.arange(D_MODEL)[None, :] // ATTN_DIM) * lanes_per_head).astype(BF16)
    router1 = (router_group_w[1], router_group_b[1], router_expert_w[1], router_expert_b[1])
    d4, d16 = DILATIONS[1], DILATIONS[2]
    (o1, l1), (o4, l4), (o16, l16) = acts
    h, route, cnt = _mix_call(
        _mix_c_kernel, ((o1, tm), (o4, tm // d4), (o16, tm // d16), (l1, tm), (l4, tm // d4), (l16, tm // d16)),
        h, w_out_c[0], ln_mix_g[1], ln_mix_b[1], router1, extra=(expand,),
        scratch=(pltpu.VMEM((D_MODEL // LANES, tm, LANES), F32), pltpu.VMEM((D_MODEL // LANES, tm, LANES), F32),
                 pltpu.VMEM((1, tm, LANES), F32), pltpu.VMEM((1, tm, LANES), F32)))
    h = _moe(h, route, cnt, moe_w_gate, moe_w_up, moe_w_down, ln_ffn_g[1], ln_ffn_b[1], 1)
    return h.reshape(bsz, seq, d)
```

```python
import functools
import math

import jax
import jax.numpy as jnp
from jax import lax
from jax.experimental import pallas as pl
from jax.experimental.pallas import tpu as pltpu

F32 = jnp.float32
BF16 = jnp.bfloat16
HI = lax.Precision.HIGHEST

D_MODEL = 1024
DEPTH = 2
ALPHA = (2 * DEPTH) ** 0.25
LN_EPS = 1e-5

GDN_HEADS = 4
GDN_DIM = 128
GDN_W = GDN_HEADS * GDN_DIM
CONV_K = 4
GDN_CHUNK = 64
GDN_BATCH = 4

S5_W = 512
S5_GROUP_CH = 16
S5_GROUPS = 32
S5_STATE = 64
S5_SUPER = 4
S5_SUPER_CH = S5_W // S5_SUPER
S5_SUPER_ST = S5_GROUPS * S5_STATE // S5_SUPER

ATTN_HEADS = 16
ATTN_DIM = 64
ATTN_STEPS = 128
DILATIONS = (1, 4, 16)

MOE_GROUPS = 4
EXPERTS_PER_GROUP = 8
N_EXPERTS = 32
EXPERT_FF = 512
EXPERT_LANE0 = MOE_GROUPS

LANES = 128
SUBLANES = 8
VMEM_LIMIT = 56 * 1024 * 1024

ROW_TILE = 512
S5_STEPS = 128
MOE_ROWS = 512
GATHER_TILE = 256

NEG = -1e30
NT_DIMS = (((1,), (1,)), ((), ()))
TN_DIMS = (((0,), (0,)), ((), ()))


def _params(*sem):
    return pltpu.CompilerParams(dimension_semantics=sem, vmem_limit_bytes=VMEM_LIMIT)


def _sigmoid(x):
    return 1.0 / (1.0 + jnp.exp(-x))


def _softplus(x):
    return jnp.maximum(x, 0.0) + jnp.log(1.0 + jnp.exp(-jnp.abs(x)))


def _layer_norm(r, g, b):
    mu = jnp.mean(r, axis=-1, keepdims=True)
    c = r - mu
    var = jnp.mean(c * c, axis=-1, keepdims=True)
    return c * lax.rsqrt(var + LN_EPS) * g + b


def _dot(a, b, precision=None):
    return jnp.dot(a, b, precision=precision, preferred_element_type=F32)


def _mm(a, b, dims=None):
    a, b = a.astype(BF16), b.astype(BF16)
    if dims is None:
        return jnp.dot(a, b, preferred_element_type=F32)
    return lax.dot_general(a, b, dims, preferred_element_type=F32)


def _split_bf16(x):
    hi = x.astype(BF16)
    return hi, (x - hi.astype(F32)).astype(BF16)


IN_COLS = 3 * GDN_W + GDN_W + S5_W + LANES


def _inproj_kernel(x_ref, w_ref, qkv_ref, z_ref, u_ref, ba_ref):
    x = x_ref[...].astype(BF16)
    qkv_ref[...] = _dot(x, w_ref[:, 0:1536])
    z_ref[...] = _dot(x, w_ref[:, 1536:2048])
    u_ref[...] = _dot(x, w_ref[:, 2048:2560])
    ba_ref[...] = _dot(x, w_ref[:, 2560:2688])


def _inproj(x2, w_cat):
    n = x2.shape[0]
    tm = ROW_TILE
    row = lambda i: (i, 0)
    return pl.pallas_call(
        _inproj_kernel,
        grid=(n // tm,),
        in_specs=[pl.BlockSpec((tm, D_MODEL), row), pl.BlockSpec((D_MODEL, IN_COLS), lambda i: (0, 0))],
        out_specs=[pl.BlockSpec((tm, 3 * GDN_W), row), pl.BlockSpec((tm, GDN_W), row),
                   pl.BlockSpec((tm, S5_W), row), pl.BlockSpec((tm, LANES), row)],
        out_shape=[jax.ShapeDtypeStruct((n, 3 * GDN_W), F32), jax.ShapeDtypeStruct((n, GDN_W), F32),
                   jax.ShapeDtypeStruct((n, S5_W), F32), jax.ShapeDtypeStruct((n, LANES), F32)],
        compiler_params=_params("arbitrary"),
    )(x2, w_cat)


def _l2norm(t):
    return t * lax.rsqrt(jnp.sum(t * t, axis=-1, keepdims=True) + 1e-6)


def _each(fn, *lists):
    return [fn(*args) for args in zip(*lists)]


def _unit_lower_inverse(lows, same16, cross32, cross64, eye):
    ps = [jnp.where(same16, low, 0.0) for low in lows]
    ts = [eye - p for p in ps]
    for _ in range(3):
        ps = _each(lambda p: _mm(p, p), ps)
        ts = _each(lambda t, p: _mm(t, eye + p), ts, ps)
    for cross in (cross32, cross64):
        tc = _each(lambda t, low: _mm(t, jnp.where(cross, low, 0.0)), ts, lows)
        ts = _each(lambda t, x: t - _mm(x, t), ts, tc)
    return ts


def _gdn_kernel(qkv_ref, z_ref, ba_ref, convw_ref, alog_ref, dtb_ref, normw_ref, tri_ref, o_ref,
                state_ref, tail_ref):
    c = GDN_CHUNK

    @pl.when(pl.program_id(1) == 0)
    def _():
        state_ref[...] = jnp.zeros_like(state_ref)
        tail_ref[...] = jnp.zeros_like(tail_ref)

    w = convw_ref[...]
    row = lax.broadcasted_iota(jnp.int32, (c, 1), 0)
    ri = lax.broadcasted_iota(jnp.int32, (c, c), 0)
    ci = lax.broadcasted_iota(jnp.int32, (c, c), 1)
    causal = ri >= ci
    strict = ri > ci
    eye = (ri == ci).astype(F32)
    same16 = (ri // 16) == (ci // 16)
    same32 = (ri // 32) == (ci // 32)
    cross32 = same32 & jnp.logical_not(same16)
    cross64 = jnp.logical_not(same32)
    normw = normw_ref[...]
    a_neg = -jnp.exp(alog_ref[...])
    zpad = jnp.zeros((c - SUBLANES, 3 * GDN_W), F32)
    head_rows = (lax.broadcasted_iota(jnp.int32, (SUBLANES, LANES), 1)
                 == lax.broadcasted_iota(jnp.int32, (SUBLANES, LANES), 0) + GDN_HEADS).astype(F32)

    chains, qs, ks, kbs, vbs, kws, decays, egcs, kdecs, glasts = [], [], [], [], [], [], [], [], [], []
    for bi in range(GDN_BATCH):
        x = qkv_ref[bi]
        tail = tail_ref[bi]
        conv = x * w[CONV_K - 1:CONV_K, :]
        for j in range(1, CONV_K):
            cur = pltpu.roll(x, j, axis=0)
            prev = jnp.concatenate([pltpu.roll(tail, j, axis=0), zpad], axis=0)
            conv = conv + jnp.where(row < j, prev, cur) * w[CONV_K - 1 - j:CONV_K - j, :]
        tail_ref[bi] = x[c - SUBLANES:, :]
        qkv = conv * _sigmoid(conv)

        ba = ba_ref[bi]
        beta_all = _sigmoid(ba)
        g_all = a_neg * _softplus(ba + dtb_ref[...])
        gc_all = _dot(tri_ref[...], g_all, HI)
        egc_all = jnp.exp(gc_all)
        gc_rows = lax.dot_general(head_rows, gc_all, NT_DIMS, precision=HI, preferred_element_type=F32)

        for h in range(GDN_HEADS):
            q = _l2norm(qkv[:, h * GDN_DIM:(h + 1) * GDN_DIM]) * (GDN_DIM ** -0.5)
            k = _l2norm(qkv[:, GDN_W + h * GDN_DIM:GDN_W + (h + 1) * GDN_DIM])
            v = qkv[:, 2 * GDN_W + h * GDN_DIM:2 * GDN_W + (h + 1) * GDN_DIM]
            gl = GDN_HEADS + h
            beta = beta_all[:, h:h + 1]
            gc = gc_all[:, gl:gl + 1]
            egc = egc_all[:, gl:gl + 1]
            gc_last = gc_all[c - 1:c, gl:gl + 1]
            kb = k * beta
            chains.append((bi, h))
            qs.append(q)
            ks.append(k)
            kbs.append(kb)
            vbs.append(v * beta)
            kws.append(kb * egc)
            decays.append(jnp.exp(jnp.where(causal, gc - gc_rows[h:h + 1, :], NEG)))
            egcs.append(egc)
            kdecs.append(k * jnp.exp(gc_last - gc))
            glasts.append(jnp.exp(gc_last))

    kk = _each(lambda kb, q, k: _mm(jnp.concatenate([kb, q], axis=0), k, NT_DIMS), kbs, qs, ks)
    lows = _each(lambda a, dec: jnp.where(strict, a[:c] * dec, 0.0), kk, decays)
    qks = _each(lambda a, dec: jnp.where(causal, a[c:] * dec, 0.0), kk, decays)
    ts = _unit_lower_inverse(lows, same16, cross32, cross64, eye)
    sols = _each(lambda t, vb, kw: _mm(t, jnp.concatenate([vb, kw], axis=1)), ts, vbs, kws)
    states = [state_ref[bi, h] for bi, h in chains]
    wss = _each(lambda sol, q, egc, s: _mm(jnp.concatenate([sol[:, GDN_DIM:], q * egc], axis=0), s),
                sols, qs, egcs, states)
    v_news = _each(lambda sol, ws: sol[:, :GDN_DIM] - ws[:c], sols, wss)
    outs = _each(lambda ws, qk, v_new: ws[c:] + _mm(qk, v_new), wss, qks, v_news)
    upds = _each(lambda kd, v_new: _mm(kd, v_new, TN_DIMS), kdecs, v_news)
    for (bi, h), s, gl, upd, o in zip(chains, states, glasts, upds, outs):
        hs = slice(h * GDN_DIM, (h + 1) * GDN_DIM)
        state_ref[bi, h] = s * gl + upd
        zh = z_ref[bi, :, hs]
        o_ref[bi, :, hs] = o * lax.rsqrt(jnp.mean(o * o, axis=-1, keepdims=True) + 1e-6) * normw * (zh * _sigmoid(zh))


def _gdn(qkv, z, ba, conv_w, a_log, dt_bias, norm_w, bsz, seq):
    c = GDN_CHUNK
    nb = GDN_BATCH
    convw = jnp.zeros((SUBLANES, 3 * GDN_W), F32).at[:CONV_K].set(conv_w)
    alog = jnp.zeros((1, LANES), F32).at[0, GDN_HEADS:2 * GDN_HEADS].set(a_log)
    dtb = jnp.zeros((1, LANES), F32).at[0, GDN_HEADS:2 * GDN_HEADS].set(dt_bias)
    tri = jnp.tril(jnp.ones((c, c), F32))
    row = lambda b, i: (b, i, 0)
    const = lambda b, i: (0, 0)
    out = pl.pallas_call(
        _gdn_kernel,
        grid=(bsz // nb, seq // c),
        in_specs=[pl.BlockSpec((nb, c, 3 * GDN_W), row), pl.BlockSpec((nb, c, GDN_W), row),
                  pl.BlockSpec((nb, c, LANES), row),
                  pl.BlockSpec((SUBLANES, 3 * GDN_W), const), pl.BlockSpec((1, LANES), const),
                  pl.BlockSpec((1, LANES), const), pl.BlockSpec((1, GDN_DIM), const), pl.BlockSpec((c, c), const)],
        out_specs=pl.BlockSpec((nb, c, GDN_W), row),
        out_shape=jax.ShapeDtypeStruct((bsz, seq, GDN_W), F32),
        scratch_shapes=[pltpu.VMEM((nb, GDN_HEADS, GDN_DIM, GDN_DIM), F32),
                        pltpu.VMEM((nb, SUBLANES, 3 * GDN_W), F32)],
        compiler_params=_params("arbitrary", "arbitrary"),
    )(qkv.reshape(bsz, seq, 3 * GDN_W), z.reshape(bsz, seq, GDN_W), ba.reshape(bsz, seq, LANES),
      convw, alog, dtb, norm_w.reshape(1, GDN_DIM), tri)
    return out.reshape(bsz * seq, GDN_W)


def _gelu_tanh(x):
    return x * (0.5 * (1.0 + jnp.tanh(math.sqrt(2.0 / math.pi) * (x + 0.044715 * (x * x * x)))))


def _s5_kernel(u_ref, wb_ref, a_ref, wc_ref, d_ref, wglu_ref, y_ref, utb_ref, ytb_ref, bu_ref, h_ref, *, steps):
    @pl.when(pl.program_id(0) == 0)
    def _():
        h_ref[...] = jnp.zeros_like(h_ref)

    nb = h_ref.shape[0]
    nlc = S5_W // LANES
    for b in range(nb):
        for lc in range(nlc):
            utb_ref[lc, pl.ds(b, steps, stride=nb), :] = u_ref[b, :, lc * LANES:(lc + 1) * LANES]
    u = jnp.concatenate([utb_ref[lc] for lc in range(nlc)], axis=1)
    ub = u.astype(BF16)
    st = S5_SUPER_ST
    for j in range(S5_SUPER):
        bu_ref[:, 2 * st * j:2 * st * (j + 1)] = _dot(ub[:, S5_SUPER_CH * j:S5_SUPER_CH * (j + 1)], wb_ref[j])

    for j in range(S5_SUPER):
        re = slice(2 * st * j, 2 * st * j + st)
        im = slice(2 * st * j + st, 2 * st * (j + 1))
        ar = jnp.broadcast_to(a_ref[0:1, st * j:st * (j + 1)], (nb, st))
        ai = jnp.broadcast_to(a_ref[1:2, st * j:st * (j + 1)], (nb, st))

        def body(t, carry, re=re, im=im, ar=ar, ai=ai):
            hr, hi = carry
            rows = pl.ds(pl.multiple_of(t * nb, nb), nb)
            nr = ar * hr - ai * hi + bu_ref[rows, re]
            ni = ar * hi + ai * hr + bu_ref[rows, im]
            bu_ref[rows, re] = nr
            bu_ref[rows, im] = ni
            return nr, ni

        hr, hi = lax.fori_loop(0, steps, body, (h_ref[:, re], h_ref[:, im]), unroll=4)
        h_ref[:, re] = hr
        h_ref[:, im] = hi

    y = jnp.concatenate(
        [_dot(bu_ref[:, 2 * st * j:2 * st * (j + 1)].astype(BF16), wc_ref[j]) for j in range(S5_SUPER)], axis=1)
    y = _gelu_tanh(y + d_ref[...] * u)
    y = y * _sigmoid(_dot(y.astype(BF16), wglu_ref[...]))
    for lc in range(nlc):
        ytb_ref[lc] = y[:, lc * LANES:(lc + 1) * LANES]
    for b in range(nb):
        for lc in range(nlc):
            y_ref[b, :, lc * LANES:(lc + 1) * LANES] = ytb_ref[lc, pl.ds(b, steps, stride=nb), :]


def _s5(u3, lam_re, lam_im, log_dt, b_re, b_im, c_re, c_im, d_skip, w_glu):
    bsz, seq, _ = u3.shape
    dt = jnp.exp(log_dt)[:, None]
    mag = jnp.exp(lam_re * dt)
    a_re, a_im = mag * jnp.cos(lam_im * dt), mag * jnp.sin(lam_im * dt)
    den = lam_re * lam_re + lam_im * lam_im
    nr, ni = a_re - 1.0, a_im
    cr = (nr * lam_re + ni * lam_im) / den
    ci = (ni * lam_re - nr * lam_im) / den
    bb_re = cr[..., None] * b_re - ci[..., None] * b_im
    bb_im = cr[..., None] * b_im + ci[..., None] * b_re
    gps = S5_GROUPS // S5_SUPER
    eye = jnp.eye(gps, dtype=F32)

    def in_blockdiag(t):
        t = t.reshape(S5_SUPER, gps, S5_STATE, S5_GROUP_CH)
        return jnp.einsum('jgph,gk->jghkp', t, eye).reshape(S5_SUPER, S5_SUPER_CH, S5_SUPER_ST)

    def out_blockdiag(t):
        t = t.reshape(S5_SUPER, gps, S5_GROUP_CH, S5_STATE)
        return jnp.einsum('jgkp,gm->jgpmk', t, eye).reshape(S5_SUPER, S5_SUPER_ST, S5_SUPER_CH)

    wb = jnp.concatenate([in_blockdiag(bb_re), in_blockdiag(bb_im)], axis=2).astype(BF16)
    wc = jnp.concatenate([out_blockdiag(c_re), -out_blockdiag(c_im)], axis=1).astype(BF16)
    nstate = S5_GROUPS * S5_STATE
    a = jnp.stack([a_re.reshape(nstate), a_im.reshape(nstate)], axis=0)

    steps = S5_STEPS
    rows = steps * bsz
    blk = lambda i: (0, i, 0)
    const2 = lambda i: (0, 0)
    const3 = lambda i: (0, 0, 0)
    return pl.pallas_call(
        functools.partial(_s5_kernel, steps=steps),
        grid=(seq // steps,),
        in_specs=[pl.BlockSpec((bsz, steps, S5_W), blk), pl.BlockSpec(wb.shape, const3),
                  pl.BlockSpec((2, nstate), const2), pl.BlockSpec(wc.shape, const3),
                  pl.BlockSpec((1, S5_W), const2), pl.BlockSpec((S5_W, S5_W), const2)],
        out_specs=pl.BlockSpec((bsz, steps, S5_W), blk),
        out_shape=jax.ShapeDtypeStruct((bsz, seq, S5_W), F32),
        scratch_shapes=[pltpu.VMEM((S5_W // LANES, rows, LANES), F32), pltpu.VMEM((S5_W // LANES, rows, LANES), F32),
                        pltpu.VMEM((rows, 2 * nstate), F32), pltpu.VMEM((bsz, 2 * nstate), F32)],
        compiler_params=_params("arbitrary"),
    )(u3, wb, a, wc, d_skip.reshape(1, S5_W), w_glu.astype(BF16))


def _route(h, wr_ref, br_ref, tri_ref, run_ref, route_ref, pos_ref, tab_ref):
    @pl.when(pl.program_id(0) == 0)
    def _():
        run_ref[...] = jnp.zeros_like(run_ref)

    tm = h.shape[0]
    h_hi, h_lo = _split_bf16(h)
    both = _dot(h_hi, wr_ref[...])
    logits = both[:, :LANES] + both[:, LANES:] + _dot(h_lo, wr_ref[:, :LANES]) + br_ref[...]
    lane = lax.broadcasted_iota(jnp.int32, (tm, LANES), 1)
    lanef = lane.astype(F32)
    big = float(LANES)

    gmask = lane < MOE_GROUPS
    gmax = jnp.max(jnp.where(gmask, logits, NEG), axis=-1, keepdims=True)
    gsum = jnp.sum(jnp.where(gmask, jnp.exp(logits - gmax), 0.0), axis=-1, keepdims=True)
    p_group = 1.0 / gsum
    gidx = jnp.min(jnp.where(gmask & (logits == gmax), lanef, big), axis=-1, keepdims=True)

    lo = EXPERT_LANE0 + EXPERTS_PER_GROUP * gidx
    emask = (lanef >= lo) & (lanef < lo + EXPERTS_PER_GROUP)
    el = jnp.where(emask, logits, NEG)
    t1 = jnp.max(el, axis=-1, keepdims=True)
    i1 = jnp.min(jnp.where(emask & (el == t1), lanef, big), axis=-1, keepdims=True)
    emask2 = emask & (lanef != i1)
    el2 = jnp.where(emask2, logits, NEG)
    t2 = jnp.max(el2, axis=-1, keepdims=True)
    i2 = jnp.min(jnp.where(emask2 & (el2 == t2), lanef, big), axis=-1, keepdims=True)
    e2 = jnp.exp(t2 - t1)
    gate1 = p_group / (1.0 + e2)
    gate2 = p_group * e2 / (1.0 + e2)

    oh1 = lanef == i1
    oh2 = lanef == i2
    oh = jnp.where(oh1 | oh2, 1.0, 0.0)
    cnt = jnp.sum(oh, axis=0, keepdims=True)
    ei = lax.broadcasted_iota(jnp.int32, (LANES, LANES), 0)
    ej = lax.broadcasted_iota(jnp.int32, (LANES, LANES), 1)
    start = _dot(jnp.broadcast_to(cnt, (SUBLANES, LANES)), (ei < ej).astype(F32), HI)[0:1]
    where = _dot(tri_ref[...], oh.astype(BF16)) + start
    pos1 = jnp.sum(jnp.where(oh1, where, 0.0), axis=-1, keepdims=True)
    pos2 = jnp.sum(jnp.where(oh2, where, 0.0), axis=-1, keepdims=True)

    run = run_ref[...]
    srow = lax.broadcasted_iota(jnp.int32, (SUBLANES, LANES), 0)
    slane = lax.broadcasted_iota(jnp.int32, (SUBLANES, LANES), 1)
    tab_ref[...] = jnp.where(srow == 0, cnt, jnp.where(srow == 1, run, jnp.where(srow == 2, start, 0.0)))
    run_ref[...] = run + cnt

    route = jnp.where(lane == 2, gate1, 0.0)
    route = jnp.where(lane == 3, gate2, route)
    route = jnp.where(lane == 4, pos1, route)
    route = jnp.where(lane == 5, pos2, route)
    route_ref[...] = route
    pick = (slane == srow + 4).astype(F32)
    pos_ref[...] = lax.dot_general(pick, route, NT_DIMS, precision=HI, preferred_element_type=F32)


def _mix_ab_kernel(ya_ref, yb_ref, x_ref, w_ref, g_ref, b_ref, wr_ref, br_ref, tri_ref,
                   h_ref, route_ref, pos_ref, tab_ref, run_ref):
    mix = _dot(ya_ref[...].astype(BF16), w_ref[0:GDN_W, :]) + _dot(yb_ref[...].astype(BF16), w_ref[GDN_W:, :])
    h = _layer_norm(ALPHA * x_ref[...] + mix, g_ref[...], b_ref[...])
    h_ref[...] = h
    _route(h, wr_ref, br_ref, tri_ref, run_ref, route_ref, pos_ref, tab_ref)


def _to_natural(view_ref, nat_ref, dil):
    rows = view_ref.shape[0]
    nlc = nat_ref.shape[0]
    for r in range(dil):
        for lc in range(nlc):
            col = (r * nlc + lc) * LANES
            nat_ref[lc, pl.ds(r, rows, stride=dil), :] = view_ref[:, col:col + LANES]
    return jnp.concatenate([nat_ref[lc] for lc in range(nlc)], axis=1)


def _mix_c_kernel(o1_ref, o4_ref, o16_ref, l1_ref, l4_ref, l16_ref, expand_ref, x_ref, w_ref, g_ref, b_ref,
                  wr_ref, br_ref, tri_ref, h_ref, route_ref, pos_ref, tab_ref, run_ref,
                  on4_ref, on16_ref, ln4_ref, ln16_ref):
    o4 = _to_natural(o4_ref, on4_ref, DILATIONS[1])
    o16 = _to_natural(o16_ref, on16_ref, DILATIONS[2])
    l2 = _to_natural(l4_ref, ln4_ref, DILATIONS[1])
    l3 = _to_natural(l16_ref, ln16_ref, DILATIONS[2])
    l1 = l1_ref[...]
    m = jnp.maximum(jnp.maximum(l1, l2), l3)
    e1, e2, e3 = jnp.exp(l1 - m), jnp.exp(l2 - m), jnp.exp(l3 - m)
    inv = 1.0 / (e1 + e2 + e3)
    ex = expand_ref[...]

    def spread(wt):
        hi, lo = _split_bf16(wt)
        return _dot(hi, ex) + _dot(lo, ex)

    o = spread(e1 * inv) * o1_ref[...] + spread(e2 * inv) * o4 + spread(e3 * inv) * o16
    mix = _dot(o.astype(BF16), w_ref[...])
    h = _layer_norm(ALPHA * x_ref[...] + mix, g_ref[...], b_ref[...])
    h_ref[...] = h
    _route(h, wr_ref, br_ref, tri_ref, run_ref, route_ref, pos_ref, tab_ref)


def _router_operands(wg, bg, we, be):
    wr = jnp.zeros((D_MODEL, LANES), F32).at[:, :MOE_GROUPS].set(wg).at[:, EXPERT_LANE0:EXPERT_LANE0 + N_EXPERTS].set(we)
    wr_hi = wr.astype(BF16)
    wr_lo = (wr - wr_hi.astype(F32)).astype(BF16)
    br = jnp.zeros((1, LANES), F32).at[0, :MOE_GROUPS].set(bg).at[0, EXPERT_LANE0:EXPERT_LANE0 + N_EXPERTS].set(be)
    tri = jnp.tril(jnp.ones((ROW_TILE, ROW_TILE), F32), -1).astype(BF16)
    return jnp.concatenate([wr_hi, wr_lo], axis=1), br, tri


def _mix_call(kernel_fn, acts, x2, w_out, ln_g, ln_b, router, extra=(), scratch=()):
    n = x2.shape[0]
    tm = ROW_TILE
    wr, br, tri = _router_operands(*router)
    row = lambda i: (i, 0)
    const = lambda i: (0, 0)
    in_specs = [pl.BlockSpec((rows, a.shape[1]), row) for a, rows in acts]
    in_specs += [pl.BlockSpec(e.shape, const) for e in extra]
    in_specs += [pl.BlockSpec((tm, D_MODEL), row), pl.BlockSpec(w_out.shape, const),
                 pl.BlockSpec((1, D_MODEL), const), pl.BlockSpec((1, D_MODEL), const),
                 pl.BlockSpec((D_MODEL, 2 * LANES), const), pl.BlockSpec((1, LANES), const),
                 pl.BlockSpec((tm, tm), const)]
    return pl.pallas_call(
        kernel_fn,
        grid=(n // tm,),
        in_specs=in_specs,
        out_specs=[pl.BlockSpec((tm, D_MODEL), row), pl.BlockSpec((tm, LANES), row),
                   pl.BlockSpec((SUBLANES, tm), lambda i: (0, i)),
                   pl.BlockSpec((None, SUBLANES, LANES), lambda i: (i, 0, 0))],
        out_shape=[jax.ShapeDtypeStruct((n, D_MODEL), F32), jax.ShapeDtypeStruct((n, LANES), F32),
                   jax.ShapeDtypeStruct((SUBLANES, n), F32), jax.ShapeDtypeStruct((n // tm, SUBLANES, LANES), F32)],
        scratch_shapes=[pltpu.VMEM((1, LANES), F32), *scratch],
        compiler_params=_params("arbitrary"),
    )(*[a for a, _ in acts], *extra, x2, w_out.astype(BF16), ln_g.reshape(1, D_MODEL), ln_b.reshape(1, D_MODEL),
      wr, br, tri)


SLAB = D_MODEL // LANES
SEG_SIZES = tuple(1 << k for k in range(ROW_TILE.bit_length() - 1, -1, -1))


def _segment_tables(tab, n_blocks):
    lanes = slice(EXPERT_LANE0, EXPERT_LANE0 + N_EXPERTS)
    cnt = tab[:, 0, lanes].astype(jnp.int32)
    run = tab[:, 1, lanes].astype(jnp.int32)
    src = tab[:, 2, lanes].astype(jnp.int32)
    counts = run[-1] + cnt[-1]
    padded = (counts + MOE_ROWS - 1) // MOE_ROWS * MOE_ROWS
    padded_end = jnp.cumsum(padded)
    dst = (padded_end - padded)[None, :] + run
    block_start = jnp.arange(n_blocks, dtype=jnp.int32) * MOE_ROWS
    block_expert = jnp.minimum(jnp.sum((padded_end[None, :] <= block_start[:, None]).astype(jnp.int32), axis=1),
                               N_EXPERTS - 1)
    n_used = (padded_end[-1:] // MOE_ROWS).astype(jnp.int32)
    return (cnt.reshape(-1), src.reshape(-1), dst.reshape(-1)), block_expert, n_used


def _slab_rows(ref, start, size):
    return ref.at[pl.ds(pl.multiple_of(start * SLAB, SLAB), size * SLAB)]


def _for_each_segment(tabs, tile, fn):
    cnt_ref, src_ref, dst_ref = tabs

    def per_expert(e, carry):
        idx = tile * N_EXPERTS + e
        c, s, d = cnt_ref[idx], src_ref[idx], dst_ref[idx]
        for size in SEG_SIZES:
            @pl.when((c & size) != 0)
            def _():
                off = c & (-2 * size)
                fn(s + off, d + off, size)
        return carry

    lax.fori_loop(0, N_EXPERTS, per_expert, 0)


def _to_slabs(val, ref):
    for lc in range(SLAB):
        ref[pl.ds(lc, val.shape[0], stride=SLAB), :] = val[:, lc * LANES:(lc + 1) * LANES]


def _from_slabs(ref):
    rows = ref.shape[0] // SLAB
    return jnp.concatenate([ref[pl.ds(lc, rows, stride=SLAB), :] for lc in range(SLAB)], axis=1)


def _dispatch_kernel(cnt_ref, src_ref, dst_ref, h_ref, pos_ref, xs_in_ref, xs_ref, sort_ref, sem):
    del xs_in_ref
    tile = pl.program_id(0)
    tm = h_ref.shape[0]
    j = lax.broadcasted_iota(jnp.int32, (2 * tm, tm), 0).astype(F32)
    pos = pos_ref[...]
    perm = jnp.where((j == pos[0:1, :]) | (j == pos[1:2, :]), 1.0, 0.0).astype(BF16)
    _to_slabs(_dot(perm, h_ref[...].astype(BF16)), sort_ref)

    def copy(s, d, size):
        return pltpu.make_async_copy(_slab_rows(sort_ref, s, size), _slab_rows(xs_ref, d, size), sem)

    tabs = (cnt_ref, src_ref, dst_ref)
    _for_each_segment(tabs, tile, lambda s, d, size: copy(s, d, size).start())
    _for_each_segment(tabs, tile, lambda s, d, size: copy(s, d, size).wait())


def _dispatch(tabs, h, pos, n_slots):
    n = h.shape[0]
    tm = ROW_TILE
    xs0 = jnp.zeros((n_slots * SLAB, LANES), F32)
    grid_spec = pltpu.PrefetchScalarGridSpec(
        num_scalar_prefetch=3,
        grid=(n // tm,),
        in_specs=[pl.BlockSpec((tm, D_MODEL), lambda i, *_: (i, 0)), pl.BlockSpec((SUBLANES, tm), lambda i, *_: (0, i)),
                  pl.BlockSpec(memory_space=pl.ANY)],
        out_specs=pl.BlockSpec(memory_space=pl.ANY),
        scratch_shapes=[pltpu.VMEM((2 * tm * SLAB, LANES), F32), pltpu.SemaphoreType.DMA],
    )
    return pl.pallas_call(
        _dispatch_kernel,
        grid_spec=grid_spec,
        out_shape=jax.ShapeDtypeStruct((n_slots * SLAB, LANES), F32),
        input_output_aliases={5: 0},
        compiler_params=_params("arbitrary"),
    )(*tabs, h, pos, xs0)


def _expert_kernel(be_ref, nu_ref, xs_ref, wg_ref, wu_ref, wd_ref, ys_ref, wgb_ref, wub_ref, wdb_ref):
    i = pl.program_id(0)

    @pl.when((i == 0) | (be_ref[i] != be_ref[jnp.maximum(i - 1, 0)]))
    def _():
        wgb_ref[...] = wg_ref[...].astype(BF16)
        wub_ref[...] = wu_ref[...].astype(BF16)
        wdb_ref[...] = wd_ref[...].astype(BF16)

    @pl.when(i < nu_ref[0])
    def _():
        x = _from_slabs(xs_ref).astype(BF16)
        hg = _dot(x, wgb_ref[...])
        hu = _dot(x, wub_ref[...])
        hdn = hg * _sigmoid(hg) * hu
        _to_slabs(_dot(hdn.astype(BF16), wdb_ref[...]).astype(BF16).astype(F32), ys_ref)

    @pl.when(i >= nu_ref[0])
    def _():
        ys_ref[...] = jnp.zeros_like(ys_ref)


def _experts(block_expert, n_used, xs, w_gate, w_up, w_down, layer):
    blk = MOE_ROWS * SLAB
    row = lambda i, be, nu: (i, 0)
    wsel = lambda i, be, nu: (layer, be[i], 0, 0)
    grid_spec = pltpu.PrefetchScalarGridSpec(
        num_scalar_prefetch=2,
        grid=(xs.shape[0] // blk,),
        in_specs=[pl.BlockSpec((blk, LANES), row),
                  pl.BlockSpec((None, None, D_MODEL, EXPERT_FF), wsel),
                  pl.BlockSpec((None, None, D_MODEL, EXPERT_FF), wsel),
                  pl.BlockSpec((None, None, EXPERT_FF, D_MODEL), wsel)],
        out_specs=pl.BlockSpec((blk, LANES), row),
        scratch_shapes=[pltpu.VMEM((D_MODEL, EXPERT_FF), BF16), pltpu.VMEM((D_MODEL, EXPERT_FF), BF16),
                        pltpu.VMEM((EXPERT_FF, D_MODEL), BF16)],
    )
    return pl.pallas_call(
        _expert_kernel,
        grid_spec=grid_spec,
        out_shape=jax.ShapeDtypeStruct(xs.shape, F32),
        compiler_params=_params("arbitrary"),
    )(block_expert, n_used, xs, w_gate, w_up, w_down)


def _combine_kernel(cnt_ref, src_ref, dst_ref, h_ref, route_ref, ys_ref, g_ref, b_ref, o_ref, sort_ref, sem):
    tile = pl.program_id(0)
    tm = h_ref.shape[0]

    def copy(s, d, size):
        return pltpu.make_async_copy(_slab_rows(ys_ref, d, size), _slab_rows(sort_ref, s, size), sem)

    tabs = (cnt_ref, src_ref, dst_ref)
    _for_each_segment(tabs, tile, lambda s, d, size: copy(s, d, size).start())
    _for_each_segment(tabs, tile, lambda s, d, size: copy(s, d, size).wait())
    ysorted = _from_slabs(sort_ref).astype(BF16)
    r = route_ref[...]
    j = lax.broadcasted_iota(jnp.int32, (tm, 2 * tm), 1).astype(F32)
    y1 = _dot(jnp.where(j == r[:, 4:5], 1.0, 0.0).astype(BF16), ysorted)
    y2 = _dot(jnp.where(j == r[:, 5:6], 1.0, 0.0).astype(BF16), ysorted)
    ffn = y1 * r[:, 2:3] + y2 * r[:, 3:4]
    o_ref[...] = _layer_norm(ALPHA * h_ref[...] + ffn, g_ref[...], b_ref[...])


def _combine(tabs, h, route, ys, ln_g, ln_b):
    n = h.shape[0]
    tm = ROW_TILE
    row = lambda i, *_: (i, 0)
    const = lambda i, *_: (0, 0)
    grid_spec = pltpu.PrefetchScalarGridSpec(
        num_scalar_prefetch=3,
        grid=(n // tm,),
        in_specs=[pl.BlockSpec((tm, D_MODEL), row), pl.BlockSpec((tm, LANES), row), pl.BlockSpec(memory_space=pl.ANY),
                  pl.BlockSpec((1, D_MODEL), const), pl.BlockSpec((1, D_MODEL), const)],
        out_specs=pl.BlockSpec((tm, D_MODEL), row),
        scratch_shapes=[pltpu.VMEM((2 * tm * SLAB, LANES), F32), pltpu.SemaphoreType.DMA],
    )
    return pl.pallas_call(
        _combine_kernel,
        grid_spec=grid_spec,
        out_shape=jax.ShapeDtypeStruct((n, D_MODEL), F32),
        compiler_params=_params("arbitrary"),
    )(*tabs, h, route, ys, ln_g.reshape(1, D_MODEL), ln_b.reshape(1, D_MODEL))


def _moe(h, route, pos, tab, w_gate, w_up, w_down, ln_g, ln_b, layer):
    n = h.shape[0]
    n_slots = n * 2 + N_EXPERTS * MOE_ROWS
    tabs, block_expert, n_used = _segment_tables(tab, n_slots // MOE_ROWS)
    xs = _dispatch(tabs, h, pos, n_slots)
    ys = _experts(block_expert, n_used, xs, w_gate, w_up, w_down, layer)
    return _combine(tabs, h, route, ys, ln_g, ln_b)


def _qkv_kernel(x_ref, w_ref, o1_ref, o4_ref, o16_ref, acc_ref):
    acc = _dot(x_ref[...].astype(BF16), w_ref[...])
    o1_ref[...] = acc.astype(BF16)
    nlc = acc_ref.shape[0]
    for lc in range(nlc):
        acc_ref[lc] = acc[:, lc * LANES:(lc + 1) * LANES]
    for o_ref, dil in ((o4_ref, DILATIONS[1]), (o16_ref, DILATIONS[2])):
        rows = o_ref.shape[0]
        for r in range(dil):
            for lc in range(nlc):
                col = (r * nlc + lc) * LANES
                o_ref[:, col:col + LANES] = acc_ref[lc, pl.ds(r, rows, stride=dil), :].astype(BF16)


def _qkv_proj(h, w_qkv):
    n = h.shape[0]
    tm = ROW_TILE
    wd = 3 * D_MODEL
    d4, d16 = DILATIONS[1], DILATIONS[2]
    row = lambda i: (i, 0)
    return pl.pallas_call(
        _qkv_kernel,
        grid=(n // tm,),
        in_specs=[pl.BlockSpec((tm, D_MODEL), row), pl.BlockSpec((D_MODEL, wd), lambda i: (0, 0))],
        out_specs=[pl.BlockSpec((tm, wd), row), pl.BlockSpec((tm // d4, d4 * wd), row),
                   pl.BlockSpec((tm // d16, d16 * wd), row)],
        out_shape=[jax.ShapeDtypeStruct((n, wd), BF16), jax.ShapeDtypeStruct((n // d4, d4 * wd), BF16),
                   jax.ShapeDtypeStruct((n // d16, d16 * wd), BF16)],
        scratch_shapes=[pltpu.VMEM((wd // LANES, tm, LANES), F32)],
        compiler_params=_params("arbitrary"),
    )(h, w_qkv.astype(BF16))


def _attn_kernel(q_ref, kc_ref, vc_ref, kp_ref, vp_ref, o_ref, lse_ref):
    t = ATTN_STEPS
    qi = lax.broadcasted_iota(jnp.int32, (t, 2 * t), 0)
    kj = lax.broadcasted_iota(jnp.int32, (t, 2 * t), 1)
    dist = t + qi - kj
    valid = (dist >= 0) & (dist <= t) & ((kj >= t) | (pl.program_id(2) > 0))
    lane = lax.broadcasted_iota(jnp.int32, (t, LANES), 1)
    upper = lane >= ATTN_DIM
    lse_blk = jnp.zeros((t, LANES), F32)
    lanes_per_head = LANES // ATTN_HEADS
    for hp in range(ATTN_HEADS // 2):
        cs = slice(hp * LANES, (hp + 1) * LANES)
        q2 = q_ref[:, cs]
        k2 = jnp.concatenate([kp_ref[:, cs], kc_ref[:, cs]], axis=0)
        v2 = jnp.concatenate([vp_ref[:, cs], vc_ref[:, cs]], axis=0)
        halves = []
        for sub in range(2):
            mine = upper if sub else jnp.logical_not(upper)
            qm = jnp.where(mine, q2, jnp.zeros_like(q2))
            s = lax.dot_general(qm, k2, NT_DIMS, preferred_element_type=F32) * (ATTN_DIM ** -0.5)
            s = jnp.where(valid, s, NEG)
            m = jnp.max(s, axis=-1, keepdims=True)
            p = jnp.exp(s - m)
            l = jnp.sum(p, axis=-1, keepdims=True)
            halves.append(_dot(p.astype(BF16), v2) / l)
            head = 2 * hp + sub
            lse_blk = jnp.where(lane // lanes_per_head == head, m + jnp.log(l), lse_blk)
        o_ref[:, cs] = jnp.where(upper, halves[1], halves[0])
    lse_ref[...] = lse_blk


def _attn_branch(qkv_view, bsz, seq, dil):
    length = seq // dil
    t = ATTN_STEPS
    nq = length // t
    qkv_v = qkv_view.reshape(bsz, length, dil * 3 * D_MODEL)
    cur = lambda part: (lambda b, r, i: (b, i, 3 * r + part))
    prev = lambda part: (lambda b, r, i: (b, jnp.maximum(i - 1, 0), 3 * r + part))
    blk = (None, t, D_MODEL)
    o, lse = pl.pallas_call(
        _attn_kernel,
        grid=(bsz, dil, nq),
        in_specs=[pl.BlockSpec(blk, cur(0)), pl.BlockSpec(blk, cur(1)), pl.BlockSpec(blk, cur(2)),
                  pl.BlockSpec(blk, prev(1)), pl.BlockSpec(blk, prev(2))],
        out_specs=[pl.BlockSpec(blk, lambda b, r, i: (b, i, r)), pl.BlockSpec((None, t, LANES), lambda b, r, i: (b, i, r))],
        out_shape=[jax.ShapeDtypeStruct((bsz, length, dil * D_MODEL), F32),
                   jax.ShapeDtypeStruct((bsz, length, dil * LANES), F32)],
        compiler_params=_params("arbitrary", "arbitrary", "arbitrary"),
    )(qkv_v, qkv_v, qkv_v, qkv_v, qkv_v)
    return o.reshape(bsz * length, dil * D_MODEL), lse.reshape(bsz * length, dil * LANES)


def kernel(x, w_in_ab, conv_qkv, gdn_a_log, gdn_dt_bias, gdn_norm, s5_lam_re, s5_lam_im, s5_log_dt, s5_b_re, s5_b_im, s5_c_re, s5_c_im, s5_d, s5_w_glu, w_out_ab, w_qkv_c, w_out_c, ln_mix_g, ln_mix_b, router_group_w, router_group_b, router_expert_w, router_expert_b, moe_w_gate, moe_w_up, moe_w_down, ln_ffn_g, ln_ffn_b):
    bsz, seq, d = x.shape
    n = bsz * seq
    x2 = x.reshape(n, d)
    tm = ROW_TILE

    w_in = w_in_ab[0]
    nq = 4 * GDN_W
    ba_cols = jnp.zeros((d, LANES), F32).at[:, :2 * GDN_HEADS].set(w_in[:, nq:nq + 2 * GDN_HEADS])
    w_cat = jnp.concatenate([w_in[:, :nq], w_in[:, nq + 2 * GDN_HEADS:], ba_cols], axis=1).astype(BF16)
    qkv, z, u, ba = _inproj(x2, w_cat)
    ya = _gdn(qkv, z, ba, conv_qkv[0], gdn_a_log[0], gdn_dt_bias[0], gdn_norm[0], bsz, seq)
    yb = _s5(u.reshape(bsz, seq, S5_W), s5_lam_re[0], s5_lam_im[0], s5_log_dt[0], s5_b_re[0], s5_b_im[0],
             s5_c_re[0], s5_c_im[0], s5_d[0], s5_w_glu[0]).reshape(n, S5_W)
    router0 = (router_group_w[0], router_group_b[0], router_expert_w[0], router_expert_b[0])
    h, route, pos, tab = _mix_call(_mix_ab_kernel, ((ya, tm), (yb, tm)), x2, w_out_ab[0], ln_mix_g[0], ln_mix_b[0], router0)
    h = _moe(h, route, pos, tab, moe_w_gate, moe_w_up, moe_w_down, ln_ffn_g[0], ln_ffn_b[0], 0)

    acts = []
    for qkv_view, dil in zip(_qkv_proj(h, w_qkv_c[0]), DILATIONS):
        acts.append(_attn_branch(qkv_view, bsz, seq, dil))
    lanes_per_head = LANES // ATTN_HEADS
    expand = (jnp.arange(LANES)[:, None] == (jnp.arange(D_MODEL)[None, :] // ATTN_DIM) * lanes_per_head).astype(BF16)
    router1 = (router_group_w[1], router_group_b[1], router_expert_w[1], router_expert_b[1])
    d4, d16 = DILATIONS[1], DILATIONS[2]
    (o1, l1), (o4, l4), (o16, l16) = acts
    h, route, pos, tab = _mix_call(
        _mix_c_kernel, ((o1, tm), (o4, tm // d4), (o16, tm // d16), (l1, tm), (l4, tm // d4), (l16, tm // d16)),
        h, w_out_c[0], ln_mix_g[1], ln_mix_b[1], router1, extra=(expand,),
        scratch=(pltpu.VMEM((D_MODEL // LANES, tm, LANES), F32), pltpu.VMEM((D_MODEL // LANES, tm, LANES), F32),
                 pltpu.VMEM((1, tm, LANES), F32), pltpu.VMEM((1, tm, LANES), F32)))
    h = _moe(h, route, pos, tab, moe_w_gate, moe_w_up, moe_w_down, ln_ffn_g[1], ln_ffn_b[1], 1)
    return h.reshape(bsz, seq, d)
```

```python
import functools
import math

import jax
import jax.numpy as jnp
from jax import lax
from jax.experimental import pallas as pl
from jax.experimental.pallas import tpu as pltpu

F32 = jnp.float32
BF16 = jnp.bfloat16
HI = lax.Precision.HIGHEST

D_MODEL = 1024
DEPTH = 2
ALPHA = (2 * DEPTH) ** 0.25
LN_EPS = 1e-5

GDN_HEADS = 4
GDN_DIM = 128
GDN_W = GDN_HEADS * GDN_DIM
CONV_K = 4
GDN_CHUNK = 64
GDN_BATCH = 4

S5_W = 512
S5_GROUP_CH = 16
S5_GROUPS = 32
S5_STATE = 64
S5_SUPER = 4
S5_SUPER_CH = S5_W // S5_SUPER
S5_SUPER_ST = S5_GROUPS * S5_STATE // S5_SUPER

ATTN_HEADS = 16
ATTN_DIM = 64
ATTN_STEPS = 128
DILATIONS = (1, 4, 16)

MOE_GROUPS = 4
EXPERTS_PER_GROUP = 8
N_EXPERTS = 32
EXPERT_FF = 512
EXPERT_LANE0 = MOE_GROUPS

LANES = 128
SUBLANES = 8
VMEM_LIMIT = 56 * 1024 * 1024

ROW_TILE = 512
S5_STEPS = 128
MOE_ROWS = 512
GATHER_TILE = 256

NEG = -1e30
NT_DIMS = (((1,), (1,)), ((), ()))
TN_DIMS = (((0,), (0,)), ((), ()))


def _params(*sem):
    return pltpu.CompilerParams(dimension_semantics=sem, vmem_limit_bytes=VMEM_LIMIT)


def _sigmoid(x):
    return 1.0 / (1.0 + jnp.exp(-x))


def _softplus(x):
    return jnp.maximum(x, 0.0) + jnp.log(1.0 + jnp.exp(-jnp.abs(x)))


def _layer_norm(r, g, b):
    mu = jnp.mean(r, axis=-1, keepdims=True)
    c = r - mu
    var = jnp.mean(c * c, axis=-1, keepdims=True)
    return c * lax.rsqrt(var + LN_EPS) * g + b


def _dot(a, b, precision=None):
    return jnp.dot(a, b, precision=precision, preferred_element_type=F32)


def _mm(a, b, dims=None):
    a, b = a.astype(BF16), b.astype(BF16)
    if dims is None:
        return jnp.dot(a, b, preferred_element_type=F32)
    return lax.dot_general(a, b, dims, preferred_element_type=F32)


def _split_bf16(x):
    hi = x.astype(BF16)
    return hi, (x - hi.astype(F32)).astype(BF16)


IN_COLS = 3 * GDN_W + GDN_W + S5_W + LANES


def _inproj_kernel(x_ref, w_ref, qkv_ref, z_ref, u_ref, ba_ref):
    x = x_ref[...].astype(BF16)
    qkv_ref[...] = _dot(x, w_ref[:, 0:1536])
    z_ref[...] = _dot(x, w_ref[:, 1536:2048])
    u_ref[...] = _dot(x, w_ref[:, 2048:2560])
    ba_ref[...] = _dot(x, w_ref[:, 2560:2688])


def _inproj(x2, w_cat):
    n = x2.shape[0]
    tm = ROW_TILE
    row = lambda i: (i, 0)
    return pl.pallas_call(
        _inproj_kernel,
        grid=(n // tm,),
        in_specs=[pl.BlockSpec((tm, D_MODEL), row), pl.BlockSpec((D_MODEL, IN_COLS), lambda i: (0, 0))],
        out_specs=[pl.BlockSpec((tm, 3 * GDN_W), row), pl.BlockSpec((tm, GDN_W), row),
                   pl.BlockSpec((tm, S5_W), row), pl.BlockSpec((tm, LANES), row)],
        out_shape=[jax.ShapeDtypeStruct((n, 3 * GDN_W), F32), jax.ShapeDtypeStruct((n, GDN_W), F32),
                   jax.ShapeDtypeStruct((n, S5_W), F32), jax.ShapeDtypeStruct((n, LANES), F32)],
        compiler_params=_params("arbitrary"),
    )(x2, w_cat)


def _l2norm(t):
    return t * lax.rsqrt(jnp.sum(t * t, axis=-1, keepdims=True) + 1e-6)


def _each(fn, *lists):
    return [fn(*args) for args in zip(*lists)]


def _unit_lower_inverse(lows, same16, cross32, cross64, eye):
    ps = [jnp.where(same16, low, 0.0) for low in lows]
    ts = [eye - p for p in ps]
    for _ in range(3):
        ps = _each(lambda p: _mm(p, p), ps)
        ts = _each(lambda t, p: _mm(t, eye + p), ts, ps)
    for cross in (cross32, cross64):
        tc = _each(lambda t, low: _mm(t, jnp.where(cross, low, 0.0)), ts, lows)
        ts = _each(lambda t, x: t - _mm(x, t), ts, tc)
    return ts


def _gdn_kernel(qkv_ref, z_ref, ba_ref, convw_ref, alog_ref, dtb_ref, normw_ref, tri_ref, o_ref,
                state_ref, tail_ref):
    c = GDN_CHUNK

    @pl.when(pl.program_id(1) == 0)
    def _():
        state_ref[...] = jnp.zeros_like(state_ref)
        tail_ref[...] = jnp.zeros_like(tail_ref)

    w = convw_ref[...]
    row = lax.broadcasted_iota(jnp.int32, (c, 1), 0)
    ri = lax.broadcasted_iota(jnp.int32, (c, c), 0)
    ci = lax.broadcasted_iota(jnp.int32, (c, c), 1)
    causal = ri >= ci
    strict = ri > ci
    eye = (ri == ci).astype(F32)
    same16 = (ri // 16) == (ci // 16)
    same32 = (ri // 32) == (ci // 32)
    cross32 = same32 & jnp.logical_not(same16)
    cross64 = jnp.logical_not(same32)
    normw = normw_ref[...]
    a_neg = -jnp.exp(alog_ref[...])
    zpad = jnp.zeros((c - SUBLANES, 3 * GDN_W), F32)
    head_rows = (lax.broadcasted_iota(jnp.int32, (SUBLANES, LANES), 1)
                 == lax.broadcasted_iota(jnp.int32, (SUBLANES, LANES), 0) + GDN_HEADS).astype(F32)

    chains, qs, ks, kbs, vbs, kws, decays, egcs, kdecs, glasts = [], [], [], [], [], [], [], [], [], []
    for bi in range(GDN_BATCH):
        x = qkv_ref[bi]
        tail = tail_ref[bi]
        conv = x * w[CONV_K - 1:CONV_K, :]
        for j in range(1, CONV_K):
            cur = pltpu.roll(x, j, axis=0)
            prev = jnp.concatenate([pltpu.roll(tail, j, axis=0), zpad], axis=0)
            conv = conv + jnp.where(row < j, prev, cur) * w[CONV_K - 1 - j:CONV_K - j, :]
        tail_ref[bi] = x[c - SUBLANES:, :]
        qkv = conv * _sigmoid(conv)

        ba = ba_ref[bi]
        beta_all = _sigmoid(ba)
        g_all = a_neg * _softplus(ba + dtb_ref[...])
        gc_all = _dot(tri_ref[...], g_all, HI)
        egc_all = jnp.exp(gc_all)
        gc_rows = lax.dot_general(head_rows, gc_all, NT_DIMS, precision=HI, preferred_element_type=F32)

        for h in range(GDN_HEADS):
            q = _l2norm(qkv[:, h * GDN_DIM:(h + 1) * GDN_DIM]) * (GDN_DIM ** -0.5)
            k = _l2norm(qkv[:, GDN_W + h * GDN_DIM:GDN_W + (h + 1) * GDN_DIM])
            v = qkv[:, 2 * GDN_W + h * GDN_DIM:2 * GDN_W + (h + 1) * GDN_DIM]
            gl = GDN_HEADS + h
            beta = beta_all[:, h:h + 1]
            gc = gc_all[:, gl:gl + 1]
            egc = egc_all[:, gl:gl + 1]
            gc_last = gc_all[c - 1:c, gl:gl + 1]
            kb = k * beta
            chains.append((bi, h))
            qs.append(q)
            ks.append(k)
            kbs.append(kb)
            vbs.append(v * beta)
            kws.append(kb * egc)
            decays.append(jnp.exp(jnp.where(causal, gc - gc_rows[h:h + 1, :], NEG)))
            egcs.append(egc)
            kdecs.append(k * jnp.exp(gc_last - gc))
            glasts.append(jnp.exp(gc_last))

    kk = _each(lambda kb, q, k: _mm(jnp.concatenate([kb, q], axis=0), k, NT_DIMS), kbs, qs, ks)
    lows = _each(lambda a, dec: jnp.where(strict, a[:c] * dec, 0.0), kk, decays)
    qks = _each(lambda a, dec: jnp.where(causal, a[c:] * dec, 0.0), kk, decays)
    ts = _unit_lower_inverse(lows, same16, cross32, cross64, eye)
    sols = _each(lambda t, vb, kw: _mm(t, jnp.concatenate([vb, kw], axis=1)), ts, vbs, kws)
    states = [state_ref[bi, h] for bi, h in chains]
    wss = _each(lambda sol, q, egc, s: _mm(jnp.concatenate([sol[:, GDN_DIM:], q * egc], axis=0), s),
                sols, qs, egcs, states)
    v_news = _each(lambda sol, ws: sol[:, :GDN_DIM] - ws[:c], sols, wss)
    outs = _each(lambda ws, qk, v_new: ws[c:] + _mm(qk, v_new), wss, qks, v_news)
    upds = _each(lambda kd, v_new: _mm(kd, v_new, TN_DIMS), kdecs, v_news)
    for (bi, h), s, gl, upd, o in zip(chains, states, glasts, upds, outs):
        hs = slice(h * GDN_DIM, (h + 1) * GDN_DIM)
        state_ref[bi, h] = s * gl + upd
        zh = z_ref[bi, :, hs]
        o_ref[bi, :, hs] = o * lax.rsqrt(jnp.mean(o * o, axis=-1, keepdims=True) + 1e-6) * normw * (zh * _sigmoid(zh))


def _gdn(qkv, z, ba, conv_w, a_log, dt_bias, norm_w, bsz, seq):
    c = GDN_CHUNK
    nb = GDN_BATCH
    convw = jnp.zeros((SUBLANES, 3 * GDN_W), F32).at[:CONV_K].set(conv_w)
    alog = jnp.zeros((1, LANES), F32).at[0, GDN_HEADS:2 * GDN_HEADS].set(a_log)
    dtb = jnp.zeros((1, LANES), F32).at[0, GDN_HEADS:2 * GDN_HEADS].set(dt_bias)
    tri = jnp.tril(jnp.ones((c, c), F32))
    row = lambda b, i: (b, i, 0)
    const = lambda b, i: (0, 0)
    out = pl.pallas_call(
        _gdn_kernel,
        grid=(bsz // nb, seq // c),
        in_specs=[pl.BlockSpec((nb, c, 3 * GDN_W), row), pl.BlockSpec((nb, c, GDN_W), row),
                  pl.BlockSpec((nb, c, LANES), row),
                  pl.BlockSpec((SUBLANES, 3 * GDN_W), const), pl.BlockSpec((1, LANES), const),
                  pl.BlockSpec((1, LANES), const), pl.BlockSpec((1, GDN_DIM), const), pl.BlockSpec((c, c), const)],
        out_specs=pl.BlockSpec((nb, c, GDN_W), row),
        out_shape=jax.ShapeDtypeStruct((bsz, seq, GDN_W), F32),
        scratch_shapes=[pltpu.VMEM((nb, GDN_HEADS, GDN_DIM, GDN_DIM), F32),
                        pltpu.VMEM((nb, SUBLANES, 3 * GDN_W), F32)],
        compiler_params=_params("arbitrary", "arbitrary"),
    )(qkv.reshape(bsz, seq, 3 * GDN_W), z.reshape(bsz, seq, GDN_W), ba.reshape(bsz, seq, LANES),
      convw, alog, dtb, norm_w.reshape(1, GDN_DIM), tri)
    return out.reshape(bsz * seq, GDN_W)


def _gelu_tanh(x):
    return x * (0.5 * (1.0 + jnp.tanh(math.sqrt(2.0 / math.pi) * (x + 0.044715 * (x * x * x)))))


def _s5_kernel(u_ref, wb_ref, a_ref, wc_ref, d_ref, wglu_ref, y_ref, utb_ref, ytb_ref, bu_ref, h_ref, *, steps):
    @pl.when(pl.program_id(0) == 0)
    def _():
        h_ref[...] = jnp.zeros_like(h_ref)

    nb = h_ref.shape[0]
    nlc = S5_W // LANES
    for b in range(nb):
        for lc in range(nlc):
            utb_ref[lc, pl.ds(b, steps, stride=nb), :] = u_ref[b, :, lc * LANES:(lc + 1) * LANES]
    u = jnp.concatenate([utb_ref[lc] for lc in range(nlc)], axis=1)
    ub = u.astype(BF16)
    st = S5_SUPER_ST
    for j in range(S5_SUPER):
        bu_ref[:, 2 * st * j:2 * st * (j + 1)] = _dot(ub[:, S5_SUPER_CH * j:S5_SUPER_CH * (j + 1)], wb_ref[j])

    for j in range(S5_SUPER):
        re = slice(2 * st * j, 2 * st * j + st)
        im = slice(2 * st * j + st, 2 * st * (j + 1))
        ar = jnp.broadcast_to(a_ref[0:1, st * j:st * (j + 1)], (nb, st))
        ai = jnp.broadcast_to(a_ref[1:2, st * j:st * (j + 1)], (nb, st))

        def body(t, carry, re=re, im=im, ar=ar, ai=ai):
            hr, hi = carry
            rows = pl.ds(pl.multiple_of(t * nb, nb), nb)
            nr = ar * hr - ai * hi + bu_ref[rows, re]
            ni = ar * hi + ai * hr + bu_ref[rows, im]
            bu_ref[rows, re] = nr
            bu_ref[rows, im] = ni
            return nr, ni

        hr, hi = lax.fori_loop(0, steps, body, (h_ref[:, re], h_ref[:, im]), unroll=4)
        h_ref[:, re] = hr
        h_ref[:, im] = hi

    y = jnp.concatenate(
        [_dot(bu_ref[:, 2 * st * j:2 * st * (j + 1)].astype(BF16), wc_ref[j]) for j in range(S5_SUPER)], axis=1)
    y = _gelu_tanh(y + d_ref[...] * u)
    y = y * _sigmoid(_dot(y.astype(BF16), wglu_ref[...]))
    for lc in range(nlc):
        ytb_ref[lc] = y[:, lc * LANES:(lc + 1) * LANES]
    for b in range(nb):
        for lc in range(nlc):
            y_ref[b, :, lc * LANES:(lc + 1) * LANES] = ytb_ref[lc, pl.ds(b, steps, stride=nb), :]


def _s5(u3, lam_re, lam_im, log_dt, b_re, b_im, c_re, c_im, d_skip, w_glu):
    bsz, seq, _ = u3.shape
    dt = jnp.exp(log_dt)[:, None]
    mag = jnp.exp(lam_re * dt)
    a_re, a_im = mag * jnp.cos(lam_im * dt), mag * jnp.sin(lam_im * dt)
    den = lam_re * lam_re + lam_im * lam_im
    nr, ni = a_re - 1.0, a_im
    cr = (nr * lam_re + ni * lam_im) / den
    ci = (ni * lam_re - nr * lam_im) / den
    bb_re = cr[..., None] * b_re - ci[..., None] * b_im
    bb_im = cr[..., None] * b_im + ci[..., None] * b_re
    gps = S5_GROUPS // S5_SUPER
    eye = jnp.eye(gps, dtype=F32)

    def in_blockdiag(t):
        t = t.reshape(S5_SUPER, gps, S5_STATE, S5_GROUP_CH)
        return jnp.einsum('jgph,gk->jghkp', t, eye).reshape(S5_SUPER, S5_SUPER_CH, S5_SUPER_ST)

    def out_blockdiag(t):
        t = t.reshape(S5_SUPER, gps, S5_GROUP_CH, S5_STATE)
        return jnp.einsum('jgkp,gm->jgpmk', t, eye).reshape(S5_SUPER, S5_SUPER_ST, S5_SUPER_CH)

    wb = jnp.concatenate([in_blockdiag(bb_re), in_blockdiag(bb_im)], axis=2).astype(BF16)
    wc = jnp.concatenate([out_blockdiag(c_re), -out_blockdiag(c_im)], axis=1).astype(BF16)
    nstate = S5_GROUPS * S5_STATE
    a = jnp.stack([a_re.reshape(nstate), a_im.reshape(nstate)], axis=0)

    steps = S5_STEPS
    rows = steps * bsz
    blk = lambda i: (0, i, 0)
    const2 = lambda i: (0, 0)
    const3 = lambda i: (0, 0, 0)
    return pl.pallas_call(
        functools.partial(_s5_kernel, steps=steps),
        grid=(seq // steps,),
        in_specs=[pl.BlockSpec((bsz, steps, S5_W), blk), pl.BlockSpec(wb.shape, const3),
                  pl.BlockSpec((2, nstate), const2), pl.BlockSpec(wc.shape, const3),
                  pl.BlockSpec((1, S5_W), const2), pl.BlockSpec((S5_W, S5_W), const2)],
        out_specs=pl.BlockSpec((bsz, steps, S5_W), blk),
        out_shape=jax.ShapeDtypeStruct((bsz, seq, S5_W), F32),
        scratch_shapes=[pltpu.VMEM((S5_W // LANES, rows, LANES), F32), pltpu.VMEM((S5_W // LANES, rows, LANES), F32),
                        pltpu.VMEM((rows, 2 * nstate), F32), pltpu.VMEM((bsz, 2 * nstate), F32)],
        compiler_params=_params("arbitrary"),
    )(u3, wb, a, wc, d_skip.reshape(1, S5_W), w_glu.astype(BF16))


def _route(h, wr_ref, br_ref, tri_ref, run_ref, route_ref, pos_ref, tab_ref):
    @pl.when(pl.program_id(0) == 0)
    def _():
        run_ref[...] = jnp.zeros_like(run_ref)

    tm = h.shape[0]
    h_hi, h_lo = _split_bf16(h)
    both = _dot(h_hi, wr_ref[...])
    logits = both[:, :LANES] + both[:, LANES:] + _dot(h_lo, wr_ref[:, :LANES]) + br_ref[...]
    lane = lax.broadcasted_iota(jnp.int32, (tm, LANES), 1)
    lanef = lane.astype(F32)
    big = float(LANES)

    gmask = lane < MOE_GROUPS
    gmax = jnp.max(jnp.where(gmask, logits, NEG), axis=-1, keepdims=True)
    gsum = jnp.sum(jnp.where(gmask, jnp.exp(logits - gmax), 0.0), axis=-1, keepdims=True)
    p_group = 1.0 / gsum
    gidx = jnp.min(jnp.where(gmask & (logits == gmax), lanef, big), axis=-1, keepdims=True)

    lo = EXPERT_LANE0 + EXPERTS_PER_GROUP * gidx
    emask = (lanef >= lo) & (lanef < lo + EXPERTS_PER_GROUP)
    el = jnp.where(emask, logits, NEG)
    t1 = jnp.max(el, axis=-1, keepdims=True)
    i1 = jnp.min(jnp.where(emask & (el == t1), lanef, big), axis=-1, keepdims=True)
    emask2 = emask & (lanef != i1)
    el2 = jnp.where(emask2, logits, NEG)
    t2 = jnp.max(el2, axis=-1, keepdims=True)
    i2 = jnp.min(jnp.where(emask2 & (el2 == t2), lanef, big), axis=-1, keepdims=True)
    e2 = jnp.exp(t2 - t1)
    gate1 = p_group / (1.0 + e2)
    gate2 = p_group * e2 / (1.0 + e2)

    oh1 = lanef == i1
    oh2 = lanef == i2
    oh = jnp.where(oh1 | oh2, 1.0, 0.0)
    cnt = jnp.sum(oh, axis=0, keepdims=True)
    ei = lax.broadcasted_iota(jnp.int32, (LANES, LANES), 0)
    ej = lax.broadcasted_iota(jnp.int32, (LANES, LANES), 1)
    start = _dot(jnp.broadcast_to(cnt, (SUBLANES, LANES)), (ei < ej).astype(F32), HI)[0:1]
    where = _dot(tri_ref[...], oh.astype(BF16)) + start
    pos1 = jnp.sum(jnp.where(oh1, where, 0.0), axis=-1, keepdims=True)
    pos2 = jnp.sum(jnp.where(oh2, where, 0.0), axis=-1, keepdims=True)

    run = run_ref[...]
    srow = lax.broadcasted_iota(jnp.int32, (SUBLANES, LANES), 0)
    slane = lax.broadcasted_iota(jnp.int32, (SUBLANES, LANES), 1)
    tab_ref[...] = jnp.where(srow == 0, cnt, jnp.where(srow == 1, run, jnp.where(srow == 2, start, 0.0)))
    run_ref[...] = run + cnt

    route = jnp.where(lane == 2, gate1, 0.0)
    route = jnp.where(lane == 3, gate2, route)
    route = jnp.where(lane == 4, pos1, route)
    route = jnp.where(lane == 5, pos2, route)
    route_ref[...] = route
    pick = (slane == srow + 4).astype(F32)
    pos_ref[...] = lax.dot_general(pick, route, NT_DIMS, precision=HI, preferred_element_type=F32)


def _mix_ab_kernel(ya_ref, yb_ref, x_ref, w_ref, g_ref, b_ref, wr_ref, br_ref, tri_ref,
                   h_ref, route_ref, pos_ref, tab_ref, run_ref):
    mix = _dot(ya_ref[...].astype(BF16), w_ref[0:GDN_W, :]) + _dot(yb_ref[...].astype(BF16), w_ref[GDN_W:, :])
    h = _layer_norm(ALPHA * x_ref[...] + mix, g_ref[...], b_ref[...])
    h_ref[...] = h
    _route(h, wr_ref, br_ref, tri_ref, run_ref, route_ref, pos_ref, tab_ref)


def _to_natural(view_ref, nat_ref, dil):
    rows = view_ref.shape[0]
    nlc = nat_ref.shape[0]
    for r in range(dil):
        for lc in range(nlc):
            col = (r * nlc + lc) * LANES
            nat_ref[lc, pl.ds(r, rows, stride=dil), :] = view_ref[:, col:col + LANES]
    return jnp.concatenate([nat_ref[lc] for lc in range(nlc)], axis=1)


def _mix_c_kernel(o1_ref, o4_ref, o16_ref, l1_ref, l4_ref, l16_ref, expand_ref, x_ref, w_ref, g_ref, b_ref,
                  wr_ref, br_ref, tri_ref, h_ref, route_ref, pos_ref, tab_ref, run_ref,
                  on4_ref, on16_ref, ln4_ref, ln16_ref):
    o4 = _to_natural(o4_ref, on4_ref, DILATIONS[1])
    o16 = _to_natural(o16_ref, on16_ref, DILATIONS[2])
    l2 = _to_natural(l4_ref, ln4_ref, DILATIONS[1])
    l3 = _to_natural(l16_ref, ln16_ref, DILATIONS[2])
    l1 = l1_ref[...]
    m = jnp.maximum(jnp.maximum(l1, l2), l3)
    e1, e2, e3 = jnp.exp(l1 - m), jnp.exp(l2 - m), jnp.exp(l3 - m)
    inv = 1.0 / (e1 + e2 + e3)
    ex = expand_ref[...]

    def spread(wt):
        hi, lo = _split_bf16(wt)
        return _dot(hi, ex) + _dot(lo, ex)

    o = spread(e1 * inv) * o1_ref[...] + spread(e2 * inv) * o4 + spread(e3 * inv) * o16
    mix = _dot(o.astype(BF16), w_ref[...])
    h = _layer_norm(ALPHA * x_ref[...] + mix, g_ref[...], b_ref[...])
    h_ref[...] = h
    _route(h, wr_ref, br_ref, tri_ref, run_ref, route_ref, pos_ref, tab_ref)


def _router_operands(wg, bg, we, be):
    wr = jnp.zeros((D_MODEL, LANES), F32).at[:, :MOE_GROUPS].set(wg).at[:, EXPERT_LANE0:EXPERT_LANE0 + N_EXPERTS].set(we)
    wr_hi = wr.astype(BF16)
    wr_lo = (wr - wr_hi.astype(F32)).astype(BF16)
    br = jnp.zeros((1, LANES), F32).at[0, :MOE_GROUPS].set(bg).at[0, EXPERT_LANE0:EXPERT_LANE0 + N_EXPERTS].set(be)
    tri = jnp.tril(jnp.ones((ROW_TILE, ROW_TILE), F32), -1).astype(BF16)
    return jnp.concatenate([wr_hi, wr_lo], axis=1), br, tri


def _mix_call(kernel_fn, acts, x2, w_out, ln_g, ln_b, router, extra=(), scratch=()):
    n = x2.shape[0]
    tm = ROW_TILE
    wr, br, tri = _router_operands(*router)
    row = lambda i: (i, 0)
    const = lambda i: (0, 0)
    in_specs = [pl.BlockSpec((rows, a.shape[1]), row) for a, rows in acts]
    in_specs += [pl.BlockSpec(e.shape, const) for e in extra]
    in_specs += [pl.BlockSpec((tm, D_MODEL), row), pl.BlockSpec(w_out.shape, const),
                 pl.BlockSpec((1, D_MODEL), const), pl.BlockSpec((1, D_MODEL), const),
                 pl.BlockSpec((D_MODEL, 2 * LANES), const), pl.BlockSpec((1, LANES), const),
                 pl.BlockSpec((tm, tm), const)]
    return pl.pallas_call(
        kernel_fn,
        grid=(n // tm,),
        in_specs=in_specs,
        out_specs=[pl.BlockSpec((tm, D_MODEL), row), pl.BlockSpec((tm, LANES), row),
                   pl.BlockSpec((SUBLANES, tm), lambda i: (0, i)),
                   pl.BlockSpec((None, SUBLANES, LANES), lambda i: (i, 0, 0))],
        out_shape=[jax.ShapeDtypeStruct((n, D_MODEL), F32), jax.ShapeDtypeStruct((n, LANES), F32),
                   jax.ShapeDtypeStruct((SUBLANES, n), F32), jax.ShapeDtypeStruct((n // tm, SUBLANES, LANES), F32)],
        scratch_shapes=[pltpu.VMEM((1, LANES), F32), *scratch],
        compiler_params=_params("arbitrary"),
    )(*[a for a, _ in acts], *extra, x2, w_out.astype(BF16), ln_g.reshape(1, D_MODEL), ln_b.reshape(1, D_MODEL),
      wr, br, tri)


SLAB = D_MODEL // LANES
SEG_SIZES = tuple(1 << k for k in range(ROW_TILE.bit_length() - 1, -1, -1))
SEG_RARE = 64


def _segment_tables(tab, n_blocks):
    lanes = slice(EXPERT_LANE0, EXPERT_LANE0 + N_EXPERTS)
    cnt = tab[:, 0, lanes].astype(jnp.int32)
    run = tab[:, 1, lanes].astype(jnp.int32)
    src = tab[:, 2, lanes].astype(jnp.int32)
    counts = run[-1] + cnt[-1]
    padded = (counts + MOE_ROWS - 1) // MOE_ROWS * MOE_ROWS
    padded_end = jnp.cumsum(padded)
    dst = (padded_end - padded)[None, :] + run
    tail = jnp.arange(N_EXPERTS, dtype=jnp.int32) * MOE_ROWS + padded_end[-1]
    tail_cnt = jnp.where(tail < n_blocks * MOE_ROWS, MOE_ROWS, 0)
    cnt = jnp.concatenate([cnt, (padded - counts)[None, :], tail_cnt[None, :]], axis=0)
    src = jnp.concatenate([src, jnp.zeros((2, N_EXPERTS), jnp.int32)], axis=0)
    dst = jnp.concatenate([dst, (padded_end - padded + counts)[None, :], tail[None, :]], axis=0)
    block_start = jnp.arange(n_blocks, dtype=jnp.int32) * MOE_ROWS
    block_expert = jnp.minimum(jnp.sum((padded_end[None, :] <= block_start[:, None]).astype(jnp.int32), axis=1),
                               N_EXPERTS - 1)
    n_used = (padded_end[-1:] // MOE_ROWS).astype(jnp.int32)
    return (cnt.reshape(-1), src.reshape(-1), dst.reshape(-1)), block_expert, n_used


def _slab_rows(ref, start, size):
    return ref.at[pl.ds(pl.multiple_of(start * SLAB, SLAB), size * SLAB)]


def _for_each_segment(tabs, tile, fn):
    cnt_ref, src_ref, dst_ref = tabs

    def per_expert(e, carry):
        idx = tile * N_EXPERTS + e
        c, s, d = cnt_ref[idx], src_ref[idx], dst_ref[idx]

        def pieces(sizes):
            for size in sizes:
                @pl.when((c & size) != 0)
                def _():
                    off = c & (-2 * size)
                    fn(s + off, d + off, size)

        @pl.when(c >= SEG_RARE)
        def _():
            pieces([size for size in SEG_SIZES if size >= SEG_RARE])

        pieces([size for size in SEG_SIZES if size < SEG_RARE])
        return carry

    lax.fori_loop(0, N_EXPERTS, per_expert, 0)


def _to_slabs(val, ref):
    for lc in range(SLAB):
        ref[pl.ds(lc, val.shape[0], stride=SLAB), :] = val[:, lc * LANES:(lc + 1) * LANES]


def _from_slabs(ref):
    rows = ref.shape[0] // SLAB
    return jnp.concatenate([ref[pl.ds(lc, rows, stride=SLAB), :] for lc in range(SLAB)], axis=1)


def _dispatch_kernel(cnt_ref, src_ref, dst_ref, h_ref, pos_ref, xs_ref, sort_ref, sem):
    tile = pl.program_id(0)
    last = pl.num_programs(0) - 1
    tm = h_ref.shape[0]
    slot = tile % 2
    tabs = (cnt_ref, src_ref, dst_ref)

    def copy(k, s, d, size):
        return pltpu.make_async_copy(_slab_rows(sort_ref.at[k], s, size), _slab_rows(xs_ref, d, size), sem.at[k])

    j = lax.broadcasted_iota(jnp.int32, (2 * tm, tm), 0).astype(F32)
    pos = pos_ref[...]
    perm = jnp.where((j == pos[0:1, :]) | (j == pos[1:2, :]), 1.0, 0.0).astype(BF16)
    _to_slabs(_dot(perm, h_ref[...].astype(BF16)), sort_ref.at[slot])
    _for_each_segment(tabs, tile, lambda s, d, size: copy(slot, s, d, size).start())

    def wait_tile(k):
        pltpu.make_async_copy(sort_ref.at[k], xs_ref.at[pl.ds(0, 2 * tm * SLAB)], sem.at[k]).wait()

    @pl.when(tile > 0)
    def _():
        wait_tile(1 - slot)

    @pl.when(tile == last)
    def _():
        wait_tile(slot)
        sort_ref[slot, 0:MOE_ROWS * SLAB, :] = jnp.zeros((MOE_ROWS * SLAB, LANES), F32)
        for pseudo in (1, 2):
            _for_each_segment(tabs, tile + pseudo, lambda s, d, size: copy(slot, s, d, size).start())
        for pseudo in (1, 2):
            _for_each_segment(tabs, tile + pseudo, lambda s, d, size: copy(slot, s, d, size).wait())


def _dispatch(tabs, h, pos, n_slots):
    n = h.shape[0]
    tm = ROW_TILE
    grid_spec = pltpu.PrefetchScalarGridSpec(
        num_scalar_prefetch=3,
        grid=(n // tm,),
        in_specs=[pl.BlockSpec((tm, D_MODEL), lambda i, *_: (i, 0)), pl.BlockSpec((SUBLANES, tm), lambda i, *_: (0, i))],
        out_specs=pl.BlockSpec(memory_space=pl.ANY),
        scratch_shapes=[pltpu.VMEM((2, 2 * tm * SLAB, LANES), F32), pltpu.SemaphoreType.DMA((2,))],
    )
    return pl.pallas_call(
        _dispatch_kernel,
        grid_spec=grid_spec,
        out_shape=jax.ShapeDtypeStruct((n_slots * SLAB, LANES), F32),
        compiler_params=_params("arbitrary"),
    )(*tabs, h, pos)


def _expert_kernel(be_ref, nu_ref, xs_ref, wg_ref, wu_ref, wd_ref, ys_ref, wgb_ref, wub_ref, wdb_ref):
    i = pl.program_id(0)

    @pl.when((i == 0) | (be_ref[i] != be_ref[jnp.maximum(i - 1, 0)]))
    def _():
        wgb_ref[...] = wg_ref[...].astype(BF16)
        wub_ref[...] = wu_ref[...].astype(BF16)
        wdb_ref[...] = wd_ref[...].astype(BF16)

    @pl.when(i < nu_ref[0])
    def _():
        x = _from_slabs(xs_ref).astype(BF16)
        hg = _dot(x, wgb_ref[...])
        hu = _dot(x, wub_ref[...])
        hdn = hg * _sigmoid(hg) * hu
        _to_slabs(_dot(hdn.astype(BF16), wdb_ref[...]).astype(BF16).astype(F32), ys_ref)

    @pl.when(i >= nu_ref[0])
    def _():
        ys_ref[...] = jnp.zeros_like(ys_ref)


def _experts(block_expert, n_used, xs, w_gate, w_up, w_down, layer):
    blk = MOE_ROWS * SLAB
    row = lambda i, be, nu: (i, 0)
    used_row = lambda i, be, nu: (jnp.minimum(i, nu[0] - 1), 0)
    wsel = lambda i, be, nu: (layer, be[i], 0, 0)
    grid_spec = pltpu.PrefetchScalarGridSpec(
        num_scalar_prefetch=2,
        grid=(xs.shape[0] // blk,),
        in_specs=[pl.BlockSpec((blk, LANES), used_row),
                  pl.BlockSpec((None, None, D_MODEL, EXPERT_FF), wsel),
                  pl.BlockSpec((None, None, D_MODEL, EXPERT_FF), wsel),
                  pl.BlockSpec((None, None, EXPERT_FF, D_MODEL), wsel)],
        out_specs=pl.BlockSpec((blk, LANES), row),
        scratch_shapes=[pltpu.VMEM((D_MODEL, EXPERT_FF), BF16), pltpu.VMEM((D_MODEL, EXPERT_FF), BF16),
                        pltpu.VMEM((EXPERT_FF, D_MODEL), BF16)],
    )
    return pl.pallas_call(
        _expert_kernel,
        grid_spec=grid_spec,
        out_shape=jax.ShapeDtypeStruct(xs.shape, F32),
        compiler_params=_params("arbitrary"),
    )(block_expert, n_used, xs, w_gate, w_up, w_down)


def _combine_kernel(cnt_ref, src_ref, dst_ref, h_ref, route_ref, ys_ref, g_ref, b_ref, o_ref, sort_ref, sem):
    tile = pl.program_id(0)
    last = pl.num_programs(0) - 1
    tm = h_ref.shape[0]
    slot = tile % 2
    tabs = (cnt_ref, src_ref, dst_ref)

    def copy(k, s, d, size):
        return pltpu.make_async_copy(_slab_rows(ys_ref, d, size), _slab_rows(sort_ref.at[k], s, size), sem.at[k])

    @pl.when(tile == 0)
    def _():
        _for_each_segment(tabs, tile, lambda s, d, size: copy(slot, s, d, size).start())

    @pl.when(tile < last)
    def _():
        _for_each_segment(tabs, tile + 1, lambda s, d, size: copy(1 - slot, s, d, size).start())

    pltpu.make_async_copy(ys_ref.at[pl.ds(0, 2 * tm * SLAB)], sort_ref.at[slot], sem.at[slot]).wait()
    ysorted = _from_slabs(sort_ref.at[slot]).astype(BF16)
    r = route_ref[...]
    j = lax.broadcasted_iota(jnp.int32, (tm, 2 * tm), 1).astype(F32)
    y1 = _dot(jnp.where(j == r[:, 4:5], 1.0, 0.0).astype(BF16), ysorted)
    y2 = _dot(jnp.where(j == r[:, 5:6], 1.0, 0.0).astype(BF16), ysorted)
    ffn = y1 * r[:, 2:3] + y2 * r[:, 3:4]
    o_ref[...] = _layer_norm(ALPHA * h_ref[...] + ffn, g_ref[...], b_ref[...])


def _combine(tabs, h, route, ys, ln_g, ln_b):
    n = h.shape[0]
    tm = ROW_TILE
    row = lambda i, *_: (i, 0)
    const = lambda i, *_: (0, 0)
    grid_spec = pltpu.PrefetchScalarGridSpec(
        num_scalar_prefetch=3,
        grid=(n // tm,),
        in_specs=[pl.BlockSpec((tm, D_MODEL), row), pl.BlockSpec((tm, LANES), row), pl.BlockSpec(memory_space=pl.ANY),
                  pl.BlockSpec((1, D_MODEL), const), pl.BlockSpec((1, D_MODEL), const)],
        out_specs=pl.BlockSpec((tm, D_MODEL), row),
        scratch_shapes=[pltpu.VMEM((2, 2 * tm * SLAB, LANES), F32), pltpu.SemaphoreType.DMA((2,))],
    )
    return pl.pallas_call(
        _combine_kernel,
        grid_spec=grid_spec,
        out_shape=jax.ShapeDtypeStruct((n, D_MODEL), F32),
        compiler_params=_params("arbitrary"),
    )(*tabs, h, route, ys, ln_g.reshape(1, D_MODEL), ln_b.reshape(1, D_MODEL))


def _moe(h, route, pos, tab, w_gate, w_up, w_down, ln_g, ln_b, layer):
    n = h.shape[0]
    n_slots = n * 2 + N_EXPERTS * MOE_ROWS
    tabs, block_expert, n_used = _segment_tables(tab, n_slots // MOE_ROWS)
    xs = _dispatch(tabs, h, pos, n_slots)
    ys = _experts(block_expert, n_used, xs, w_gate, w_up, w_down, layer)
    return _combine(tabs, h, route, ys, ln_g, ln_b)


def _qkv_kernel(x_ref, w_ref, o1_ref, o4_ref, o16_ref, acc_ref):
    acc = _dot(x_ref[...].astype(BF16), w_ref[...])
    o1_ref[...] = acc.astype(BF16)
    nlc = acc_ref.shape[0]
    for lc in range(nlc):
        acc_ref[lc] = acc[:, lc * LANES:(lc + 1) * LANES]
    for o_ref, dil in ((o4_ref, DILATIONS[1]), (o16_ref, DILATIONS[2])):
        rows = o_ref.shape[0]
        for r in range(dil):
            for lc in range(nlc):
                col = (r * nlc + lc) * LANES
                o_ref[:, col:col + LANES] = acc_ref[lc, pl.ds(r, rows, stride=dil), :].astype(BF16)


def _qkv_proj(h, w_qkv):
    n = h.shape[0]
    tm = ROW_TILE
    wd = 3 * D_MODEL
    d4, d16 = DILATIONS[1], DILATIONS[2]
    row = lambda i: (i, 0)
    return pl.pallas_call(
        _qkv_kernel,
        grid=(n // tm,),
        in_specs=[pl.BlockSpec((tm, D_MODEL), row), pl.BlockSpec((D_MODEL, wd), lambda i: (0, 0))],
        out_specs=[pl.BlockSpec((tm, wd), row), pl.BlockSpec((tm // d4, d4 * wd), row),
                   pl.BlockSpec((tm // d16, d16 * wd), row)],
        out_shape=[jax.ShapeDtypeStruct((n, wd), BF16), jax.ShapeDtypeStruct((n // d4, d4 * wd), BF16),
                   jax.ShapeDtypeStruct((n // d16, d16 * wd), BF16)],
        scratch_shapes=[pltpu.VMEM((wd // LANES, tm, LANES), F32)],
        compiler_params=_params("arbitrary"),
    )(h, w_qkv.astype(BF16))


ATTN_QBLOCKS = 2


def _attn_kernel(q_ref, kc_ref, vc_ref, kp_ref, vp_ref, o_ref, lse_ref):
    t = ATTN_STEPS
    qi = lax.broadcasted_iota(jnp.int32, (t, 2 * t), 0)
    kj = lax.broadcasted_iota(jnp.int32, (t, 2 * t), 1)
    dist = t + qi - kj
    band = (dist >= 0) & (dist <= t)
    lane = lax.broadcasted_iota(jnp.int32, (t, LANES), 1)
    upper = lane >= ATTN_DIM
    lanes_per_head = LANES // ATTN_HEADS
    scale = jnp.asarray(ATTN_DIM ** -0.5, BF16)
    for qb in range(ATTN_QBLOCKS):
        rows = slice(qb * t, (qb + 1) * t)
        valid = band & ((kj >= t) | (pl.program_id(2) > 0)) if qb == 0 else band
        lse_blk = jnp.zeros((t, LANES), F32)
        for hp in range(ATTN_HEADS // 2):
            cs = slice(hp * LANES, (hp + 1) * LANES)
            q2 = q_ref[rows, cs] * scale
            if qb == 0:
                k2 = jnp.concatenate([kp_ref[:, cs], kc_ref[0:t, cs]], axis=0)
                v2 = jnp.concatenate([vp_ref[:, cs], vc_ref[0:t, cs]], axis=0)
            else:
                k2 = kc_ref[(qb - 1) * t:(qb + 1) * t, cs]
                v2 = vc_ref[(qb - 1) * t:(qb + 1) * t, cs]
            halves = []
            for sub in range(2):
                mine = upper if sub else jnp.logical_not(upper)
                qm = jnp.where(mine, q2, jnp.zeros_like(q2))
                s = lax.dot_general(qm, k2, NT_DIMS, preferred_element_type=F32)
                s = jnp.where(valid, s, NEG)
                m = jnp.max(s, axis=-1, keepdims=True)
                p = jnp.exp(s - m)
                l = jnp.sum(p, axis=-1, keepdims=True)
                halves.append(_dot(p.astype(BF16), v2) / l)
                head = 2 * hp + sub
                lse_blk = jnp.where(lane // lanes_per_head == head, m + jnp.log(l), lse_blk)
            o_ref[rows, cs] = jnp.where(upper, halves[1], halves[0])
        lse_ref[rows, :] = lse_blk


def _attn_branch(qkv_view, bsz, seq, dil):
    length = seq // dil
    t = ATTN_STEPS
    tq = t * ATTN_QBLOCKS
    qkv_v = qkv_view.reshape(bsz, length, dil * 3 * D_MODEL)
    cur = lambda part: (lambda b, r, i: (b, i, 3 * r + part))
    prev = lambda part: (lambda b, r, i: (b, jnp.maximum(i * ATTN_QBLOCKS - 1, 0), 3 * r + part))
    blk = (None, tq, D_MODEL)
    pblk = (None, t, D_MODEL)
    o, lse = pl.pallas_call(
        _attn_kernel,
        grid=(bsz, dil, length // tq),
        in_specs=[pl.BlockSpec(blk, cur(0)), pl.BlockSpec(blk, cur(1)), pl.BlockSpec(blk, cur(2)),
                  pl.BlockSpec(pblk, prev(1)), pl.BlockSpec(pblk, prev(2))],
        out_specs=[pl.BlockSpec(blk, lambda b, r, i: (b, i, r)), pl.BlockSpec((None, tq, LANES), lambda b, r, i: (b, i, r))],
        out_shape=[jax.ShapeDtypeStruct((bsz, length, dil * D_MODEL), F32),
                   jax.ShapeDtypeStruct((bsz, length, dil * LANES), F32)],
        compiler_params=_params("arbitrary", "arbitrary", "arbitrary"),
    )(qkv_v, qkv_v, qkv_v, qkv_v, qkv_v)
    return o.reshape(bsz * length, dil * D_MODEL), lse.reshape(bsz * length, dil * LANES)


def kernel(x, w_in_ab, conv_qkv, gdn_a_log, gdn_dt_bias, gdn_norm, s5_lam_re, s5_lam_im, s5_log_dt, s5_b_re, s5_b_im, s5_c_re, s5_c_im, s5_d, s5_w_glu, w_out_ab, w_qkv_c, w_out_c, ln_mix_g, ln_mix_b, router_group_w, router_group_b, router_expert_w, router_expert_b, moe_w_gate, moe_w_up, moe_w_down, ln_ffn_g, ln_ffn_b):
    bsz, seq, d = x.shape
    n = bsz * seq
    x2 = x.reshape(n, d)
    tm = ROW_TILE

    w_in = w_in_ab[0]
    nq = 4 * GDN_W
    ba_cols = jnp.zeros((d, LANES), F32).at[:, :2 * GDN_HEADS].set(w_in[:, nq:nq + 2 * GDN_HEADS])
    w_cat = jnp.concatenate([w_in[:, :nq], w_in[:, nq + 2 * GDN_HEADS:], ba_cols], axis=1).astype(BF16)
    qkv, z, u, ba = _inproj(x2, w_cat)
    ya = _gdn(qkv, z, ba, conv_qkv[0], gdn_a_log[0], gdn_dt_bias[0], gdn_norm[0], bsz, seq)
    yb = _s5(u.reshape(bsz, seq, S5_W), s5_lam_re[0], s5_lam_im[0], s5_log_dt[0], s5_b_re[0], s5_b_im[0],
             s5_c_re[0], s5_c_im[0], s5_d[0], s5_w_glu[0]).reshape(n, S5_W)
    router0 = (router_group_w[0], router_group_b[0], router_expert_w[0], router_expert_b[0])
    h, route, pos, tab = _mix_call(_mix_ab_kernel, ((ya, tm), (yb, tm)), x2, w_out_ab[0], ln_mix_g[0], ln_mix_b[0], router0)
    h = _moe(h, route, pos, tab, moe_w_gate, moe_w_up, moe_w_down, ln_ffn_g[0], ln_ffn_b[0], 0)

    acts = []
    for qkv_view, dil in zip(_qkv_proj(h, w_qkv_c[0]), DILATIONS):
        acts.append(_attn_branch(qkv_view, bsz, seq, dil))
    lanes_per_head = LANES // ATTN_HEADS
    expand = (jnp.arange(LANES)[:, None] == (jnp.arange(D_MODEL)[None, :] // ATTN_DIM) * lanes_per_head).astype(BF16)
    router1 = (router_group_w[1], router_group_b[1], router_expert_w[1], router_expert_b[1])
    d4, d16 = DILATIONS[1], DILATIONS[2]
    (o1, l1), (o4, l4), (o16, l16) = acts
    h, route, pos, tab = _mix_call(
        _mix_c_kernel, ((o1, tm), (o4, tm // d4), (o16, tm // d16), (l1, tm), (l4, tm // d4), (l16, tm // d16)),
        h, w_out_c[0], ln_mix_g[1], ln_mix_b[1], router1, extra=(expand,),
        scratch=(pltpu.VMEM((D_MODEL // LANES, tm, LANES), F32), pltpu.VMEM((D_MODEL // LANES, tm, LANES), F32),
                 pltpu.VMEM((1, tm, LANES), F32), pltpu.VMEM((1, tm, LANES), F32)))
    h = _moe(h, route, pos, tab, moe_w_gate, moe_w_up, moe_w_down, ln_ffn_g[1], ln_ffn_b[1], 1)
    return h.reshape(bsz, seq, d)
```

```python
import functools
import math

import jax
import jax.numpy as jnp
from jax import lax
from jax.experimental import pallas as pl
from jax.experimental.pallas import tpu as pltpu

F32 = jnp.float32
BF16 = jnp.bfloat16
HI = lax.Precision.HIGHEST

D_MODEL = 1024
DEPTH = 2
ALPHA = (2 * DEPTH) ** 0.25
LN_EPS = 1e-5

GDN_HEADS = 4
GDN_DIM = 128
GDN_W = GDN_HEADS * GDN_DIM
CONV_K = 4
GDN_CHUNK = 64
GDN_BATCH = 4

S5_W = 512
S5_GROUP_CH = 16
S5_GROUPS = 32
S5_STATE = 64
S5_SUPER = 4
S5_SUPER_CH = S5_W // S5_SUPER
S5_SUPER_ST = S5_GROUPS * S5_STATE // S5_SUPER

ATTN_HEADS = 16
ATTN_DIM = 64
ATTN_STEPS = 128
DILATIONS = (1, 4, 16)

MOE_GROUPS = 4
EXPERTS_PER_GROUP = 8
N_EXPERTS = 32
EXPERT_FF = 512
EXPERT_LANE0 = MOE_GROUPS

LANES = 128
SUBLANES = 8
VMEM_LIMIT = 56 * 1024 * 1024

ROW_TILE = 512
S5_STEPS = 128
MOE_ROWS = 512
GATHER_TILE = 256

NEG = -1e30
NT_DIMS = (((1,), (1,)), ((), ()))
TN_DIMS = (((0,), (0,)), ((), ()))


def _params(*sem):
    return pltpu.CompilerParams(dimension_semantics=sem, vmem_limit_bytes=VMEM_LIMIT)


def _sigmoid(x):
    return 1.0 / (1.0 + jnp.exp(-x))


def _softplus(x):
    return jnp.maximum(x, 0.0) + jnp.log(1.0 + jnp.exp(-jnp.abs(x)))


def _layer_norm(r, g, b):
    mu = jnp.mean(r, axis=-1, keepdims=True)
    c = r - mu
    var = jnp.mean(c * c, axis=-1, keepdims=True)
    return c * lax.rsqrt(var + LN_EPS) * g + b


def _dot(a, b, precision=None):
    return jnp.dot(a, b, precision=precision, preferred_element_type=F32)


def _mm(a, b, dims=None):
    a, b = a.astype(BF16), b.astype(BF16)
    if dims is None:
        return jnp.dot(a, b, preferred_element_type=F32)
    return lax.dot_general(a, b, dims, preferred_element_type=F32)


def _split_bf16(x):
    hi = x.astype(BF16)
    return hi, (x - hi.astype(F32)).astype(BF16)


IN_COLS = 3 * GDN_W + GDN_W + S5_W + LANES


def _inproj_kernel(x_ref, w_ref, qkv_ref, z_ref, u_ref, ba_ref):
    x = x_ref[...].astype(BF16)
    qkv_ref[...] = _dot(x, w_ref[:, 0:1536])
    z_ref[...] = _dot(x, w_ref[:, 1536:2048])
    u_ref[...] = _dot(x, w_ref[:, 2048:2560])
    ba_ref[...] = _dot(x, w_ref[:, 2560:2688])


def _inproj(x2, w_cat):
    n = x2.shape[0]
    tm = ROW_TILE
    row = lambda i: (i, 0)
    return pl.pallas_call(
        _inproj_kernel,
        grid=(n // tm,),
        in_specs=[pl.BlockSpec((tm, D_MODEL), row), pl.BlockSpec((D_MODEL, IN_COLS), lambda i: (0, 0))],
        out_specs=[pl.BlockSpec((tm, 3 * GDN_W), row), pl.BlockSpec((tm, GDN_W), row),
                   pl.BlockSpec((tm, S5_W), row), pl.BlockSpec((tm, LANES), row)],
        out_shape=[jax.ShapeDtypeStruct((n, 3 * GDN_W), F32), jax.ShapeDtypeStruct((n, GDN_W), F32),
                   jax.ShapeDtypeStruct((n, S5_W), F32), jax.ShapeDtypeStruct((n, LANES), F32)],
        compiler_params=_params("arbitrary"),
    )(x2, w_cat)


def _l2norm(t):
    return t * lax.rsqrt(jnp.sum(t * t, axis=-1, keepdims=True) + 1e-6)


def _each(fn, *lists):
    return [fn(*args) for args in zip(*lists)]


def _unit_lower_inverse(lows, same16, cross32, cross64, eye):
    ps = [jnp.where(same16, low, 0.0) for low in lows]
    ts = [eye - p for p in ps]
    for _ in range(3):
        ps = _each(lambda p: _mm(p, p), ps)
        ts = _each(lambda t, p: _mm(t, eye + p), ts, ps)
    for cross in (cross32, cross64):
        tc = _each(lambda t, low: _mm(t, jnp.where(cross, low, 0.0)), ts, lows)
        ts = _each(lambda t, x: t - _mm(x, t), ts, tc)
    return ts


def _gdn_kernel(qkv_ref, z_ref, ba_ref, convw_ref, alog_ref, dtb_ref, normw_ref, tri_ref, o_ref,
                state_ref, tail_ref):
    c = GDN_CHUNK

    @pl.when(pl.program_id(1) == 0)
    def _():
        state_ref[...] = jnp.zeros_like(state_ref)
        tail_ref[...] = jnp.zeros_like(tail_ref)

    w = convw_ref[...]
    row = lax.broadcasted_iota(jnp.int32, (c, 1), 0)
    ri = lax.broadcasted_iota(jnp.int32, (c, c), 0)
    ci = lax.broadcasted_iota(jnp.int32, (c, c), 1)
    causal = ri >= ci
    strict = ri > ci
    eye = (ri == ci).astype(F32)
    same16 = (ri // 16) == (ci // 16)
    same32 = (ri // 32) == (ci // 32)
    cross32 = same32 & jnp.logical_not(same16)
    cross64 = jnp.logical_not(same32)
    normw = normw_ref[...]
    a_neg = -jnp.exp(alog_ref[...])
    zpad = jnp.zeros((c - SUBLANES, 3 * GDN_W), F32)
    head_rows = (lax.broadcasted_iota(jnp.int32, (SUBLANES, LANES), 1)
                 == lax.broadcasted_iota(jnp.int32, (SUBLANES, LANES), 0) + GDN_HEADS).astype(F32)

    chains, qs, ks, kbs, vbs, kws, decays, egcs, kdecs, glasts = [], [], [], [], [], [], [], [], [], []
    for bi in range(GDN_BATCH):
        x = qkv_ref[bi]
        tail = tail_ref[bi]
        conv = x * w[CONV_K - 1:CONV_K, :]
        for j in range(1, CONV_K):
            cur = pltpu.roll(x, j, axis=0)
            prev = jnp.concatenate([pltpu.roll(tail, j, axis=0), zpad], axis=0)
            conv = conv + jnp.where(row < j, prev, cur) * w[CONV_K - 1 - j:CONV_K - j, :]
        tail_ref[bi] = x[c - SUBLANES:, :]
        qkv = conv * _sigmoid(conv)

        ba = ba_ref[bi]
        beta_all = _sigmoid(ba)
        g_all = a_neg * _softplus(ba + dtb_ref[...])
        gc_all = _dot(tri_ref[...], g_all, HI)
        egc_all = jnp.exp(gc_all)
        gc_rows = lax.dot_general(head_rows, gc_all, NT_DIMS, precision=HI, preferred_element_type=F32)

        for h in range(GDN_HEADS):
            q = _l2norm(qkv[:, h * GDN_DIM:(h + 1) * GDN_DIM]) * (GDN_DIM ** -0.5)
            k = _l2norm(qkv[:, GDN_W + h * GDN_DIM:GDN_W + (h + 1) * GDN_DIM])
            v = qkv[:, 2 * GDN_W + h * GDN_DIM:2 * GDN_W + (h + 1) * GDN_DIM]
            gl = GDN_HEADS + h
            beta = beta_all[:, h:h + 1]
            gc = gc_all[:, gl:gl + 1]
            egc = egc_all[:, gl:gl + 1]
            gc_last = gc_all[c - 1:c, gl:gl + 1]
            kb = k * beta
            chains.append((bi, h))
            qs.append(q)
            ks.append(k)
            kbs.append(kb)
            vbs.append(v * beta)
            kws.append(kb * egc)
            decays.append(jnp.exp(jnp.where(causal, gc - gc_rows[h:h + 1, :], NEG)))
            egcs.append(egc)
            kdecs.append(k * jnp.exp(gc_last - gc))
            glasts.append(jnp.exp(gc_last))

    kk = _each(lambda kb, q, k: _mm(jnp.concatenate([kb, q], axis=0), k, NT_DIMS), kbs, qs, ks)
    lows = _each(lambda a, dec: jnp.where(strict, a[:c] * dec, 0.0), kk, decays)
    qks = _each(lambda a, dec: jnp.where(causal, a[c:] * dec, 0.0), kk, decays)
    ts = _unit_lower_inverse(lows, same16, cross32, cross64, eye)
    sols = _each(lambda t, vb, kw: _mm(t, jnp.concatenate([vb, kw], axis=1)), ts, vbs, kws)
    states = [state_ref[bi, h] for bi, h in chains]
    wss = _each(lambda sol, q, egc, s: _mm(jnp.concatenate([sol[:, GDN_DIM:], q * egc], axis=0), s),
                sols, qs, egcs, states)
    v_news = _each(lambda sol, ws: sol[:, :GDN_DIM] - ws[:c], sols, wss)
    outs = _each(lambda ws, qk, v_new: ws[c:] + _mm(qk, v_new), wss, qks, v_news)
    upds = _each(lambda kd, v_new: _mm(kd, v_new, TN_DIMS), kdecs, v_news)
    for (bi, h), s, gl, upd, o in zip(chains, states, glasts, upds, outs):
        hs = slice(h * GDN_DIM, (h + 1) * GDN_DIM)
        state_ref[bi, h] = s * gl + upd
        zh = z_ref[bi, :, hs]
        o_ref[bi, :, hs] = o * lax.rsqrt(jnp.mean(o * o, axis=-1, keepdims=True) + 1e-6) * normw * (zh * _sigmoid(zh))


def _gdn(qkv, z, ba, conv_w, a_log, dt_bias, norm_w, bsz, seq):
    c = GDN_CHUNK
    nb = GDN_BATCH
    convw = jnp.zeros((SUBLANES, 3 * GDN_W), F32).at[:CONV_K].set(conv_w)
    alog = jnp.zeros((1, LANES), F32).at[0, GDN_HEADS:2 * GDN_HEADS].set(a_log)
    dtb = jnp.zeros((1, LANES), F32).at[0, GDN_HEADS:2 * GDN_HEADS].set(dt_bias)
    tri = jnp.tril(jnp.ones((c, c), F32))
    row = lambda b, i: (b, i, 0)
    const = lambda b, i: (0, 0)
    out = pl.pallas_call(
        _gdn_kernel,
        grid=(bsz // nb, seq // c),
        in_specs=[pl.BlockSpec((nb, c, 3 * GDN_W), row), pl.BlockSpec((nb, c, GDN_W), row),
                  pl.BlockSpec((nb, c, LANES), row),
                  pl.BlockSpec((SUBLANES, 3 * GDN_W), const), pl.BlockSpec((1, LANES), const),
                  pl.BlockSpec((1, LANES), const), pl.BlockSpec((1, GDN_DIM), const), pl.BlockSpec((c, c), const)],
        out_specs=pl.BlockSpec((nb, c, GDN_W), row),
        out_shape=jax.ShapeDtypeStruct((bsz, seq, GDN_W), F32),
        scratch_shapes=[pltpu.VMEM((nb, GDN_HEADS, GDN_DIM, GDN_DIM), F32),
                        pltpu.VMEM((nb, SUBLANES, 3 * GDN_W), F32)],
        compiler_params=_params("arbitrary", "arbitrary"),
    )(qkv.reshape(bsz, seq, 3 * GDN_W), z.reshape(bsz, seq, GDN_W), ba.reshape(bsz, seq, LANES),
      convw, alog, dtb, norm_w.reshape(1, GDN_DIM), tri)
    return out.reshape(bsz * seq, GDN_W)


def _gelu_tanh(x):
    return x * (0.5 * (1.0 + jnp.tanh(math.sqrt(2.0 / math.pi) * (x + 0.044715 * (x * x * x)))))


def _s5_kernel(u_ref, wb_ref, a_ref, wc_ref, d_ref, wglu_ref, y_ref, utb_ref, ytb_ref, bu_ref, h_ref, *, steps):
    @pl.when(pl.program_id(0) == 0)
    def _():
        h_ref[...] = jnp.zeros_like(h_ref)

    nb = h_ref.shape[0]
    nlc = S5_W // LANES
    for b in range(nb):
        for lc in range(nlc):
            utb_ref[lc, pl.ds(b, steps, stride=nb), :] = u_ref[b, :, lc * LANES:(lc + 1) * LANES]
    u = jnp.concatenate([utb_ref[lc] for lc in range(nlc)], axis=1)
    ub = u.astype(BF16)
    st = S5_SUPER_ST
    for j in range(S5_SUPER):
        bu_ref[:, 2 * st * j:2 * st * (j + 1)] = _dot(ub[:, S5_SUPER_CH * j:S5_SUPER_CH * (j + 1)], wb_ref[j])

    for j in range(S5_SUPER):
        re = slice(2 * st * j, 2 * st * j + st)
        im = slice(2 * st * j + st, 2 * st * (j + 1))
        ar = jnp.broadcast_to(a_ref[0:1, st * j:st * (j + 1)], (nb, st))
        ai = jnp.broadcast_to(a_ref[1:2, st * j:st * (j + 1)], (nb, st))

        def body(t, carry, re=re, im=im, ar=ar, ai=ai):
            hr, hi = carry
            rows = pl.ds(pl.multiple_of(t * nb, nb), nb)
            nr = ar * hr - ai * hi + bu_ref[rows, re]
            ni = ar * hi + ai * hr + bu_ref[rows, im]
            bu_ref[rows, re] = nr
            bu_ref[rows, im] = ni
            return nr, ni

        hr, hi = lax.fori_loop(0, steps, body, (h_ref[:, re], h_ref[:, im]), unroll=4)
        h_ref[:, re] = hr
        h_ref[:, im] = hi

    y = jnp.concatenate(
        [_dot(bu_ref[:, 2 * st * j:2 * st * (j + 1)].astype(BF16), wc_ref[j]) for j in range(S5_SUPER)], axis=1)
    y = _gelu_tanh(y + d_ref[...] * u)
    y = y * _sigmoid(_dot(y.astype(BF16), wglu_ref[...]))
    for lc in range(nlc):
        ytb_ref[lc] = y[:, lc * LANES:(lc + 1) * LANES]
    for b in range(nb):
        for lc in range(nlc):
            y_ref[b, :, lc * LANES:(lc + 1) * LANES] = ytb_ref[lc, pl.ds(b, steps, stride=nb), :]


def _s5(u3, lam_re, lam_im, log_dt, b_re, b_im, c_re, c_im, d_skip, w_glu):
    bsz, seq, _ = u3.shape
    dt = jnp.exp(log_dt)[:, None]
    mag = jnp.exp(lam_re * dt)
    a_re, a_im = mag * jnp.cos(lam_im * dt), mag * jnp.sin(lam_im * dt)
    den = lam_re * lam_re + lam_im * lam_im
    nr, ni = a_re - 1.0, a_im
    cr = (nr * lam_re + ni * lam_im) / den
    ci = (ni * lam_re - nr * lam_im) / den
    bb_re = cr[..., None] * b_re - ci[..., None] * b_im
    bb_im = cr[..., None] * b_im + ci[..., None] * b_re
    gps = S5_GROUPS // S5_SUPER
    eye = jnp.eye(gps, dtype=F32)

    def in_blockdiag(t):
        t = t.reshape(S5_SUPER, gps, S5_STATE, S5_GROUP_CH)
        return jnp.einsum('jgph,gk->jghkp', t, eye).reshape(S5_SUPER, S5_SUPER_CH, S5_SUPER_ST)

    def out_blockdiag(t):
        t = t.reshape(S5_SUPER, gps, S5_GROUP_CH, S5_STATE)
        return jnp.einsum('jgkp,gm->jgpmk', t, eye).reshape(S5_SUPER, S5_SUPER_ST, S5_SUPER_CH)

    wb = jnp.concatenate([in_blockdiag(bb_re), in_blockdiag(bb_im)], axis=2).astype(BF16)
    wc = jnp.concatenate([out_blockdiag(c_re), -out_blockdiag(c_im)], axis=1).astype(BF16)
    nstate = S5_GROUPS * S5_STATE
    a = jnp.stack([a_re.reshape(nstate), a_im.reshape(nstate)], axis=0)

    steps = S5_STEPS
    rows = steps * bsz
    blk = lambda i: (0, i, 0)
    const2 = lambda i: (0, 0)
    const3 = lambda i: (0, 0, 0)
    return pl.pallas_call(
        functools.partial(_s5_kernel, steps=steps),
        grid=(seq // steps,),
        in_specs=[pl.BlockSpec((bsz, steps, S5_W), blk), pl.BlockSpec(wb.shape, const3),
                  pl.BlockSpec((2, nstate), const2), pl.BlockSpec(wc.shape, const3),
                  pl.BlockSpec((1, S5_W), const2), pl.BlockSpec((S5_W, S5_W), const2)],
        out_specs=pl.BlockSpec((bsz, steps, S5_W), blk),
        out_shape=jax.ShapeDtypeStruct((bsz, seq, S5_W), F32),
        scratch_shapes=[pltpu.VMEM((S5_W // LANES, rows, LANES), F32), pltpu.VMEM((S5_W // LANES, rows, LANES), F32),
                        pltpu.VMEM((rows, 2 * nstate), F32), pltpu.VMEM((bsz, 2 * nstate), F32)],
        compiler_params=_params("arbitrary"),
    )(u3, wb, a, wc, d_skip.reshape(1, S5_W), w_glu.astype(BF16))


def _route(h, wr_ref, br_ref, tri_ref, run_ref, route_ref, pos_ref, tab_ref):
    @pl.when(pl.program_id(0) == 0)
    def _():
        run_ref[...] = jnp.zeros_like(run_ref)

    tm = h.shape[0]
    h_hi, h_lo = _split_bf16(h)
    both = _dot(h_hi, wr_ref[...])
    logits = both[:, :LANES] + both[:, LANES:] + _dot(h_lo, wr_ref[:, :LANES]) + br_ref[...]
    lane = lax.broadcasted_iota(jnp.int32, (tm, LANES), 1)
    lanef = lane.astype(F32)
    big = float(LANES)

    gmask = lane < MOE_GROUPS
    gmax = jnp.max(jnp.where(gmask, logits, NEG), axis=-1, keepdims=True)
    gsum = jnp.sum(jnp.where(gmask, jnp.exp(logits - gmax), 0.0), axis=-1, keepdims=True)
    p_group = 1.0 / gsum
    gidx = jnp.min(jnp.where(gmask & (logits == gmax), lanef, big), axis=-1, keepdims=True)

    lo = EXPERT_LANE0 + EXPERTS_PER_GROUP * gidx
    emask = (lanef >= lo) & (lanef < lo + EXPERTS_PER_GROUP)
    el = jnp.where(emask, logits, NEG)
    t1 = jnp.max(el, axis=-1, keepdims=True)
    i1 = jnp.min(jnp.where(emask & (el == t1), lanef, big), axis=-1, keepdims=True)
    emask2 = emask & (lanef != i1)
    el2 = jnp.where(emask2, logits, NEG)
    t2 = jnp.max(el2, axis=-1, keepdims=True)
    i2 = jnp.min(jnp.where(emask2 & (el2 == t2), lanef, big), axis=-1, keepdims=True)
    e2 = jnp.exp(t2 - t1)
    gate1 = p_group / (1.0 + e2)
    gate2 = p_group * e2 / (1.0 + e2)

    oh1 = lanef == i1
    oh2 = lanef == i2
    oh = jnp.where(oh1 | oh2, 1.0, 0.0)
    cnt = jnp.sum(oh, axis=0, keepdims=True)
    ei = lax.broadcasted_iota(jnp.int32, (LANES, LANES), 0)
    ej = lax.broadcasted_iota(jnp.int32, (LANES, LANES), 1)
    start = _dot(jnp.broadcast_to(cnt, (SUBLANES, LANES)), (ei < ej).astype(F32), HI)[0:1]
    where = _dot(tri_ref[...], oh.astype(BF16)) + start
    pos1 = jnp.sum(jnp.where(oh1, where, 0.0), axis=-1, keepdims=True)
    pos2 = jnp.sum(jnp.where(oh2, where, 0.0), axis=-1, keepdims=True)

    run = run_ref[...]
    srow = lax.broadcasted_iota(jnp.int32, (SUBLANES, LANES), 0)
    slane = lax.broadcasted_iota(jnp.int32, (SUBLANES, LANES), 1)
    tab_ref[...] = jnp.where(srow == 0, cnt, jnp.where(srow == 1, run, jnp.where(srow == 2, start, 0.0)))
    run_ref[...] = run + cnt

    route = jnp.where(lane == 2, gate1, 0.0)
    route = jnp.where(lane == 3, gate2, route)
    route = jnp.where(lane == 4, pos1, route)
    route = jnp.where(lane == 5, pos2, route)
    route_ref[...] = route
    pick = (slane == srow + 4).astype(F32)
    pos_ref[...] = lax.dot_general(pick, route, NT_DIMS, precision=HI, preferred_element_type=F32)


def _mix_ab_kernel(ya_ref, yb_ref, x_ref, w_ref, g_ref, b_ref, wr_ref, br_ref, tri_ref,
                   h_ref, route_ref, pos_ref, tab_ref, run_ref):
    mix = _dot(ya_ref[...].astype(BF16), w_ref[0:GDN_W, :]) + _dot(yb_ref[...].astype(BF16), w_ref[GDN_W:, :])
    h = _layer_norm(ALPHA * x_ref[...] + mix, g_ref[...], b_ref[...])
    h_ref[...] = h
    _route(h, wr_ref, br_ref, tri_ref, run_ref, route_ref, pos_ref, tab_ref)


def _to_natural(view_ref, nat_ref, dil):
    rows = view_ref.shape[0]
    nlc = nat_ref.shape[0]
    for r in range(dil):
        for lc in range(nlc):
            col = (r * nlc + lc) * LANES
            nat_ref[lc, pl.ds(r, rows, stride=dil), :] = view_ref[:, col:col + LANES]
    return jnp.concatenate([nat_ref[lc] for lc in range(nlc)], axis=1)


def _mix_c_kernel(o1_ref, o4_ref, o16_ref, l1_ref, l4_ref, l16_ref, expand_ref, x_ref, w_ref, g_ref, b_ref,
                  wr_ref, br_ref, tri_ref, h_ref, route_ref, pos_ref, tab_ref, run_ref,
                  on4_ref, on16_ref, ln4_ref, ln16_ref):
    o4 = _to_natural(o4_ref, on4_ref, DILATIONS[1])
    o16 = _to_natural(o16_ref, on16_ref, DILATIONS[2])
    l2 = _to_natural(l4_ref, ln4_ref, DILATIONS[1])
    l3 = _to_natural(l16_ref, ln16_ref, DILATIONS[2])
    l1 = l1_ref[...]
    m = jnp.maximum(jnp.maximum(l1, l2), l3)
    e1, e2, e3 = jnp.exp(l1 - m), jnp.exp(l2 - m), jnp.exp(l3 - m)
    inv = 1.0 / (e1 + e2 + e3)
    ex = expand_ref[...]

    def spread(wt):
        hi, lo = _split_bf16(wt)
        return _dot(hi, ex) + _dot(lo, ex)

    o = spread(e1 * inv) * o1_ref[...] + spread(e2 * inv) * o4 + spread(e3 * inv) * o16
    mix = _dot(o.astype(BF16), w_ref[...])
    h = _layer_norm(ALPHA * x_ref[...] + mix, g_ref[...], b_ref[...])
    h_ref[...] = h
    _route(h, wr_ref, br_ref, tri_ref, run_ref, route_ref, pos_ref, tab_ref)


def _router_operands(wg, bg, we, be):
    wr = jnp.zeros((D_MODEL, LANES), F32).at[:, :MOE_GROUPS].set(wg).at[:, EXPERT_LANE0:EXPERT_LANE0 + N_EXPERTS].set(we)
    wr_hi = wr.astype(BF16)
    wr_lo = (wr - wr_hi.astype(F32)).astype(BF16)
    br = jnp.zeros((1, LANES), F32).at[0, :MOE_GROUPS].set(bg).at[0, EXPERT_LANE0:EXPERT_LANE0 + N_EXPERTS].set(be)
    tri = jnp.tril(jnp.ones((ROW_TILE, ROW_TILE), F32), -1).astype(BF16)
    return jnp.concatenate([wr_hi, wr_lo], axis=1), br, tri


def _mix_call(kernel_fn, acts, x2, w_out, ln_g, ln_b, router, extra=(), scratch=()):
    n = x2.shape[0]
    tm = ROW_TILE
    wr, br, tri = _router_operands(*router)
    row = lambda i: (i, 0)
    const = lambda i: (0, 0)
    in_specs = [pl.BlockSpec((rows, a.shape[1]), row) for a, rows in acts]
    in_specs += [pl.BlockSpec(e.shape, const) for e in extra]
    in_specs += [pl.BlockSpec((tm, D_MODEL), row), pl.BlockSpec(w_out.shape, const),
                 pl.BlockSpec((1, D_MODEL), const), pl.BlockSpec((1, D_MODEL), const),
                 pl.BlockSpec((D_MODEL, 2 * LANES), const), pl.BlockSpec((1, LANES), const),
                 pl.BlockSpec((tm, tm), const)]
    return pl.pallas_call(
        kernel_fn,
        grid=(n // tm,),
        in_specs=in_specs,
        out_specs=[pl.BlockSpec((tm, D_MODEL), row), pl.BlockSpec((tm, LANES), row),
                   pl.BlockSpec((SUBLANES, tm), lambda i: (0, i)),
                   pl.BlockSpec((None, SUBLANES, LANES), lambda i: (i, 0, 0))],
        out_shape=[jax.ShapeDtypeStruct((n, D_MODEL), F32), jax.ShapeDtypeStruct((n, LANES), F32),
                   jax.ShapeDtypeStruct((SUBLANES, n), F32), jax.ShapeDtypeStruct((n // tm, SUBLANES, LANES), F32)],
        scratch_shapes=[pltpu.VMEM((1, LANES), F32), *scratch],
        compiler_params=_params("arbitrary"),
    )(*[a for a, _ in acts], *extra, x2, w_out.astype(BF16), ln_g.reshape(1, D_MODEL), ln_b.reshape(1, D_MODEL),
      wr, br, tri)


SLAB = D_MODEL // LANES
SEG_SIZES = tuple(1 << k for k in range(ROW_TILE.bit_length() - 1, -1, -1))
SEG_RARE = 64


def _segment_tables(tab, n_blocks):
    lanes = slice(EXPERT_LANE0, EXPERT_LANE0 + N_EXPERTS)
    cnt = tab[:, 0, lanes].astype(jnp.int32)
    run = tab[:, 1, lanes].astype(jnp.int32)
    src = tab[:, 2, lanes].astype(jnp.int32)
    counts = run[-1] + cnt[-1]
    padded = (counts + MOE_ROWS - 1) // MOE_ROWS * MOE_ROWS
    padded_end = jnp.cumsum(padded)
    dst = (padded_end - padded)[None, :] + run
    tail = jnp.arange(N_EXPERTS, dtype=jnp.int32) * MOE_ROWS + padded_end[-1]
    tail_cnt = jnp.where(tail < n_blocks * MOE_ROWS, MOE_ROWS, 0)
    cnt = jnp.concatenate([cnt, (padded - counts)[None, :], tail_cnt[None, :]], axis=0)
    src = jnp.concatenate([src, jnp.zeros((2, N_EXPERTS), jnp.int32)], axis=0)
    dst = jnp.concatenate([dst, (padded_end - padded + counts)[None, :], tail[None, :]], axis=0)
    block_start = jnp.arange(n_blocks, dtype=jnp.int32) * MOE_ROWS
    block_expert = jnp.minimum(jnp.sum((padded_end[None, :] <= block_start[:, None]).astype(jnp.int32), axis=1),
                               N_EXPERTS - 1)
    n_used = (padded_end[-1:] // MOE_ROWS).astype(jnp.int32)
    return (cnt.reshape(-1), src.reshape(-1), dst.reshape(-1)), block_expert, n_used


def _slab_rows(ref, start, size):
    return ref.at[pl.ds(pl.multiple_of(start * SLAB, SLAB), size * SLAB)]


def _for_each_segment(tabs, tile, fn):
    cnt_ref, src_ref, dst_ref = tabs

    def per_expert(e, carry):
        idx = tile * N_EXPERTS + e
        c, s, d = cnt_ref[idx], src_ref[idx], dst_ref[idx]

        def pieces(sizes):
            for size in sizes:
                @pl.when((c & size) != 0)
                def _():
                    off = c & (-2 * size)
                    fn(s + off, d + off, size)

        @pl.when(c >= SEG_RARE)
        def _():
            pieces([size for size in SEG_SIZES if size >= SEG_RARE])

        pieces([size for size in SEG_SIZES if size < SEG_RARE])
        return carry

    lax.fori_loop(0, N_EXPERTS, per_expert, 0)


def _to_slabs(val, ref):
    for lc in range(SLAB):
        ref[pl.ds(lc, val.shape[0], stride=SLAB), :] = val[:, lc * LANES:(lc + 1) * LANES]


def _from_slabs(ref):
    rows = ref.shape[0] // SLAB
    return jnp.concatenate([ref[pl.ds(lc, rows, stride=SLAB), :] for lc in range(SLAB)], axis=1)


def _dispatch_kernel(cnt_ref, src_ref, dst_ref, h_ref, pos_ref, xs_ref, sort_ref, sem):
    tile = pl.program_id(0)
    last = pl.num_programs(0) - 1
    tm = h_ref.shape[0]
    slot = tile % 2
    tabs = (cnt_ref, src_ref, dst_ref)

    def copy(k, s, d, size):
        return pltpu.make_async_copy(_slab_rows(sort_ref.at[k], s, size), _slab_rows(xs_ref, d, size), sem.at[k])

    j = lax.broadcasted_iota(jnp.int32, (2 * tm, tm), 0).astype(F32)
    pos = pos_ref[...]
    perm = jnp.where((j == pos[0:1, :]) | (j == pos[1:2, :]), 1.0, 0.0).astype(BF16)
    _to_slabs(_dot(perm, h_ref[...].astype(BF16)), sort_ref.at[slot])
    _for_each_segment(tabs, tile, lambda s, d, size: copy(slot, s, d, size).start())

    def wait_tile(k):
        pltpu.make_async_copy(sort_ref.at[k], xs_ref.at[pl.ds(0, 2 * tm * SLAB)], sem.at[k]).wait()

    @pl.when(tile > 0)
    def _():
        wait_tile(1 - slot)

    @pl.when(tile == last)
    def _():
        wait_tile(slot)
        sort_ref[slot, 0:MOE_ROWS * SLAB, :] = jnp.zeros((MOE_ROWS * SLAB, LANES), F32)
        for pseudo in (1, 2):
            _for_each_segment(tabs, tile + pseudo, lambda s, d, size: copy(slot, s, d, size).start())
        for pseudo in (1, 2):
            _for_each_segment(tabs, tile + pseudo, lambda s, d, size: copy(slot, s, d, size).wait())


def _dispatch(tabs, h, pos, n_slots):
    n = h.shape[0]
    tm = ROW_TILE
    grid_spec = pltpu.PrefetchScalarGridSpec(
        num_scalar_prefetch=3,
        grid=(n // tm,),
        in_specs=[pl.BlockSpec((tm, D_MODEL), lambda i, *_: (i, 0)), pl.BlockSpec((SUBLANES, tm), lambda i, *_: (0, i))],
        out_specs=pl.BlockSpec(memory_space=pl.ANY),
        scratch_shapes=[pltpu.VMEM((2, 2 * tm * SLAB, LANES), F32), pltpu.SemaphoreType.DMA((2,))],
    )
    return pl.pallas_call(
        _dispatch_kernel,
        grid_spec=grid_spec,
        out_shape=jax.ShapeDtypeStruct((n_slots * SLAB, LANES), F32),
        compiler_params=_params("arbitrary"),
    )(*tabs, h, pos)


def _expert_kernel(be_ref, nu_ref, xs_ref, wg_ref, wu_ref, wd_ref, ys_ref, wgb_ref, wub_ref, wdb_ref):
    i = pl.program_id(0)

    @pl.when((i == 0) | (be_ref[i] != be_ref[jnp.maximum(i - 1, 0)]))
    def _():
        wgb_ref[...] = wg_ref[...].astype(BF16)
        wub_ref[...] = wu_ref[...].astype(BF16)
        wdb_ref[...] = wd_ref[...].astype(BF16)

    @pl.when(i < nu_ref[0])
    def _():
        x = _from_slabs(xs_ref).astype(BF16)
        hg = _dot(x, wgb_ref[...])
        hu = _dot(x, wub_ref[...])
        hdn = hg * _sigmoid(hg) * hu
        _to_slabs(_dot(hdn.astype(BF16), wdb_ref[...]).astype(BF16).astype(F32), ys_ref)

    @pl.when(i >= nu_ref[0])
    def _():
        ys_ref[...] = jnp.zeros_like(ys_ref)


def _experts(block_expert, n_used, xs, w_gate, w_up, w_down, layer):
    blk = MOE_ROWS * SLAB
    row = lambda i, be, nu: (i, 0)
    used_row = lambda i, be, nu: (jnp.minimum(i, jnp.maximum(nu[0] - 1, 0)), 0)
    wsel = lambda i, be, nu: (layer, be[i], 0, 0)
    grid_spec = pltpu.PrefetchScalarGridSpec(
        num_scalar_prefetch=2,
        grid=(xs.shape[0] // blk,),
        in_specs=[pl.BlockSpec((blk, LANES), used_row),
                  pl.BlockSpec((None, None, D_MODEL, EXPERT_FF), wsel),
                  pl.BlockSpec((None, None, D_MODEL, EXPERT_FF), wsel),
                  pl.BlockSpec((None, None, EXPERT_FF, D_MODEL), wsel)],
        out_specs=pl.BlockSpec((blk, LANES), row),
        scratch_shapes=[pltpu.VMEM((D_MODEL, EXPERT_FF), BF16), pltpu.VMEM((D_MODEL, EXPERT_FF), BF16),
                        pltpu.VMEM((EXPERT_FF, D_MODEL), BF16)],
    )
    return pl.pallas_call(
        _expert_kernel,
        grid_spec=grid_spec,
        out_shape=jax.ShapeDtypeStruct(xs.shape, F32),
        compiler_params=_params("arbitrary"),
    )(block_expert, n_used, xs, w_gate, w_up, w_down)


def _combine_kernel(cnt_ref, src_ref, dst_ref, h_ref, route_ref, ys_ref, g_ref, b_ref, o_ref, sort_ref, sem):
    tile = pl.program_id(0)
    last = pl.num_programs(0) - 1
    tm = h_ref.shape[0]
    slot = tile % 2
    tabs = (cnt_ref, src_ref, dst_ref)

    def copy(k, s, d, size):
        return pltpu.make_async_copy(_slab_rows(ys_ref, d, size), _slab_rows(sort_ref.at[k], s, size), sem.at[k])

    @pl.when(tile == 0)
    def _():
        _for_each_segment(tabs, tile, lambda s, d, size: copy(slot, s, d, size).start())

    @pl.when(tile < last)
    def _():
        _for_each_segment(tabs, tile + 1, lambda s, d, size: copy(1 - slot, s, d, size).start())

    pltpu.make_async_copy(ys_ref.at[pl.ds(0, 2 * tm * SLAB)], sort_ref.at[slot], sem.at[slot]).wait()
    ysorted = _from_slabs(sort_ref.at[slot]).astype(BF16)
    r = route_ref[...]
    j = lax.broadcasted_iota(jnp.int32, (tm, 2 * tm), 1).astype(F32)
    y1 = _dot(jnp.where(j == r[:, 4:5], 1.0, 0.0).astype(BF16), ysorted)
    y2 = _dot(jnp.where(j == r[:, 5:6], 1.0, 0.0).astype(BF16), ysorted)
    ffn = y1 * r[:, 2:3] + y2 * r[:, 3:4]
    o_ref[...] = _layer_norm(ALPHA * h_ref[...] + ffn, g_ref[...], b_ref[...])


def _combine(tabs, h, route, ys, ln_g, ln_b):
    n = h.shape[0]
    tm = ROW_TILE
    row = lambda i, *_: (i, 0)
    const = lambda i, *_: (0, 0)
    grid_spec = pltpu.PrefetchScalarGridSpec(
        num_scalar_prefetch=3,
        grid=(n // tm,),
        in_specs=[pl.BlockSpec((tm, D_MODEL), row), pl.BlockSpec((tm, LANES), row), pl.BlockSpec(memory_space=pl.ANY),
                  pl.BlockSpec((1, D_MODEL), const), pl.BlockSpec((1, D_MODEL), const)],
        out_specs=pl.BlockSpec((tm, D_MODEL), row),
        scratch_shapes=[pltpu.VMEM((2, 2 * tm * SLAB, LANES), F32), pltpu.SemaphoreType.DMA((2,))],
    )
    return pl.pallas_call(
        _combine_kernel,
        grid_spec=grid_spec,
        out_shape=jax.ShapeDtypeStruct((n, D_MODEL), F32),
        compiler_params=_params("arbitrary"),
    )(*tabs, h, route, ys, ln_g.reshape(1, D_MODEL), ln_b.reshape(1, D_MODEL))


def _moe(h, route, pos, tab, w_gate, w_up, w_down, ln_g, ln_b, layer):
    n = h.shape[0]
    n_slots = n * 2 + N_EXPERTS * MOE_ROWS
    tabs, block_expert, n_used = _segment_tables(tab, n_slots // MOE_ROWS)
    xs = _dispatch(tabs, h, pos, n_slots)
    ys = _experts(block_expert, n_used, xs, w_gate, w_up, w_down, layer)
    return _combine(tabs, h, route, ys, ln_g, ln_b)


def _qkv_kernel(x_ref, w_ref, o1_ref, o4_ref, o16_ref, acc_ref):
    acc = _dot(x_ref[...].astype(BF16), w_ref[...])
    o1_ref[...] = acc.astype(BF16)
    nlc = acc_ref.shape[0]
    for lc in range(nlc):
        acc_ref[lc] = acc[:, lc * LANES:(lc + 1) * LANES]
    for o_ref, dil in ((o4_ref, DILATIONS[1]), (o16_ref, DILATIONS[2])):
        rows = o_ref.shape[0]
        for r in range(dil):
            for lc in range(nlc):
                col = (r * nlc + lc) * LANES
                o_ref[:, col:col + LANES] = acc_ref[lc, pl.ds(r, rows, stride=dil), :].astype(BF16)


def _qkv_proj(h, w_qkv):
    n = h.shape[0]
    tm = ROW_TILE
    wd = 3 * D_MODEL
    d4, d16 = DILATIONS[1], DILATIONS[2]
    row = lambda i: (i, 0)
    return pl.pallas_call(
        _qkv_kernel,
        grid=(n // tm,),
        in_specs=[pl.BlockSpec((tm, D_MODEL), row), pl.BlockSpec((D_MODEL, wd), lambda i: (0, 0))],
        out_specs=[pl.BlockSpec((tm, wd), row), pl.BlockSpec((tm // d4, d4 * wd), row),
                   pl.BlockSpec((tm // d16, d16 * wd), row)],
        out_shape=[jax.ShapeDtypeStruct((n, wd), BF16), jax.ShapeDtypeStruct((n // d4, d4 * wd), BF16),
                   jax.ShapeDtypeStruct((n // d16, d16 * wd), BF16)],
        scratch_shapes=[pltpu.VMEM((wd // LANES, tm, LANES), F32)],
        compiler_params=_params("arbitrary"),
    )(h, w_qkv.astype(BF16))


ATTN_QBLOCKS = 2


def _attn_kernel(q_ref, kc_ref, vc_ref, kp_ref, vp_ref, o_ref, lse_ref):
    t = ATTN_STEPS
    qi = lax.broadcasted_iota(jnp.int32, (t, 2 * t), 0)
    kj = lax.broadcasted_iota(jnp.int32, (t, 2 * t), 1)
    dist = t + qi - kj
    band = (dist >= 0) & (dist <= t)
    lane = lax.broadcasted_iota(jnp.int32, (t, LANES), 1)
    upper = lane >= ATTN_DIM
    ones_kv = jnp.ones((2 * t, LANES), BF16)
    lanes_per_head = LANES // ATTN_HEADS
    scale = jnp.asarray(ATTN_DIM ** -0.5, BF16)
    for qb in range(ATTN_QBLOCKS):
        rows = slice(qb * t, (qb + 1) * t)
        valid = band & ((kj >= t) | (pl.program_id(2) > 0)) if qb == 0 else band
        scores, vals = [], []
        for hp in range(ATTN_HEADS // 2):
            cs = slice(hp * LANES, (hp + 1) * LANES)
            q2 = q_ref[rows, cs] * scale
            if qb == 0:
                k2 = jnp.concatenate([kp_ref[:, cs], kc_ref[0:t, cs]], axis=0)
                v2 = jnp.concatenate([vp_ref[:, cs], vc_ref[0:t, cs]], axis=0)
            else:
                k2 = kc_ref[(qb - 1) * t:(qb + 1) * t, cs]
                v2 = vc_ref[(qb - 1) * t:(qb + 1) * t, cs]
            vals.append(jnp.concatenate([v2, ones_kv], axis=1))
            for sub in range(2):
                mine = upper if sub else jnp.logical_not(upper)
                qm = jnp.where(mine, q2, jnp.zeros_like(q2))
                scores.append(lax.dot_general(qm, k2, NT_DIMS, preferred_element_type=F32))
        maxes, probs = [], []
        for s in scores:
            s = jnp.where(valid, s, NEG)
            m = jnp.max(s, axis=-1, keepdims=True)
            maxes.append(m)
            probs.append(jnp.exp(s - m).astype(BF16))
        both = [_dot(p, vals[head // 2]) for head, p in enumerate(probs)]
        nums = [r[:, :LANES] for r in both]
        dens = [r[:, LANES:] for r in both]
        lse_blk = jnp.zeros((t, LANES), F32)
        for head in range(ATTN_HEADS):
            lse_blk = jnp.where(lane // lanes_per_head == head, maxes[head] + jnp.log(dens[head]), lse_blk)
        for hp in range(ATTN_HEADS // 2):
            lo, hi = 2 * hp, 2 * hp + 1
            o_ref[rows, hp * LANES:(hp + 1) * LANES] = jnp.where(upper, nums[hi] / dens[hi], nums[lo] / dens[lo])
        lse_ref[rows, :] = lse_blk


def _attn_branch(qkv_view, bsz, seq, dil):
    length = seq // dil
    t = ATTN_STEPS
    tq = t * ATTN_QBLOCKS
    qkv_v = qkv_view.reshape(bsz, length, dil * 3 * D_MODEL)
    cur = lambda part: (lambda b, r, i: (b, i, 3 * r + part))
    prev = lambda part: (lambda b, r, i: (b, jnp.maximum(i * ATTN_QBLOCKS - 1, 0), 3 * r + part))
    blk = (None, tq, D_MODEL)
    pblk = (None, t, D_MODEL)
    o, lse = pl.pallas_call(
        _attn_kernel,
        grid=(bsz, dil, length // tq),
        in_specs=[pl.BlockSpec(blk, cur(0)), pl.BlockSpec(blk, cur(1)), pl.BlockSpec(blk, cur(2)),
                  pl.BlockSpec(pblk, prev(1)), pl.BlockSpec(pblk, prev(2))],
        out_specs=[pl.BlockSpec(blk, lambda b, r, i: (b, i, r)), pl.BlockSpec((None, tq, LANES), lambda b, r, i: (b, i, r))],
        out_shape=[jax.ShapeDtypeStruct((bsz, length, dil * D_MODEL), F32),
                   jax.ShapeDtypeStruct((bsz, length, dil * LANES), F32)],
        compiler_params=_params("arbitrary", "arbitrary", "arbitrary"),
    )(qkv_v, qkv_v, qkv_v, qkv_v, qkv_v)
    return o.reshape(bsz * length, dil * D_MODEL), lse.reshape(bsz * length, dil * LANES)


def kernel(x, w_in_ab, conv_qkv, gdn_a_log, gdn_dt_bias, gdn_norm, s5_lam_re, s5_lam_im, s5_log_dt, s5_b_re, s5_b_im, s5_c_re, s5_c_im, s5_d, s5_w_glu, w_out_ab, w_qkv_c, w_out_c, ln_mix_g, ln_mix_b, router_group_w, router_group_b, router_expert_w, router_expert_b, moe_w_gate, moe_w_up, moe_w_down, ln_ffn_g, ln_ffn_b):
    bsz, seq, d = x.shape
    n = bsz * seq
    x2 = x.reshape(n, d)
    tm = ROW_TILE

    w_in = w_in_ab[0]
    nq = 4 * GDN_W
    ba_cols = jnp.zeros((d, LANES), F32).at[:, :2 * GDN_HEADS].set(w_in[:, nq:nq + 2 * GDN_HEADS])
    w_cat = jnp.concatenate([w_in[:, :nq], w_in[:, nq + 2 * GDN_HEADS:], ba_cols], axis=1).astype(BF16)
    qkv, z, u, ba = _inproj(x2, w_cat)
    ya = _gdn(qkv, z, ba, conv_qkv[0], gdn_a_log[0], gdn_dt_bias[0], gdn_norm[0], bsz, seq)
    yb = _s5(u.reshape(bsz, seq, S5_W), s5_lam_re[0], s5_lam_im[0], s5_log_dt[0], s5_b_re[0], s5_b_im[0],
             s5_c_re[0], s5_c_im[0], s5_d[0], s5_w_glu[0]).reshape(n, S5_W)
    router0 = (router_group_w[0], router_group_b[0], router_expert_w[0], router_expert_b[0])
    h, route, pos, tab = _mix_call(_mix_ab_kernel, ((ya, tm), (yb, tm)), x2, w_out_ab[0], ln_mix_g[0], ln_mix_b[0], router0)
    h = _moe(h, route, pos, tab, moe_w_gate, moe_w_up, moe_w_down, ln_ffn_g[0], ln_ffn_b[0], 0)

    acts = []
    for qkv_view, dil in zip(_qkv_proj(h, w_qkv_c[0]), DILATIONS):
        acts.append(_attn_branch(qkv_view, bsz, seq, dil))
    lanes_per_head = LANES // ATTN_HEADS
    expand = (jnp.arange(LANES)[:, None] == (jnp.arange(D_MODEL)[None, :] // ATTN_DIM) * lanes_per_head).astype(BF16)
    router1 = (router_group_w[1], router_group_b[1], router_expert_w[1], router_expert_b[1])
    d4, d16 = DILATIONS[1], DILATIONS[2]
    (o1, l1), (o4, l4), (o16, l16) = acts
    h, route, pos, tab = _mix_call(
        _mix_c_kernel, ((o1, tm), (o4, tm // d4), (o16, tm // d16), (l1, tm), (l4, tm // d4), (l16, tm // d16)),
        h, w_out_c[0], ln_mix_g[1], ln_mix_b[1], router1, extra=(expand,),
        scratch=(pltpu.VMEM((D_MODEL // LANES, tm, LANES), F32), pltpu.VMEM((D_MODEL // LANES, tm, LANES), F32),
                 pltpu.VMEM((1, tm, LANES), F32), pltpu.VMEM((1, tm, LANES), F32)))
    h = _moe(h, route, pos, tab, moe_w_gate, moe_w_up, moe_w_down, ln_ffn_g[1], ln_ffn_b[1], 1)
    return h.reshape(bsz, seq, d)
```

```python
import functools
import math

import jax
import jax.numpy as jnp
from jax import lax
from jax.experimental import pallas as pl
from jax.experimental.pallas import tpu as pltpu

F32 = jnp.float32
BF16 = jnp.bfloat16
HI = lax.Precision.HIGHEST

D_MODEL = 1024
DEPTH = 2
ALPHA = (2 * DEPTH) ** 0.25
LN_EPS = 1e-5

GDN_HEADS = 4
GDN_DIM = 128
GDN_W = GDN_HEADS * GDN_DIM
CONV_K = 4
GDN_CHUNK = 64
GDN_BATCH = 4

S5_W = 512
S5_GROUP_CH = 16
S5_GROUPS = 32
S5_STATE = 64
S5_SUPER = 4
S5_SUPER_CH = S5_W // S5_SUPER
S5_SUPER_ST = S5_GROUPS * S5_STATE // S5_SUPER

ATTN_HEADS = 16
ATTN_DIM = 64
ATTN_STEPS = 128
DILATIONS = (1, 4, 16)

MOE_GROUPS = 4
EXPERTS_PER_GROUP = 8
N_EXPERTS = 32
EXPERT_FF = 512
EXPERT_LANE0 = MOE_GROUPS

LANES = 128
SUBLANES = 8
VMEM_LIMIT = 56 * 1024 * 1024

ROW_TILE = 512
S5_STEPS = 128
MOE_ROWS = 512
GATHER_TILE = 256

NEG = -1e30
NT_DIMS = (((1,), (1,)), ((), ()))
TN_DIMS = (((0,), (0,)), ((), ()))


def _params(*sem):
    return pltpu.CompilerParams(dimension_semantics=sem, vmem_limit_bytes=VMEM_LIMIT)


def _sigmoid(x):
    return 1.0 / (1.0 + jnp.exp(-x))


def _softplus(x):
    return jnp.maximum(x, 0.0) + jnp.log(1.0 + jnp.exp(-jnp.abs(x)))


def _layer_norm(r, g, b):
    mu = jnp.mean(r, axis=-1, keepdims=True)
    c = r - mu
    var = jnp.mean(c * c, axis=-1, keepdims=True)
    return c * lax.rsqrt(var + LN_EPS) * g + b


def _dot(a, b, precision=None):
    return jnp.dot(a, b, precision=precision, preferred_element_type=F32)


def _mm(a, b, dims=None):
    a, b = a.astype(BF16), b.astype(BF16)
    if dims is None:
        return jnp.dot(a, b, preferred_element_type=F32)
    return lax.dot_general(a, b, dims, preferred_element_type=F32)


def _split_bf16(x):
    hi = x.astype(BF16)
    return hi, (x - hi.astype(F32)).astype(BF16)


IN_COLS = 3 * GDN_W + GDN_W + S5_W + LANES


def _inproj_kernel(x_ref, w_ref, qkv_ref, z_ref, u_ref, ba_ref):
    x = x_ref[...].astype(BF16)
    qkv_ref[...] = _dot(x, w_ref[:, 0:1536])
    z_ref[...] = _dot(x, w_ref[:, 1536:2048])
    u_ref[...] = _dot(x, w_ref[:, 2048:2560])
    ba_ref[...] = _dot(x, w_ref[:, 2560:2688])


def _inproj(x2, w_cat):
    n = x2.shape[0]
    tm = ROW_TILE
    row = lambda i: (i, 0)
    return pl.pallas_call(
        _inproj_kernel,
        grid=(n // tm,),
        in_specs=[pl.BlockSpec((tm, D_MODEL), row), pl.BlockSpec((D_MODEL, IN_COLS), lambda i: (0, 0))],
        out_specs=[pl.BlockSpec((tm, 3 * GDN_W), row), pl.BlockSpec((tm, GDN_W), row),
                   pl.BlockSpec((tm, S5_W), row), pl.BlockSpec((tm, LANES), row)],
        out_shape=[jax.ShapeDtypeStruct((n, 3 * GDN_W), F32), jax.ShapeDtypeStruct((n, GDN_W), F32),
                   jax.ShapeDtypeStruct((n, S5_W), F32), jax.ShapeDtypeStruct((n, LANES), F32)],
        compiler_params=_params("arbitrary"),
    )(x2, w_cat)


def _l2norm(t):
    return t * lax.rsqrt(jnp.sum(t * t, axis=-1, keepdims=True) + 1e-6)


def _short_conv(x, taps):
    out = x * taps[CONV_K - 1:CONV_K, :]
    for j in range(1, CONV_K):
        out = out + pltpu.roll(x, j, axis=0) * taps[CONV_K - 1 - j:CONV_K - j, :]
    return out


def _each(fn, *lists):
    return [fn(*args) for args in zip(*lists)]


def _unit_lower_inverse(lows, same16, cross32, cross64, eye):
    ps = [jnp.where(same16, low, 0.0) for low in lows]
    ts = [eye - p for p in ps]
    for _ in range(3):
        ps = _each(lambda p: _mm(p, p), ps)
        ts = _each(lambda t, p: _mm(t, eye + p), ts, ps)
    for cross in (cross32, cross64):
        tc = _each(lambda t, low: _mm(t, jnp.where(cross, low, 0.0)), ts, lows)
        ts = _each(lambda t, x: t - _mm(x, t), ts, tc)
    return ts


def _gdn_kernel(qkv_ref, z_ref, ba_ref, convw_ref, alog_ref, dtb_ref, normw_ref, tri_ref, o_ref,
                state_ref, tail_ref):
    c = GDN_CHUNK

    @pl.when(pl.program_id(1) == 0)
    def _():
        state_ref[...] = jnp.zeros_like(state_ref)
        tail_ref[...] = jnp.zeros_like(tail_ref)

    w = convw_ref[...]
    ri = lax.broadcasted_iota(jnp.int32, (c, c), 0)
    ci = lax.broadcasted_iota(jnp.int32, (c, c), 1)
    causal = ri >= ci
    strict = ri > ci
    eye = (ri == ci).astype(F32)
    same16 = (ri // 16) == (ci // 16)
    same32 = (ri // 32) == (ci // 32)
    cross32 = same32 & jnp.logical_not(same16)
    cross64 = jnp.logical_not(same32)
    normw = normw_ref[...]
    a_neg = -jnp.exp(alog_ref[...])
    head_rows = (lax.broadcasted_iota(jnp.int32, (SUBLANES, LANES), 1)
                 == lax.broadcasted_iota(jnp.int32, (SUBLANES, LANES), 0) + GDN_HEADS).astype(F32)

    chains, qs, ks, kbs, vbs, kws, decays, egcs, kdecs, glasts = [], [], [], [], [], [], [], [], [], []
    for bi in range(GDN_BATCH):
        x = qkv_ref[bi]
        head = _short_conv(jnp.concatenate([tail_ref[bi], x[:SUBLANES]], axis=0), w)[SUBLANES:]
        conv = jnp.concatenate([head, _short_conv(x, w)[SUBLANES:]], axis=0)
        tail_ref[bi] = x[c - SUBLANES:, :]
        qkv = conv * _sigmoid(conv)

        ba = ba_ref[bi]
        beta_all = _sigmoid(ba)
        g_all = a_neg * _softplus(ba + dtb_ref[...])
        gc_all = _dot(tri_ref[...], g_all, HI)
        egc_all = jnp.exp(gc_all)
        gc_rows = lax.dot_general(head_rows, gc_all, NT_DIMS, precision=HI, preferred_element_type=F32)

        for h in range(GDN_HEADS):
            q = _l2norm(qkv[:, h * GDN_DIM:(h + 1) * GDN_DIM]) * (GDN_DIM ** -0.5)
            k = _l2norm(qkv[:, GDN_W + h * GDN_DIM:GDN_W + (h + 1) * GDN_DIM])
            v = qkv[:, 2 * GDN_W + h * GDN_DIM:2 * GDN_W + (h + 1) * GDN_DIM]
            gl = GDN_HEADS + h
            beta = beta_all[:, h:h + 1]
            gc = gc_all[:, gl:gl + 1]
            egc = egc_all[:, gl:gl + 1]
            gc_last = gc_all[c - 1:c, gl:gl + 1]
            kb = k * beta
            chains.append((bi, h))
            qs.append(q)
            ks.append(k)
            kbs.append(kb)
            vbs.append(v * beta)
            kws.append(kb * egc)
            decays.append(jnp.exp(jnp.where(causal, gc - gc_rows[h:h + 1, :], NEG)))
            egcs.append(egc)
            kdecs.append(k * jnp.exp(gc_last - gc))
            glasts.append(jnp.exp(gc_last))

    kk = _each(lambda kb, q, k: _mm(jnp.concatenate([kb, q], axis=0), k, NT_DIMS), kbs, qs, ks)
    lows = _each(lambda a, dec: jnp.where(strict, a[:c] * dec, 0.0), kk, decays)
    qks = _each(lambda a, dec: jnp.where(causal, a[c:] * dec, 0.0), kk, decays)
    ts = _unit_lower_inverse(lows, same16, cross32, cross64, eye)
    sols = _each(lambda t, vb, kw: _mm(t, jnp.concatenate([vb, kw], axis=1)), ts, vbs, kws)
    states = [state_ref[bi, h] for bi, h in chains]
    wss = _each(lambda sol, q, egc, s: _mm(jnp.concatenate([sol[:, GDN_DIM:], q * egc], axis=0), s),
                sols, qs, egcs, states)
    v_news = _each(lambda sol, ws: sol[:, :GDN_DIM] - ws[:c], sols, wss)
    outs = _each(lambda ws, qk, v_new: ws[c:] + _mm(qk, v_new), wss, qks, v_news)
    upds = _each(lambda kd, v_new: _mm(kd, v_new, TN_DIMS), kdecs, v_news)
    for (bi, h), s, gl, upd, o in zip(chains, states, glasts, upds, outs):
        hs = slice(h * GDN_DIM, (h + 1) * GDN_DIM)
        state_ref[bi, h] = s * gl + upd
        zh = z_ref[bi, :, hs]
        o_ref[bi, :, hs] = o * lax.rsqrt(jnp.mean(o * o, axis=-1, keepdims=True) + 1e-6) * normw * (zh * _sigmoid(zh))


def _gdn(qkv, z, ba, conv_w, a_log, dt_bias, norm_w, bsz, seq):
    c = GDN_CHUNK
    nb = GDN_BATCH
    convw = jnp.zeros((SUBLANES, 3 * GDN_W), F32).at[:CONV_K].set(conv_w)
    alog = jnp.zeros((1, LANES), F32).at[0, GDN_HEADS:2 * GDN_HEADS].set(a_log)
    dtb = jnp.zeros((1, LANES), F32).at[0, GDN_HEADS:2 * GDN_HEADS].set(dt_bias)
    tri = jnp.tril(jnp.ones((c, c), F32))
    row = lambda b, i: (b, i, 0)
    const = lambda b, i: (0, 0)
    out = pl.pallas_call(
        _gdn_kernel,
        grid=(bsz // nb, seq // c),
        in_specs=[pl.BlockSpec((nb, c, 3 * GDN_W), row), pl.BlockSpec((nb, c, GDN_W), row),
                  pl.BlockSpec((nb, c, LANES), row),
                  pl.BlockSpec((SUBLANES, 3 * GDN_W), const), pl.BlockSpec((1, LANES), const),
                  pl.BlockSpec((1, LANES), const), pl.BlockSpec((1, GDN_DIM), const), pl.BlockSpec((c, c), const)],
        out_specs=pl.BlockSpec((nb, c, GDN_W), row),
        out_shape=jax.ShapeDtypeStruct((bsz, seq, GDN_W), F32),
        scratch_shapes=[pltpu.VMEM((nb, GDN_HEADS, GDN_DIM, GDN_DIM), F32),
                        pltpu.VMEM((nb, SUBLANES, 3 * GDN_W), F32)],
        compiler_params=_params("arbitrary", "arbitrary"),
    )(qkv.reshape(bsz, seq, 3 * GDN_W), z.reshape(bsz, seq, GDN_W), ba.reshape(bsz, seq, LANES),
      convw, alog, dtb, norm_w.reshape(1, GDN_DIM), tri)
    return out.reshape(bsz * seq, GDN_W)


def _gelu_tanh(x):
    return x * (0.5 * (1.0 + jnp.tanh(math.sqrt(2.0 / math.pi) * (x + 0.044715 * (x * x * x)))))


def _s5_kernel(u_ref, wb_ref, a_ref, wc_ref, d_ref, wglu_ref, y_ref, utb_ref, ytb_ref, bu_ref, h_ref, *, steps):
    @pl.when(pl.program_id(0) == 0)
    def _():
        h_ref[...] = jnp.zeros_like(h_ref)

    nb = h_ref.shape[0]
    nlc = S5_W // LANES
    for b in range(nb):
        for lc in range(nlc):
            utb_ref[lc, pl.ds(b, steps, stride=nb), :] = u_ref[b, :, lc * LANES:(lc + 1) * LANES]
    u = jnp.concatenate([utb_ref[lc] for lc in range(nlc)], axis=1)
    ub = u.astype(BF16)
    st = S5_SUPER_ST
    for j in range(S5_SUPER):
        bu_ref[:, 2 * st * j:2 * st * (j + 1)] = _dot(ub[:, S5_SUPER_CH * j:S5_SUPER_CH * (j + 1)], wb_ref[j])

    for j in range(S5_SUPER):
        re = slice(2 * st * j, 2 * st * j + st)
        im = slice(2 * st * j + st, 2 * st * (j + 1))
        ar = jnp.broadcast_to(a_ref[0:1, st * j:st * (j + 1)], (nb, st))
        ai = jnp.broadcast_to(a_ref[1:2, st * j:st * (j + 1)], (nb, st))

        def body(t, carry, re=re, im=im, ar=ar, ai=ai):
            hr, hi = carry
            rows = pl.ds(pl.multiple_of(t * nb, nb), nb)
            nr = ar * hr - ai * hi + bu_ref[rows, re]
            ni = ar * hi + ai * hr + bu_ref[rows, im]
            bu_ref[rows, re] = nr
            bu_ref[rows, im] = ni
            return nr, ni

        hr, hi = lax.fori_loop(0, steps, body, (h_ref[:, re], h_ref[:, im]), unroll=4)
        h_ref[:, re] = hr
        h_ref[:, im] = hi

    y = jnp.concatenate(
        [_dot(bu_ref[:, 2 * st * j:2 * st * (j + 1)].astype(BF16), wc_ref[j]) for j in range(S5_SUPER)], axis=1)
    y = _gelu_tanh(y + d_ref[...] * u)
    y = y * _sigmoid(_dot(y.astype(BF16), wglu_ref[...]))
    for lc in range(nlc):
        ytb_ref[lc] = y[:, lc * LANES:(lc + 1) * LANES]
    for b in range(nb):
        for lc in range(nlc):
            y_ref[b, :, lc * LANES:(lc + 1) * LANES] = ytb_ref[lc, pl.ds(b, steps, stride=nb), :]


def _s5(u3, lam_re, lam_im, log_dt, b_re, b_im, c_re, c_im, d_skip, w_glu):
    bsz, seq, _ = u3.shape
    dt = jnp.exp(log_dt)[:, None]
    mag = jnp.exp(lam_re * dt)
    a_re, a_im = mag * jnp.cos(lam_im * dt), mag * jnp.sin(lam_im * dt)
    den = lam_re * lam_re + lam_im * lam_im
    nr, ni = a_re - 1.0, a_im
    cr = (nr * lam_re + ni * lam_im) / den
    ci = (ni * lam_re - nr * lam_im) / den
    bb_re = cr[..., None] * b_re - ci[..., None] * b_im
    bb_im = cr[..., None] * b_im + ci[..., None] * b_re
    gps = S5_GROUPS // S5_SUPER
    eye = jnp.eye(gps, dtype=F32)

    def in_blockdiag(t):
        t = t.reshape(S5_SUPER, gps, S5_STATE, S5_GROUP_CH)
        return jnp.einsum('jgph,gk->jghkp', t, eye).reshape(S5_SUPER, S5_SUPER_CH, S5_SUPER_ST)

    def out_blockdiag(t):
        t = t.reshape(S5_SUPER, gps, S5_GROUP_CH, S5_STATE)
        return jnp.einsum('jgkp,gm->jgpmk', t, eye).reshape(S5_SUPER, S5_SUPER_ST, S5_SUPER_CH)

    wb = jnp.concatenate([in_blockdiag(bb_re), in_blockdiag(bb_im)], axis=2).astype(BF16)
    wc = jnp.concatenate([out_blockdiag(c_re), -out_blockdiag(c_im)], axis=1).astype(BF16)
    nstate = S5_GROUPS * S5_STATE
    a = jnp.stack([a_re.reshape(nstate), a_im.reshape(nstate)], axis=0)

    steps = S5_STEPS
    rows = steps * bsz
    blk = lambda i: (0, i, 0)
    const2 = lambda i: (0, 0)
    const3 = lambda i: (0, 0, 0)
    return pl.pallas_call(
        functools.partial(_s5_kernel, steps=steps),
        grid=(seq // steps,),
        in_specs=[pl.BlockSpec((bsz, steps, S5_W), blk), pl.BlockSpec(wb.shape, const3),
                  pl.BlockSpec((2, nstate), const2), pl.BlockSpec(wc.shape, const3),
                  pl.BlockSpec((1, S5_W), const2), pl.BlockSpec((S5_W, S5_W), const2)],
        out_specs=pl.BlockSpec((bsz, steps, S5_W), blk),
        out_shape=jax.ShapeDtypeStruct((bsz, seq, S5_W), F32),
        scratch_shapes=[pltpu.VMEM((S5_W // LANES, rows, LANES), F32), pltpu.VMEM((S5_W // LANES, rows, LANES), F32),
                        pltpu.VMEM((rows, 2 * nstate), F32), pltpu.VMEM((bsz, 2 * nstate), F32)],
        compiler_params=_params("arbitrary"),
    )(u3, wb, a, wc, d_skip.reshape(1, S5_W), w_glu.astype(BF16))


def _route(h, wr_ref, br_ref, tri_ref, run_ref, route_ref, pos_ref, tab_ref):
    @pl.when(pl.program_id(0) == 0)
    def _():
        run_ref[...] = jnp.zeros_like(run_ref)

    tm = h.shape[0]
    h_hi, h_lo = _split_bf16(h)
    both = _dot(h_hi, wr_ref[...])
    logits = both[:, :LANES] + both[:, LANES:] + _dot(h_lo, wr_ref[:, :LANES]) + br_ref[...]
    lane = lax.broadcasted_iota(jnp.int32, (tm, LANES), 1)
    lanef = lane.astype(F32)
    big = float(LANES)

    gmask = lane < MOE_GROUPS
    gmax = jnp.max(jnp.where(gmask, logits, NEG), axis=-1, keepdims=True)
    gsum = jnp.sum(jnp.where(gmask, jnp.exp(logits - gmax), 0.0), axis=-1, keepdims=True)
    p_group = 1.0 / gsum
    gidx = jnp.min(jnp.where(gmask & (logits == gmax), lanef, big), axis=-1, keepdims=True)

    lo = EXPERT_LANE0 + EXPERTS_PER_GROUP * gidx
    emask = (lanef >= lo) & (lanef < lo + EXPERTS_PER_GROUP)
    el = jnp.where(emask, logits, NEG)
    t1 = jnp.max(el, axis=-1, keepdims=True)
    i1 = jnp.min(jnp.where(emask & (el == t1), lanef, big), axis=-1, keepdims=True)
    emask2 = emask & (lanef != i1)
    el2 = jnp.where(emask2, logits, NEG)
    t2 = jnp.max(el2, axis=-1, keepdims=True)
    i2 = jnp.min(jnp.where(emask2 & (el2 == t2), lanef, big), axis=-1, keepdims=True)
    e2 = jnp.exp(t2 - t1)
    gate1 = p_group / (1.0 + e2)
    gate2 = p_group * e2 / (1.0 + e2)

    oh1 = lanef == i1
    oh2 = lanef == i2
    oh = jnp.where(oh1 | oh2, 1.0, 0.0)
    cnt = jnp.sum(oh, axis=0, keepdims=True)
    ei = lax.broadcasted_iota(jnp.int32, (LANES, LANES), 0)
    ej = lax.broadcasted_iota(jnp.int32, (LANES, LANES), 1)
    start = _dot(jnp.broadcast_to(cnt, (SUBLANES, LANES)), (ei < ej).astype(F32), HI)[0:1]
    where = _dot(tri_ref[...], oh.astype(BF16)) + start
    pos1 = jnp.sum(jnp.where(oh1, where, 0.0), axis=-1, keepdims=True)
    pos2 = jnp.sum(jnp.where(oh2, where, 0.0), axis=-1, keepdims=True)

    run = run_ref[...]
    srow = lax.broadcasted_iota(jnp.int32, (SUBLANES, LANES), 0)
    slane = lax.broadcasted_iota(jnp.int32, (SUBLANES, LANES), 1)
    tab_ref[...] = jnp.where(srow == 0, cnt, jnp.where(srow == 1, run, jnp.where(srow == 2, start, 0.0)))
    run_ref[...] = run + cnt

    route = jnp.where(lane == 2, gate1, 0.0)
    route = jnp.where(lane == 3, gate2, route)
    route = jnp.where(lane == 4, pos1, route)
    route = jnp.where(lane == 5, pos2, route)
    route_ref[...] = route
    pick = (slane == srow + 4).astype(F32)
    pos_ref[...] = lax.dot_general(pick, route, NT_DIMS, precision=HI, preferred_element_type=F32)


def _mix_ab_kernel(ya_ref, yb_ref, x_ref, w_ref, g_ref, b_ref, wr_ref, br_ref, tri_ref,
                   h_ref, route_ref, pos_ref, tab_ref, run_ref):
    mix = _dot(ya_ref[...].astype(BF16), w_ref[0:GDN_W, :]) + _dot(yb_ref[...].astype(BF16), w_ref[GDN_W:, :])
    h = _layer_norm(ALPHA * x_ref[...] + mix, g_ref[...], b_ref[...])
    h_ref[...] = h
    _route(h, wr_ref, br_ref, tri_ref, run_ref, route_ref, pos_ref, tab_ref)


def _to_natural(view_ref, nat_ref, dil):
    rows = view_ref.shape[0]
    nlc = nat_ref.shape[0]
    for r in range(dil):
        for lc in range(nlc):
            col = (r * nlc + lc) * LANES
            nat_ref[lc, pl.ds(r, rows, stride=dil), :] = view_ref[:, col:col + LANES]
    return jnp.concatenate([nat_ref[lc] for lc in range(nlc)], axis=1)


def _mix_c_kernel(o1_ref, o4_ref, o16_ref, l1_ref, l4_ref, l16_ref, expand_ref, x_ref, w_ref, g_ref, b_ref,
                  wr_ref, br_ref, tri_ref, h_ref, route_ref, pos_ref, tab_ref, run_ref,
                  on4_ref, on16_ref, ln4_ref, ln16_ref):
    o4 = _to_natural(o4_ref, on4_ref, DILATIONS[1])
    o16 = _to_natural(o16_ref, on16_ref, DILATIONS[2])
    l2 = _to_natural(l4_ref, ln4_ref, DILATIONS[1])
    l3 = _to_natural(l16_ref, ln16_ref, DILATIONS[2])
    l1 = l1_ref[...]
    m = jnp.maximum(jnp.maximum(l1, l2), l3)
    e1, e2, e3 = jnp.exp(l1 - m), jnp.exp(l2 - m), jnp.exp(l3 - m)
    inv = 1.0 / (e1 + e2 + e3)
    ex = expand_ref[...]

    def spread(wt):
        hi, lo = _split_bf16(wt)
        return _dot(hi, ex) + _dot(lo, ex)

    o = spread(e1 * inv) * o1_ref[...] + spread(e2 * inv) * o4 + spread(e3 * inv) * o16
    mix = _dot(o.astype(BF16), w_ref[...])
    h = _layer_norm(ALPHA * x_ref[...] + mix, g_ref[...], b_ref[...])
    h_ref[...] = h
    _route(h, wr_ref, br_ref, tri_ref, run_ref, route_ref, pos_ref, tab_ref)


def _router_operands(wg, bg, we, be):
    wr = jnp.zeros((D_MODEL, LANES), F32).at[:, :MOE_GROUPS].set(wg).at[:, EXPERT_LANE0:EXPERT_LANE0 + N_EXPERTS].set(we)
    wr_hi = wr.astype(BF16)
    wr_lo = (wr - wr_hi.astype(F32)).astype(BF16)
    br = jnp.zeros((1, LANES), F32).at[0, :MOE_GROUPS].set(bg).at[0, EXPERT_LANE0:EXPERT_LANE0 + N_EXPERTS].set(be)
    tri = jnp.tril(jnp.ones((ROW_TILE, ROW_TILE), F32), -1).astype(BF16)
    return jnp.concatenate([wr_hi, wr_lo], axis=1), br, tri


def _mix_call(kernel_fn, acts, x2, w_out, ln_g, ln_b, router, extra=(), scratch=()):
    n = x2.shape[0]
    tm = ROW_TILE
    wr, br, tri = _router_operands(*router)
    row = lambda i: (i, 0)
    const = lambda i: (0, 0)
    in_specs = [pl.BlockSpec((rows, a.shape[1]), row) for a, rows in acts]
    in_specs += [pl.BlockSpec(e.shape, const) for e in extra]
    in_specs += [pl.BlockSpec((tm, D_MODEL), row), pl.BlockSpec(w_out.shape, const),
                 pl.BlockSpec((1, D_MODEL), const), pl.BlockSpec((1, D_MODEL), const),
                 pl.BlockSpec((D_MODEL, 2 * LANES), const), pl.BlockSpec((1, LANES), const),
                 pl.BlockSpec((tm, tm), const)]
    return pl.pallas_call(
        kernel_fn,
        grid=(n // tm,),
        in_specs=in_specs,
        out_specs=[pl.BlockSpec((tm, D_MODEL), row), pl.BlockSpec((tm, LANES), row),
                   pl.BlockSpec((SUBLANES, tm), lambda i: (0, i)),
                   pl.BlockSpec((None, SUBLANES, LANES), lambda i: (i, 0, 0))],
        out_shape=[jax.ShapeDtypeStruct((n, D_MODEL), F32), jax.ShapeDtypeStruct((n, LANES), F32),
                   jax.ShapeDtypeStruct((SUBLANES, n), F32), jax.ShapeDtypeStruct((n // tm, SUBLANES, LANES), F32)],
        scratch_shapes=[pltpu.VMEM((1, LANES), F32), *scratch],
        compiler_params=_params("arbitrary"),
    )(*[a for a, _ in acts], *extra, x2, w_out.astype(BF16), ln_g.reshape(1, D_MODEL), ln_b.reshape(1, D_MODEL),
      wr, br, tri)


SLAB = D_MODEL // LANES // 2
U32 = jnp.uint32
HIGH_HALF = 0xFFFF0000
SEG_SIZES = tuple(1 << k for k in range(ROW_TILE.bit_length() - 1, -1, -1))
SEG_RARE = 64


def _segment_tables(tab, n_blocks):
    lanes = slice(EXPERT_LANE0, EXPERT_LANE0 + N_EXPERTS)
    cnt = tab[:, 0, lanes].astype(jnp.int32)
    run = tab[:, 1, lanes].astype(jnp.int32)
    src = tab[:, 2, lanes].astype(jnp.int32)
    counts = run[-1] + cnt[-1]
    padded = (counts + MOE_ROWS - 1) // MOE_ROWS * MOE_ROWS
    padded_end = jnp.cumsum(padded)
    dst = (padded_end - padded)[None, :] + run
    tail = jnp.arange(N_EXPERTS, dtype=jnp.int32) * MOE_ROWS + padded_end[-1]
    tail_cnt = jnp.where(tail < n_blocks * MOE_ROWS, MOE_ROWS, 0)
    cnt = jnp.concatenate([cnt, (padded - counts)[None, :], tail_cnt[None, :]], axis=0)
    src = jnp.concatenate([src, jnp.zeros((2, N_EXPERTS), jnp.int32)], axis=0)
    dst = jnp.concatenate([dst, (padded_end - padded + counts)[None, :], tail[None, :]], axis=0)
    block_start = jnp.arange(n_blocks, dtype=jnp.int32) * MOE_ROWS
    block_expert = jnp.minimum(jnp.sum((padded_end[None, :] <= block_start[:, None]).astype(jnp.int32), axis=1),
                               N_EXPERTS - 1)
    n_used = (padded_end[-1:] // MOE_ROWS).astype(jnp.int32)
    return (cnt.reshape(-1), src.reshape(-1), dst.reshape(-1)), block_expert, n_used


def _slab_rows(ref, start, size):
    return ref.at[pl.ds(pl.multiple_of(start * SLAB, SLAB), size * SLAB)]


def _for_each_segment(tabs, tile, fn):
    cnt_ref, src_ref, dst_ref = tabs

    def per_expert(e, carry):
        idx = tile * N_EXPERTS + e
        c, s, d = cnt_ref[idx], src_ref[idx], dst_ref[idx]

        def pieces(sizes):
            for size in sizes:
                @pl.when((c & size) != 0)
                def _():
                    off = c & (-2 * size)
                    fn(s + off, d + off, size)

        @pl.when(c >= SEG_RARE)
        def _():
            pieces([size for size in SEG_SIZES if size >= SEG_RARE])

        pieces([size for size in SEG_SIZES if size < SEG_RARE])
        return carry

    lax.fori_loop(0, N_EXPERTS, per_expert, 0)


def _to_slabs(val, ref):
    for lc in range(SLAB):
        hi = pltpu.bitcast(val[:, lc * LANES:(lc + 1) * LANES], U32)
        lo = pltpu.bitcast(val[:, (lc + SLAB) * LANES:(lc + SLAB + 1) * LANES], U32)
        ref[pl.ds(lc, val.shape[0], stride=SLAB), :] = (hi & jnp.uint32(HIGH_HALF)) | (lo >> 16)


def _from_slabs(ref):
    rows = ref.shape[0] // SLAB
    words = [ref[pl.ds(lc, rows, stride=SLAB), :] for lc in range(SLAB)]
    return jnp.concatenate([pltpu.bitcast(w & jnp.uint32(HIGH_HALF), F32) for w in words]
                           + [pltpu.bitcast(w << 16, F32) for w in words], axis=1)


def _dispatch_kernel(cnt_ref, src_ref, dst_ref, h_ref, pos_ref, xs_ref, sort_ref, sem):
    tile = pl.program_id(0)
    last = pl.num_programs(0) - 1
    tm = h_ref.shape[0]
    slot = tile % 2
    tabs = (cnt_ref, src_ref, dst_ref)

    def copy(k, s, d, size):
        return pltpu.make_async_copy(_slab_rows(sort_ref.at[k], s, size), _slab_rows(xs_ref, d, size), sem.at[k])

    j = lax.broadcasted_iota(jnp.int32, (2 * tm, tm), 0).astype(F32)
    pos = pos_ref[...]
    perm = jnp.where((j == pos[0:1, :]) | (j == pos[1:2, :]), 1.0, 0.0).astype(BF16)
    _to_slabs(_dot(perm, h_ref[...].astype(BF16)), sort_ref.at[slot])
    _for_each_segment(tabs, tile, lambda s, d, size: copy(slot, s, d, size).start())

    def wait_tile(k):
        pltpu.make_async_copy(sort_ref.at[k], xs_ref.at[pl.ds(0, 2 * tm * SLAB)], sem.at[k]).wait()

    @pl.when(tile > 0)
    def _():
        wait_tile(1 - slot)

    @pl.when(tile == last)
    def _():
        wait_tile(slot)
        sort_ref[slot, 0:MOE_ROWS * SLAB, :] = jnp.zeros((MOE_ROWS * SLAB, LANES), U32)
        for pseudo in (1, 2):
            _for_each_segment(tabs, tile + pseudo, lambda s, d, size: copy(slot, s, d, size).start())
        for pseudo in (1, 2):
            _for_each_segment(tabs, tile + pseudo, lambda s, d, size: copy(slot, s, d, size).wait())


def _dispatch(tabs, h, pos, n_slots):
    n = h.shape[0]
    tm = ROW_TILE
    grid_spec = pltpu.PrefetchScalarGridSpec(
        num_scalar_prefetch=3,
        grid=(n // tm,),
        in_specs=[pl.BlockSpec((tm, D_MODEL), lambda i, *_: (i, 0)), pl.BlockSpec((SUBLANES, tm), lambda i, *_: (0, i))],
        out_specs=pl.BlockSpec(memory_space=pl.ANY),
        scratch_shapes=[pltpu.VMEM((2, 2 * tm * SLAB, LANES), U32), pltpu.SemaphoreType.DMA((2,))],
    )
    return pl.pallas_call(
        _dispatch_kernel,
        grid_spec=grid_spec,
        out_shape=jax.ShapeDtypeStruct((n_slots * SLAB, LANES), U32),
        compiler_params=_params("arbitrary"),
    )(*tabs, h, pos)


def _expert_kernel(be_ref, nu_ref, xs_ref, wg_ref, wu_ref, wd_ref, ys_ref, wgb_ref, wub_ref, wdb_ref):
    i = pl.program_id(0)

    @pl.when((i == 0) | (be_ref[i] != be_ref[jnp.maximum(i - 1, 0)]))
    def _():
        wgb_ref[...] = wg_ref[...].astype(BF16)
        wub_ref[...] = wu_ref[...].astype(BF16)
        wdb_ref[...] = wd_ref[...].astype(BF16)

    @pl.when(i < nu_ref[0])
    def _():
        x = _from_slabs(xs_ref).astype(BF16)
        hg = _dot(x, wgb_ref[...])
        hu = _dot(x, wub_ref[...])
        hdn = hg * _sigmoid(hg) * hu
        _to_slabs(_dot(hdn.astype(BF16), wdb_ref[...]).astype(BF16).astype(F32), ys_ref)

    @pl.when(i >= nu_ref[0])
    def _():
        ys_ref[...] = jnp.zeros_like(ys_ref)


def _experts(block_expert, n_used, xs, w_gate, w_up, w_down, layer):
    blk = MOE_ROWS * SLAB
    row = lambda i, be, nu: (i, 0)
    used_row = lambda i, be, nu: (jnp.minimum(i, jnp.maximum(nu[0] - 1, 0)), 0)
    wsel = lambda i, be, nu: (layer, be[i], 0, 0)
    grid_spec = pltpu.PrefetchScalarGridSpec(
        num_scalar_prefetch=2,
        grid=(xs.shape[0] // blk,),
        in_specs=[pl.BlockSpec((blk, LANES), used_row),
                  pl.BlockSpec((None, None, D_MODEL, EXPERT_FF), wsel),
                  pl.BlockSpec((None, None, D_MODEL, EXPERT_FF), wsel),
                  pl.BlockSpec((None, None, EXPERT_FF, D_MODEL), wsel)],
        out_specs=pl.BlockSpec((blk, LANES), row),
        scratch_shapes=[pltpu.VMEM((D_MODEL, EXPERT_FF), BF16), pltpu.VMEM((D_MODEL, EXPERT_FF), BF16),
                        pltpu.VMEM((EXPERT_FF, D_MODEL), BF16)],
    )
    return pl.pallas_call(
        _expert_kernel,
        grid_spec=grid_spec,
        out_shape=jax.ShapeDtypeStruct(xs.shape, U32),
        compiler_params=_params("arbitrary"),
    )(block_expert, n_used, xs, w_gate, w_up, w_down)


def _combine_kernel(cnt_ref, src_ref, dst_ref, h_ref, route_ref, ys_ref, g_ref, b_ref, o_ref, sort_ref, sem):
    tile = pl.program_id(0)
    last = pl.num_programs(0) - 1
    tm = h_ref.shape[0]
    slot = tile % 2
    tabs = (cnt_ref, src_ref, dst_ref)

    def copy(k, s, d, size):
        return pltpu.make_async_copy(_slab_rows(ys_ref, d, size), _slab_rows(sort_ref.at[k], s, size), sem.at[k])

    @pl.when(tile == 0)
    def _():
        _for_each_segment(tabs, tile, lambda s, d, size: copy(slot, s, d, size).start())

    @pl.when(tile < last)
    def _():
        _for_each_segment(tabs, tile + 1, lambda s, d, size: copy(1 - slot, s, d, size).start())

    pltpu.make_async_copy(ys_ref.at[pl.ds(0, 2 * tm * SLAB)], sort_ref.at[slot], sem.at[slot]).wait()
    ysorted = _from_slabs(sort_ref.at[slot]).astype(BF16)
    r = route_ref[...]
    j = lax.broadcasted_iota(jnp.int32, (2 * tm, 2 * tm), 1).astype(F32)
    where = jnp.concatenate([r[:, 4:5], r[:, 5:6]], axis=0)
    picked = _dot(jnp.where(j == where, 1.0, 0.0).astype(BF16), ysorted)
    ffn = picked[:tm] * r[:, 2:3] + picked[tm:] * r[:, 3:4]
    o_ref[...] = _layer_norm(ALPHA * h_ref[...] + ffn, g_ref[...], b_ref[...])


def _combine(tabs, h, route, ys, ln_g, ln_b):
    n = h.shape[0]
    tm = ROW_TILE
    row = lambda i, *_: (i, 0)
    const = lambda i, *_: (0, 0)
    grid_spec = pltpu.PrefetchScalarGridSpec(
        num_scalar_prefetch=3,
        grid=(n // tm,),
        in_specs=[pl.BlockSpec((tm, D_MODEL), row), pl.BlockSpec((tm, LANES), row), pl.BlockSpec(memory_space=pl.ANY),
                  pl.BlockSpec((1, D_MODEL), const), pl.BlockSpec((1, D_MODEL), const)],
        out_specs=pl.BlockSpec((tm, D_MODEL), row),
        scratch_shapes=[pltpu.VMEM((2, 2 * tm * SLAB, LANES), U32), pltpu.SemaphoreType.DMA((2,))],
    )
    return pl.pallas_call(
        _combine_kernel,
        grid_spec=grid_spec,
        out_shape=jax.ShapeDtypeStruct((n, D_MODEL), F32),
        compiler_params=_params("arbitrary"),
    )(*tabs, h, route, ys, ln_g.reshape(1, D_MODEL), ln_b.reshape(1, D_MODEL))


def _moe(h, route, pos, tab, w_gate, w_up, w_down, ln_g, ln_b, layer):
    n = h.shape[0]
    n_slots = n * 2 + N_EXPERTS * MOE_ROWS
    tabs, block_expert, n_used = _segment_tables(tab, n_slots // MOE_ROWS)
    xs = _dispatch(tabs, h, pos, n_slots)
    ys = _experts(block_expert, n_used, xs, w_gate, w_up, w_down, layer)
    return _combine(tabs, h, route, ys, ln_g, ln_b)


def _qkv_kernel(x_ref, w_ref, o1_ref, o4_ref, o16_ref, acc_ref):
    acc = _dot(x_ref[...].astype(BF16), w_ref[...])
    o1_ref[...] = acc.astype(BF16)
    nlc = acc_ref.shape[0]
    for lc in range(nlc):
        acc_ref[lc] = acc[:, lc * LANES:(lc + 1) * LANES]
    for o_ref, dil in ((o4_ref, DILATIONS[1]), (o16_ref, DILATIONS[2])):
        rows = o_ref.shape[0]
        for r in range(dil):
            for lc in range(nlc):
                col = (r * nlc + lc) * LANES
                o_ref[:, col:col + LANES] = acc_ref[lc, pl.ds(r, rows, stride=dil), :].astype(BF16)


def _qkv_proj(h, w_qkv):
    n = h.shape[0]
    tm = ROW_TILE
    wd = 3 * D_MODEL
    d4, d16 = DILATIONS[1], DILATIONS[2]
    row = lambda i: (i, 0)
    return pl.pallas_call(
        _qkv_kernel,
        grid=(n // tm,),
        in_specs=[pl.BlockSpec((tm, D_MODEL), row), pl.BlockSpec((D_MODEL, wd), lambda i: (0, 0))],
        out_specs=[pl.BlockSpec((tm, wd), row), pl.BlockSpec((tm // d4, d4 * wd), row),
                   pl.BlockSpec((tm // d16, d16 * wd), row)],
        out_shape=[jax.ShapeDtypeStruct((n, wd), BF16), jax.ShapeDtypeStruct((n // d4, d4 * wd), BF16),
                   jax.ShapeDtypeStruct((n // d16, d16 * wd), BF16)],
        scratch_shapes=[pltpu.VMEM((wd // LANES, tm, LANES), F32)],
        compiler_params=_params("arbitrary"),
    )(h, w_qkv.astype(BF16))


ATTN_QBLOCKS = 2


def _attn_kernel(q_ref, kc_ref, vc_ref, kp_ref, vp_ref, o_ref, lse_ref):
    t = ATTN_STEPS
    qi = lax.broadcasted_iota(jnp.int32, (t, 2 * t), 0)
    kj = lax.broadcasted_iota(jnp.int32, (t, 2 * t), 1)
    dist = t + qi - kj
    band = (dist >= 0) & (dist <= t)
    lane = lax.broadcasted_iota(jnp.int32, (t, LANES), 1)
    upper = lane >= ATTN_DIM
    ones_kv = jnp.ones((2 * t, LANES), BF16)
    lanes_per_head = LANES // ATTN_HEADS
    scale = jnp.asarray(ATTN_DIM ** -0.5, BF16)
    for qb in range(ATTN_QBLOCKS):
        rows = slice(qb * t, (qb + 1) * t)
        valid = band & ((kj >= t) | (pl.program_id(2) > 0)) if qb == 0 else band
        scores, vals = [], []
        for hp in range(ATTN_HEADS // 2):
            cs = slice(hp * LANES, (hp + 1) * LANES)
            q2 = q_ref[rows, cs] * scale
            if qb == 0:
                k2 = jnp.concatenate([kp_ref[:, cs], kc_ref[0:t, cs]], axis=0)
                v2 = jnp.concatenate([vp_ref[:, cs], vc_ref[0:t, cs]], axis=0)
            else:
                k2 = kc_ref[(qb - 1) * t:(qb + 1) * t, cs]
                v2 = vc_ref[(qb - 1) * t:(qb + 1) * t, cs]
            vals.append(jnp.concatenate([v2, ones_kv], axis=1))
            for sub in range(2):
                mine = upper if sub else jnp.logical_not(upper)
                qm = jnp.where(mine, q2, jnp.zeros_like(q2))
                scores.append(lax.dot_general(qm, k2, NT_DIMS, preferred_element_type=F32))
        maxes, probs = [], []
        for s in scores:
            s = jnp.where(valid, s, NEG)
            m = jnp.max(s, axis=-1, keepdims=True)
            maxes.append(m)
            probs.append(jnp.exp(s - m).astype(BF16))
        both = [_dot(p, vals[head // 2]) for head, p in enumerate(probs)]
        nums = [r[:, :LANES] for r in both]
        dens = [r[:, LANES:] for r in both]
        lse_blk = jnp.zeros((t, LANES), F32)
        for head in range(ATTN_HEADS):
            lse_blk = jnp.where(lane // lanes_per_head == head, maxes[head] + jnp.log(dens[head]), lse_blk)
        for hp in range(ATTN_HEADS // 2):
            lo, hi = 2 * hp, 2 * hp + 1
            o_ref[rows, hp * LANES:(hp + 1) * LANES] = jnp.where(upper, nums[hi] / dens[hi], nums[lo] / dens[lo])
        lse_ref[rows, :] = lse_blk


def _attn_branch(qkv_view, bsz, seq, dil):
    length = seq // dil
    t = ATTN_STEPS
    tq = t * ATTN_QBLOCKS
    qkv_v = qkv_view.reshape(bsz, length, dil * 3 * D_MODEL)
    cur = lambda part: (lambda b, r, i: (b, i, 3 * r + part))
    prev = lambda part: (lambda b, r, i: (b, jnp.maximum(i * ATTN_QBLOCKS - 1, 0), 3 * r + part))
    blk = (None, tq, D_MODEL)
    pblk = (None, t, D_MODEL)
    o, lse = pl.pallas_call(
        _attn_kernel,
        grid=(bsz, dil, length // tq),
        in_specs=[pl.BlockSpec(blk, cur(0)), pl.BlockSpec(blk, cur(1)), pl.BlockSpec(blk, cur(2)),
                  pl.BlockSpec(pblk, prev(1)), pl.BlockSpec(pblk, prev(2))],
        out_specs=[pl.BlockSpec(blk, lambda b, r, i: (b, i, r)), pl.BlockSpec((None, tq, LANES), lambda b, r, i: (b, i, r))],
        out_shape=[jax.ShapeDtypeStruct((bsz, length, dil * D_MODEL), F32),
                   jax.ShapeDtypeStruct((bsz, length, dil * LANES), F32)],
        compiler_params=_params("arbitrary", "arbitrary", "arbitrary"),
    )(qkv_v, qkv_v, qkv_v, qkv_v, qkv_v)
    return o.reshape(bsz * length, dil * D_MODEL), lse.reshape(bsz * length, dil * LANES)


def kernel(x, w_in_ab, conv_qkv, gdn_a_log, gdn_dt_bias, gdn_norm, s5_lam_re, s5_lam_im, s5_log_dt, s5_b_re, s5_b_im, s5_c_re, s5_c_im, s5_d, s5_w_glu, w_out_ab, w_qkv_c, w_out_c, ln_mix_g, ln_mix_b, router_group_w, router_group_b, router_expert_w, router_expert_b, moe_w_gate, moe_w_up, moe_w_down, ln_ffn_g, ln_ffn_b):
    bsz, seq, d = x.shape
    n = bsz * seq
    x2 = x.reshape(n, d)
    tm = ROW_TILE

    w_in = w_in_ab[0]
    nq = 4 * GDN_W
    ba_cols = jnp.zeros((d, LANES), F32).at[:, :2 * GDN_HEADS].set(w_in[:, nq:nq + 2 * GDN_HEADS])
    w_cat = jnp.concatenate([w_in[:, :nq], w_in[:, nq + 2 * GDN_HEADS:], ba_cols], axis=1).astype(BF16)
    qkv, z, u, ba = _inproj(x2, w_cat)
    ya = _gdn(qkv, z, ba, conv_qkv[0], gdn_a_log[0], gdn_dt_bias[0], gdn_norm[0], bsz, seq)
    yb = _s5(u.reshape(bsz, seq, S5_W), s5_lam_re[0], s5_lam_im[0], s5_log_dt[0], s5_b_re[0], s5_b_im[0],
             s5_c_re[0], s5_c_im[0], s5_d[0], s5_w_glu[0]).reshape(n, S5_W)
    router0 = (router_group_w[0], router_group_b[0], router_expert_w[0], router_expert_b[0])
    h, route, pos, tab = _mix_call(_mix_ab_kernel, ((ya, tm), (yb, tm)), x2, w_out_ab[0], ln_mix_g[0], ln_mix_b[0], router0)
    h = _moe(h, route, pos, tab, moe_w_gate, moe_w_up, moe_w_down, ln_ffn_g[0], ln_ffn_b[0], 0)

    acts = []
    for qkv_view, dil in zip(_qkv_proj(h, w_qkv_c[0]), DILATIONS):
        acts.append(_attn_branch(qkv_view, bsz, seq, dil))
    lanes_per_head = LANES // ATTN_HEADS
    expand = (jnp.arange(LANES)[:, None] == (jnp.arange(D_MODEL)[None, :] // ATTN_DIM) * lanes_per_head).astype(BF16)
    router1 = (router_group_w[1], router_group_b[1], router_expert_w[1], router_expert_b[1])
    d4, d16 = DILATIONS[1], DILATIONS[2]
    (o1, l1), (o4, l4), (o16, l16) = acts
    h, route, pos, tab = _mix_call(
        _mix_c_kernel, ((o1, tm), (o4, tm // d4), (o16, tm // d16), (l1, tm), (l4, tm // d4), (l16, tm // d16)),
        h, w_out_c[0], ln_mix_g[1], ln_mix_b[1], router1, extra=(expand,),
        scratch=(pltpu.VMEM((D_MODEL // LANES, tm, LANES), F32), pltpu.VMEM((D_MODEL // LANES, tm, LANES), F32),
                 pltpu.VMEM((1, tm, LANES), F32), pltpu.VMEM((1, tm, LANES), F32)))
    h = _moe(h, route, pos, tab, moe_w_gate, moe_w_up, moe_w_down, ln_ffn_g[1], ln_ffn_b[1], 1)
    return h.reshape(bsz, seq, d)
```

```python
import functools
import math

import jax
import jax.numpy as jnp
from jax import lax
from jax.experimental import pallas as pl
from jax.experimental.pallas import tpu as pltpu

F32 = jnp.float32
BF16 = jnp.bfloat16
HI = lax.Precision.HIGHEST

D_MODEL = 1024
DEPTH = 2
ALPHA = (2 * DEPTH) ** 0.25
LN_EPS = 1e-5

GDN_HEADS = 4
GDN_DIM = 128
GDN_W = GDN_HEADS * GDN_DIM
CONV_K = 4
GDN_CHUNK = 64
GDN_BATCH = 4

S5_W = 512
S5_GROUP_CH = 16
S5_GROUPS = 32
S5_STATE = 64
S5_SUPER = 4
S5_SUPER_CH = S5_W // S5_SUPER
S5_SUPER_ST = S5_GROUPS * S5_STATE // S5_SUPER

ATTN_HEADS = 16
ATTN_DIM = 64
ATTN_STEPS = 128
DILATIONS = (1, 4, 16)

MOE_GROUPS = 4
EXPERTS_PER_GROUP = 8
N_EXPERTS = 32
EXPERT_FF = 512
EXPERT_LANE0 = MOE_GROUPS

LANES = 128
SUBLANES = 8
VMEM_LIMIT = 56 * 1024 * 1024

ROW_TILE = 512
S5_STEPS = 128
MOE_ROWS = 512
GATHER_TILE = 256

NEG = -1e30
NT_DIMS = (((1,), (1,)), ((), ()))
TN_DIMS = (((0,), (0,)), ((), ()))


def _params(*sem):
    return pltpu.CompilerParams(dimension_semantics=sem, vmem_limit_bytes=VMEM_LIMIT)


def _sigmoid(x):
    return 1.0 / (1.0 + jnp.exp(-x))


def _softplus(x):
    return jnp.maximum(x, 0.0) + jnp.log(1.0 + jnp.exp(-jnp.abs(x)))


def _layer_norm(r, g, b):
    mu = jnp.mean(r, axis=-1, keepdims=True)
    c = r - mu
    var = jnp.mean(c * c, axis=-1, keepdims=True)
    return c * lax.rsqrt(var + LN_EPS) * g + b


def _dot(a, b, precision=None):
    return jnp.dot(a, b, precision=precision, preferred_element_type=F32)


def _mm(a, b, dims=None):
    a, b = a.astype(BF16), b.astype(BF16)
    if dims is None:
        return jnp.dot(a, b, preferred_element_type=F32)
    return lax.dot_general(a, b, dims, preferred_element_type=F32)


def _split_bf16(x):
    hi = x.astype(BF16)
    return hi, (x - hi.astype(F32)).astype(BF16)


IN_COLS = 3 * GDN_W + GDN_W + S5_W + LANES


def _inproj_kernel(x_ref, w_ref, qkv_ref, z_ref, u_ref, ba_ref):
    x = x_ref[...].astype(BF16)
    qkv_ref[...] = _dot(x, w_ref[:, 0:1536])
    z_ref[...] = _dot(x, w_ref[:, 1536:2048])
    u_ref[...] = _dot(x, w_ref[:, 2048:2560])
    ba_ref[...] = _dot(x, w_ref[:, 2560:2688])


def _inproj(x2, w_cat):
    n = x2.shape[0]
    tm = ROW_TILE
    row = lambda i: (i, 0)
    return pl.pallas_call(
        _inproj_kernel,
        grid=(n // tm,),
        in_specs=[pl.BlockSpec((tm, D_MODEL), row), pl.BlockSpec((D_MODEL, IN_COLS), lambda i: (0, 0))],
        out_specs=[pl.BlockSpec((tm, 3 * GDN_W), row), pl.BlockSpec((tm, GDN_W), row),
                   pl.BlockSpec((tm, S5_W), row), pl.BlockSpec((tm, LANES), row)],
        out_shape=[jax.ShapeDtypeStruct((n, 3 * GDN_W), F32), jax.ShapeDtypeStruct((n, GDN_W), F32),
                   jax.ShapeDtypeStruct((n, S5_W), F32), jax.ShapeDtypeStruct((n, LANES), F32)],
        compiler_params=_params("arbitrary"),
    )(x2, w_cat)


def _l2norm(t):
    return t * lax.rsqrt(jnp.sum(t * t, axis=-1, keepdims=True) + 1e-6)


def _short_conv(x, taps):
    out = x * taps[CONV_K - 1:CONV_K, :]
    for j in range(1, CONV_K):
        out = out + pltpu.roll(x, j, axis=0) * taps[CONV_K - 1 - j:CONV_K - j, :]
    return out


def _each(fn, *lists):
    return [fn(*args) for args in zip(*lists)]


def _unit_lower_inverse(lows, same16, cross32, cross64, eye):
    ps = [jnp.where(same16, low, 0.0) for low in lows]
    ts = [eye - p for p in ps]
    for _ in range(3):
        ps = _each(lambda p: _mm(p, p), ps)
        ts = _each(lambda t, p: _mm(t, eye + p), ts, ps)
    for cross in (cross32, cross64):
        tc = _each(lambda t, low: _mm(t, jnp.where(cross, low, 0.0)), ts, lows)
        ts = _each(lambda t, x: t - _mm(x, t), ts, tc)
    return ts


def _gdn_kernel(qkv_ref, z_ref, ba_ref, convw_ref, alog_ref, dtb_ref, normw_ref, tri_ref, o_ref,
                state_ref, tail_ref):
    c = GDN_CHUNK

    @pl.when(pl.program_id(1) == 0)
    def _():
        state_ref[...] = jnp.zeros_like(state_ref)
        tail_ref[...] = jnp.zeros_like(tail_ref)

    w = convw_ref[...]
    ri = lax.broadcasted_iota(jnp.int32, (c, c), 0)
    ci = lax.broadcasted_iota(jnp.int32, (c, c), 1)
    causal = ri >= ci
    strict = ri > ci
    eye = (ri == ci).astype(F32)
    same16 = (ri // 16) == (ci // 16)
    same32 = (ri // 32) == (ci // 32)
    cross32 = same32 & jnp.logical_not(same16)
    cross64 = jnp.logical_not(same32)
    normw = normw_ref[...]
    a_neg = -jnp.exp(alog_ref[...])
    head_rows = (lax.broadcasted_iota(jnp.int32, (SUBLANES, LANES), 1)
                 == lax.broadcasted_iota(jnp.int32, (SUBLANES, LANES), 0) + GDN_HEADS).astype(F32)

    chains, qs, ks, kbs, vbs, kws, decays, egcs, kdecs, glasts = [], [], [], [], [], [], [], [], [], []
    for bi in range(GDN_BATCH):
        x = qkv_ref[bi]
        head = _short_conv(jnp.concatenate([tail_ref[bi], x[:SUBLANES]], axis=0), w)[SUBLANES:]
        conv = jnp.concatenate([head, _short_conv(x, w)[SUBLANES:]], axis=0)
        tail_ref[bi] = x[c - SUBLANES:, :]
        qkv = conv * _sigmoid(conv)

        ba = ba_ref[bi]
        beta_all = _sigmoid(ba)
        g_all = a_neg * _softplus(ba + dtb_ref[...])
        gc_all = _dot(tri_ref[...], g_all, HI)
        egc_all = jnp.exp(gc_all)
        gc_rows = lax.dot_general(head_rows, gc_all, NT_DIMS, precision=HI, preferred_element_type=F32)

        for h in range(GDN_HEADS):
            q = _l2norm(qkv[:, h * GDN_DIM:(h + 1) * GDN_DIM]) * (GDN_DIM ** -0.5)
            k = _l2norm(qkv[:, GDN_W + h * GDN_DIM:GDN_W + (h + 1) * GDN_DIM])
            v = qkv[:, 2 * GDN_W + h * GDN_DIM:2 * GDN_W + (h + 1) * GDN_DIM]
            gl = GDN_HEADS + h
            beta = beta_all[:, h:h + 1]
            gc = gc_all[:, gl:gl + 1]
            egc = egc_all[:, gl:gl + 1]
            gc_last = gc_all[c - 1:c, gl:gl + 1]
            kb = k * beta
            chains.append((bi, h))
            qs.append(q)
            ks.append(k)
            kbs.append(kb)
            vbs.append(v * beta)
            kws.append(kb * egc)
            decays.append(jnp.exp(jnp.where(causal, gc - gc_rows[h:h + 1, :], NEG)))
            egcs.append(egc)
            kdecs.append(k * jnp.exp(gc_last - gc))
            glasts.append(jnp.exp(gc_last))

    kk = _each(lambda kb, q, k: _mm(jnp.concatenate([kb, q], axis=0), k, NT_DIMS), kbs, qs, ks)
    lows = _each(lambda a, dec: jnp.where(strict, a[:c] * dec, 0.0), kk, decays)
    qks = _each(lambda a, dec: jnp.where(causal, a[c:] * dec, 0.0), kk, decays)
    ts = _unit_lower_inverse(lows, same16, cross32, cross64, eye)
    sols = _each(lambda t, vb, kw: _mm(t, jnp.concatenate([vb, kw], axis=1)), ts, vbs, kws)
    states = [state_ref[bi, h] for bi, h in chains]
    wss = _each(lambda sol, q, egc, s: _mm(jnp.concatenate([sol[:, GDN_DIM:], q * egc], axis=0), s),
                sols, qs, egcs, states)
    v_news = _each(lambda sol, ws: sol[:, :GDN_DIM] - ws[:c], sols, wss)
    outs = _each(lambda ws, qk, v_new: ws[c:] + _mm(qk, v_new), wss, qks, v_news)
    upds = _each(lambda kd, v_new: _mm(kd, v_new, TN_DIMS), kdecs, v_news)
    for (bi, h), s, gl, upd, o in zip(chains, states, glasts, upds, outs):
        hs = slice(h * GDN_DIM, (h + 1) * GDN_DIM)
        state_ref[bi, h] = s * gl + upd
        zh = z_ref[bi, :, hs]
        o_ref[bi, :, hs] = o * lax.rsqrt(jnp.mean(o * o, axis=-1, keepdims=True) + 1e-6) * normw * (zh * _sigmoid(zh))


def _gdn(qkv, z, ba, conv_w, a_log, dt_bias, norm_w, bsz, seq):
    c = GDN_CHUNK
    nb = GDN_BATCH
    convw = jnp.zeros((SUBLANES, 3 * GDN_W), F32).at[:CONV_K].set(conv_w)
    alog = jnp.zeros((1, LANES), F32).at[0, GDN_HEADS:2 * GDN_HEADS].set(a_log)
    dtb = jnp.zeros((1, LANES), F32).at[0, GDN_HEADS:2 * GDN_HEADS].set(dt_bias)
    tri = jnp.tril(jnp.ones((c, c), F32))
    row = lambda b, i: (b, i, 0)
    const = lambda b, i: (0, 0)
    out = pl.pallas_call(
        _gdn_kernel,
        grid=(bsz // nb, seq // c),
        in_specs=[pl.BlockSpec((nb, c, 3 * GDN_W), row), pl.BlockSpec((nb, c, GDN_W), row),
                  pl.BlockSpec((nb, c, LANES), row),
                  pl.BlockSpec((SUBLANES, 3 * GDN_W), const), pl.BlockSpec((1, LANES), const),
                  pl.BlockSpec((1, LANES), const), pl.BlockSpec((1, GDN_DIM), const), pl.BlockSpec((c, c), const)],
        out_specs=pl.BlockSpec((nb, c, GDN_W), row),
        out_shape=jax.ShapeDtypeStruct((bsz, seq, GDN_W), F32),
        scratch_shapes=[pltpu.VMEM((nb, GDN_HEADS, GDN_DIM, GDN_DIM), F32),
                        pltpu.VMEM((nb, SUBLANES, 3 * GDN_W), F32)],
        compiler_params=_params("arbitrary", "arbitrary"),
    )(qkv.reshape(bsz, seq, 3 * GDN_W), z.reshape(bsz, seq, GDN_W), ba.reshape(bsz, seq, LANES),
      convw, alog, dtb, norm_w.reshape(1, GDN_DIM), tri)
    return out.reshape(bsz * seq, GDN_W)


def _gelu_tanh(x):
    return x * (0.5 * (1.0 + jnp.tanh(math.sqrt(2.0 / math.pi) * (x + 0.044715 * (x * x * x)))))


def _s5_kernel(u_ref, wb_ref, a_ref, wc_ref, d_ref, wglu_ref, y_ref, utb_ref, ytb_ref, *rest, steps):
    bu_refs, h_ref = rest[:-1], rest[-1]

    @pl.when(pl.program_id(0) == 0)
    def _():
        h_ref[...] = jnp.zeros_like(h_ref)

    nb = h_ref.shape[0]
    nlc = S5_W // LANES
    for b in range(nb):
        for lc in range(nlc):
            utb_ref[lc, pl.ds(b, steps, stride=nb), :] = u_ref[b, :, lc * LANES:(lc + 1) * LANES]
    u = jnp.concatenate([utb_ref[lc] for lc in range(nlc)], axis=1)
    ub = u.astype(BF16)
    st = S5_SUPER_ST
    for j in range(S5_SUPER):
        bu_refs[j][...] = _dot(ub[:, S5_SUPER_CH * j:S5_SUPER_CH * (j + 1)], wb_ref[j])

    ys = []
    for j in range(S5_SUPER):
        bu_ref = bu_refs[j]
        re = slice(2 * st * j, 2 * st * j + st)
        im = slice(2 * st * j + st, 2 * st * (j + 1))
        ar = jnp.broadcast_to(a_ref[0:1, st * j:st * (j + 1)], (nb, st))
        ai = jnp.broadcast_to(a_ref[1:2, st * j:st * (j + 1)], (nb, st))
        hr, hi = h_ref[:, re], h_ref[:, im]
        for t in range(steps):
            rows = slice(t * nb, (t + 1) * nb)
            hr, hi = (ar * hr - ai * hi + bu_ref[rows, 0:st], ar * hi + ai * hr + bu_ref[rows, st:2 * st])
            bu_ref[rows, 0:st] = hr
            bu_ref[rows, st:2 * st] = hi
        h_ref[:, re] = hr
        h_ref[:, im] = hi
        ys.append(_dot(bu_ref[...].astype(BF16), wc_ref[j]))

    y = jnp.concatenate(ys, axis=1)
    y = _gelu_tanh(y + d_ref[...] * u)
    y = y * _sigmoid(_dot(y.astype(BF16), wglu_ref[...]))
    for lc in range(nlc):
        ytb_ref[lc] = y[:, lc * LANES:(lc + 1) * LANES]
    for b in range(nb):
        for lc in range(nlc):
            y_ref[b, :, lc * LANES:(lc + 1) * LANES] = ytb_ref[lc, pl.ds(b, steps, stride=nb), :]


def _s5(u3, lam_re, lam_im, log_dt, b_re, b_im, c_re, c_im, d_skip, w_glu):
    bsz, seq, _ = u3.shape
    dt = jnp.exp(log_dt)[:, None]
    mag = jnp.exp(lam_re * dt)
    a_re, a_im = mag * jnp.cos(lam_im * dt), mag * jnp.sin(lam_im * dt)
    den = lam_re * lam_re + lam_im * lam_im
    nr, ni = a_re - 1.0, a_im
    cr = (nr * lam_re + ni * lam_im) / den
    ci = (ni * lam_re - nr * lam_im) / den
    bb_re = cr[..., None] * b_re - ci[..., None] * b_im
    bb_im = cr[..., None] * b_im + ci[..., None] * b_re
    gps = S5_GROUPS // S5_SUPER
    eye = jnp.eye(gps, dtype=F32)

    def in_blockdiag(t):
        t = t.reshape(S5_SUPER, gps, S5_STATE, S5_GROUP_CH)
        return jnp.einsum('jgph,gk->jghkp', t, eye).reshape(S5_SUPER, S5_SUPER_CH, S5_SUPER_ST)

    def out_blockdiag(t):
        t = t.reshape(S5_SUPER, gps, S5_GROUP_CH, S5_STATE)
        return jnp.einsum('jgkp,gm->jgpmk', t, eye).reshape(S5_SUPER, S5_SUPER_ST, S5_SUPER_CH)

    wb = jnp.concatenate([in_blockdiag(bb_re), in_blockdiag(bb_im)], axis=2).astype(BF16)
    wc = jnp.concatenate([out_blockdiag(c_re), -out_blockdiag(c_im)], axis=1).astype(BF16)
    nstate = S5_GROUPS * S5_STATE
    a = jnp.stack([a_re.reshape(nstate), a_im.reshape(nstate)], axis=0)

    steps = S5_STEPS
    rows = steps * bsz
    blk = lambda i: (0, i, 0)
    const2 = lambda i: (0, 0)
    const3 = lambda i: (0, 0, 0)
    return pl.pallas_call(
        functools.partial(_s5_kernel, steps=steps),
        grid=(seq // steps,),
        in_specs=[pl.BlockSpec((bsz, steps, S5_W), blk), pl.BlockSpec(wb.shape, const3),
                  pl.BlockSpec((2, nstate), const2), pl.BlockSpec(wc.shape, const3),
                  pl.BlockSpec((1, S5_W), const2), pl.BlockSpec((S5_W, S5_W), const2)],
        out_specs=pl.BlockSpec((bsz, steps, S5_W), blk),
        out_shape=jax.ShapeDtypeStruct((bsz, seq, S5_W), F32),
        scratch_shapes=[pltpu.VMEM((S5_W // LANES, rows, LANES), F32), pltpu.VMEM((S5_W // LANES, rows, LANES), F32),
                        *[pltpu.VMEM((rows, 2 * S5_SUPER_ST), F32) for _ in range(S5_SUPER)],
                        pltpu.VMEM((bsz, 2 * nstate), F32)],
        compiler_params=_params("arbitrary"),
    )(u3, wb, a, wc, d_skip.reshape(1, S5_W), w_glu.astype(BF16))


def _route(h, wr_ref, br_ref, tri_ref, run_ref, route_ref, pos_ref, tab_ref):
    @pl.when(pl.program_id(0) == 0)
    def _():
        run_ref[...] = jnp.zeros_like(run_ref)

    tm = h.shape[0]
    h_hi, h_lo = _split_bf16(h)
    both = _dot(h_hi, wr_ref[...])
    logits = both[:, :LANES] + both[:, LANES:] + _dot(h_lo, wr_ref[:, :LANES]) + br_ref[...]
    lane = lax.broadcasted_iota(jnp.int32, (tm, LANES), 1)
    lanef = lane.astype(F32)
    big = float(LANES)

    gmask = lane < MOE_GROUPS
    gmax = jnp.max(jnp.where(gmask, logits, NEG), axis=-1, keepdims=True)
    gsum = jnp.sum(jnp.where(gmask, jnp.exp(logits - gmax), 0.0), axis=-1, keepdims=True)
    p_group = 1.0 / gsum
    gidx = jnp.min(jnp.where(gmask & (logits == gmax), lanef, big), axis=-1, keepdims=True)

    lo = EXPERT_LANE0 + EXPERTS_PER_GROUP * gidx
    emask = (lanef >= lo) & (lanef < lo + EXPERTS_PER_GROUP)
    el = jnp.where(emask, logits, NEG)
    t1 = jnp.max(el, axis=-1, keepdims=True)
    i1 = jnp.min(jnp.where(emask & (el == t1), lanef, big), axis=-1, keepdims=True)
    emask2 = emask & (lanef != i1)
    el2 = jnp.where(emask2, logits, NEG)
    t2 = jnp.max(el2, axis=-1, keepdims=True)
    i2 = jnp.min(jnp.where(emask2 & (el2 == t2), lanef, big), axis=-1, keepdims=True)
    e2 = jnp.exp(t2 - t1)
    gate1 = p_group / (1.0 + e2)
    gate2 = p_group * e2 / (1.0 + e2)

    oh1 = lanef == i1
    oh2 = lanef == i2
    oh = jnp.where(oh1 | oh2, 1.0, 0.0)
    cnt = jnp.sum(oh, axis=0, keepdims=True)
    ei = lax.broadcasted_iota(jnp.int32, (LANES, LANES), 0)
    ej = lax.broadcasted_iota(jnp.int32, (LANES, LANES), 1)
    start = _dot(jnp.broadcast_to(cnt, (SUBLANES, LANES)), (ei < ej).astype(F32), HI)[0:1]
    where = _dot(tri_ref[...], oh.astype(BF16)) + start
    pos1 = jnp.sum(jnp.where(oh1, where, 0.0), axis=-1, keepdims=True)
    pos2 = jnp.sum(jnp.where(oh2, where, 0.0), axis=-1, keepdims=True)

    run = run_ref[...]
    srow = lax.broadcasted_iota(jnp.int32, (SUBLANES, LANES), 0)
    slane = lax.broadcasted_iota(jnp.int32, (SUBLANES, LANES), 1)
    tab_ref[...] = jnp.where(srow == 0, cnt, jnp.where(srow == 1, run, jnp.where(srow == 2, start, 0.0)))
    run_ref[...] = run + cnt

    route = jnp.where(lane == 2, gate1, 0.0)
    route = jnp.where(lane == 3, gate2, route)
    route = jnp.where(lane == 4, pos1, route)
    route = jnp.where(lane == 5, pos2, route)
    route_ref[...] = route
    pick = (slane == srow + 4).astype(F32)
    pos_ref[...] = lax.dot_general(pick, route, NT_DIMS, precision=HI, preferred_element_type=F32)


def _mix_ab_kernel(ya_ref, yb_ref, x_ref, w_ref, g_ref, b_ref, wr_ref, br_ref, tri_ref,
                   h_ref, route_ref, pos_ref, tab_ref, run_ref):
    mix = _dot(ya_ref[...].astype(BF16), w_ref[0:GDN_W, :]) + _dot(yb_ref[...].astype(BF16), w_ref[GDN_W:, :])
    h = _layer_norm(ALPHA * x_ref[...] + mix, g_ref[...], b_ref[...])
    h_ref[...] = h
    _route(h, wr_ref, br_ref, tri_ref, run_ref, route_ref, pos_ref, tab_ref)


def _to_natural(view_ref, nat_ref, dil):
    rows = view_ref.shape[0]
    nlc = nat_ref.shape[0]
    for r in range(dil):
        for lc in range(nlc):
            col = (r * nlc + lc) * LANES
            nat_ref[lc, pl.ds(r, rows, stride=dil), :] = view_ref[:, col:col + LANES]
    return jnp.concatenate([nat_ref[lc] for lc in range(nlc)], axis=1)


def _mix_c_kernel(o1_ref, o4_ref, o16_ref, l1_ref, l4_ref, l16_ref, expand_ref, x_ref, w_ref, g_ref, b_ref,
                  wr_ref, br_ref, tri_ref, h_ref, route_ref, pos_ref, tab_ref, run_ref,
                  on4_ref, on16_ref, ln4_ref, ln16_ref):
    o4 = _to_natural(o4_ref, on4_ref, DILATIONS[1])
    o16 = _to_natural(o16_ref, on16_ref, DILATIONS[2])
    l2 = _to_natural(l4_ref, ln4_ref, DILATIONS[1])
    l3 = _to_natural(l16_ref, ln16_ref, DILATIONS[2])
    l1 = l1_ref[...]
    m = jnp.maximum(jnp.maximum(l1, l2), l3)
    e1, e2, e3 = jnp.exp(l1 - m), jnp.exp(l2 - m), jnp.exp(l3 - m)
    inv = 1.0 / (e1 + e2 + e3)
    ex = expand_ref[...]

    def spread(wt):
        hi, lo = _split_bf16(wt)
        return _dot(hi, ex) + _dot(lo, ex)

    o = spread(e1 * inv) * o1_ref[...] + spread(e2 * inv) * o4 + spread(e3 * inv) * o16
    mix = _dot(o.astype(BF16), w_ref[...])
    h = _layer_norm(ALPHA * x_ref[...] + mix, g_ref[...], b_ref[...])
    h_ref[...] = h
    _route(h, wr_ref, br_ref, tri_ref, run_ref, route_ref, pos_ref, tab_ref)


def _router_operands(wg, bg, we, be):
    wr = jnp.zeros((D_MODEL, LANES), F32).at[:, :MOE_GROUPS].set(wg).at[:, EXPERT_LANE0:EXPERT_LANE0 + N_EXPERTS].set(we)
    wr_hi = wr.astype(BF16)
    wr_lo = (wr - wr_hi.astype(F32)).astype(BF16)
    br = jnp.zeros((1, LANES), F32).at[0, :MOE_GROUPS].set(bg).at[0, EXPERT_LANE0:EXPERT_LANE0 + N_EXPERTS].set(be)
    tri = jnp.tril(jnp.ones((ROW_TILE, ROW_TILE), F32), -1).astype(BF16)
    return jnp.concatenate([wr_hi, wr_lo], axis=1), br, tri


def _mix_call(kernel_fn, acts, x2, w_out, ln_g, ln_b, router, extra=(), scratch=()):
    n = x2.shape[0]
    tm = ROW_TILE
    wr, br, tri = _router_operands(*router)
    row = lambda i: (i, 0)
    const = lambda i: (0, 0)
    in_specs = [pl.BlockSpec((rows, a.shape[1]), row) for a, rows in acts]
    in_specs += [pl.BlockSpec(e.shape, const) for e in extra]
    in_specs += [pl.BlockSpec((tm, D_MODEL), row), pl.BlockSpec(w_out.shape, const),
                 pl.BlockSpec((1, D_MODEL), const), pl.BlockSpec((1, D_MODEL), const),
                 pl.BlockSpec((D_MODEL, 2 * LANES), const), pl.BlockSpec((1, LANES), const),
                 pl.BlockSpec((tm, tm), const)]
    return pl.pallas_call(
        kernel_fn,
        grid=(n // tm,),
        in_specs=in_specs,
        out_specs=[pl.BlockSpec((tm, D_MODEL), row), pl.BlockSpec((tm, LANES), row),
                   pl.BlockSpec((SUBLANES, tm), lambda i: (0, i)),
                   pl.BlockSpec((None, SUBLANES, LANES), lambda i: (i, 0, 0))],
        out_shape=[jax.ShapeDtypeStruct((n, D_MODEL), F32), jax.ShapeDtypeStruct((n, LANES), F32),
                   jax.ShapeDtypeStruct((SUBLANES, n), F32), jax.ShapeDtypeStruct((n // tm, SUBLANES, LANES), F32)],
        scratch_shapes=[pltpu.VMEM((1, LANES), F32), *scratch],
        compiler_params=_params("arbitrary"),
    )(*[a for a, _ in acts], *extra, x2, w_out.astype(BF16), ln_g.reshape(1, D_MODEL), ln_b.reshape(1, D_MODEL),
      wr, br, tri)


SLAB = D_MODEL // LANES // 2
U32 = jnp.uint32
HIGH_HALF = 0xFFFF0000
SEG_SIZES = tuple(1 << k for k in range(ROW_TILE.bit_length() - 1, -1, -1))
SEG_RARE = 64


def _segment_tables(tab, n_blocks):
    lanes = slice(EXPERT_LANE0, EXPERT_LANE0 + N_EXPERTS)
    cnt = tab[:, 0, lanes].astype(jnp.int32)
    run = tab[:, 1, lanes].astype(jnp.int32)
    src = tab[:, 2, lanes].astype(jnp.int32)
    counts = run[-1] + cnt[-1]
    padded = (counts + MOE_ROWS - 1) // MOE_ROWS * MOE_ROWS
    padded_end = jnp.cumsum(padded)
    dst = (padded_end - padded)[None, :] + run
    tail = jnp.arange(N_EXPERTS, dtype=jnp.int32) * MOE_ROWS + padded_end[-1]
    tail_cnt = jnp.where(tail < n_blocks * MOE_ROWS, MOE_ROWS, 0)
    cnt = jnp.concatenate([cnt, (padded - counts)[None, :], tail_cnt[None, :]], axis=0)
    src = jnp.concatenate([src, jnp.zeros((2, N_EXPERTS), jnp.int32)], axis=0)
    dst = jnp.concatenate([dst, (padded_end - padded + counts)[None, :], tail[None, :]], axis=0)
    block_start = jnp.arange(n_blocks, dtype=jnp.int32) * MOE_ROWS
    block_expert = jnp.minimum(jnp.sum((padded_end[None, :] <= block_start[:, None]).astype(jnp.int32), axis=1),
                               N_EXPERTS - 1)
    n_used = (padded_end[-1:] // MOE_ROWS).astype(jnp.int32)
    return (cnt.reshape(-1), src.reshape(-1), dst.reshape(-1)), block_expert, n_used


def _slab_rows(ref, start, size):
    return ref.at[pl.ds(pl.multiple_of(start * SLAB, SLAB), size * SLAB)]


def _for_each_segment(tabs, tile, fn):
    cnt_ref, src_ref, dst_ref = tabs

    def per_expert(e, carry):
        idx = tile * N_EXPERTS + e
        c, s, d = cnt_ref[idx], src_ref[idx], dst_ref[idx]

        def pieces(sizes):
            for size in sizes:
                @pl.when((c & size) != 0)
                def _():
                    off = c & (-2 * size)
                    fn(s + off, d + off, size)

        @pl.when(c >= SEG_RARE)
        def _():
            pieces([size for size in SEG_SIZES if size >= SEG_RARE])

        pieces([size for size in SEG_SIZES if size < SEG_RARE])
        return carry

    lax.fori_loop(0, N_EXPERTS, per_expert, 0)


def _to_slabs(val, ref):
    for lc in range(SLAB):
        hi = pltpu.bitcast(val[:, lc * LANES:(lc + 1) * LANES], U32)
        lo = pltpu.bitcast(val[:, (lc + SLAB) * LANES:(lc + SLAB + 1) * LANES], U32)
        ref[pl.ds(lc, val.shape[0], stride=SLAB), :] = (hi & jnp.uint32(HIGH_HALF)) | (lo >> 16)


def _from_slabs(ref):
    rows = ref.shape[0] // SLAB
    words = [ref[pl.ds(lc, rows, stride=SLAB), :] for lc in range(SLAB)]
    return jnp.concatenate([pltpu.bitcast(w & jnp.uint32(HIGH_HALF), F32) for w in words]
                           + [pltpu.bitcast(w << 16, F32) for w in words], axis=1)


def _dispatch_kernel(cnt_ref, src_ref, dst_ref, h_ref, pos_ref, xs_ref, sort_ref, sem):
    tile = pl.program_id(0)
    last = pl.num_programs(0) - 1
    tm = h_ref.shape[0]
    slot = tile % 2
    tabs = (cnt_ref, src_ref, dst_ref)

    def copy(k, s, d, size):
        return pltpu.make_async_copy(_slab_rows(sort_ref.at[k], s, size), _slab_rows(xs_ref, d, size), sem.at[k])

    j = lax.broadcasted_iota(jnp.int32, (2 * tm, tm), 0).astype(F32)
    pos = pos_ref[...]
    perm = jnp.where((j == pos[0:1, :]) | (j == pos[1:2, :]), 1.0, 0.0).astype(BF16)
    _to_slabs(_dot(perm, h_ref[...].astype(BF16)), sort_ref.at[slot])
    _for_each_segment(tabs, tile, lambda s, d, size: copy(slot, s, d, size).start())

    def wait_tile(k):
        pltpu.make_async_copy(sort_ref.at[k], xs_ref.at[pl.ds(0, 2 * tm * SLAB)], sem.at[k]).wait()

    @pl.when(tile > 0)
    def _():
        wait_tile(1 - slot)

    @pl.when(tile == last)
    def _():
        wait_tile(slot)
        sort_ref[slot, 0:MOE_ROWS * SLAB, :] = jnp.zeros((MOE_ROWS * SLAB, LANES), U32)
        for pseudo in (1, 2):
            _for_each_segment(tabs, tile + pseudo, lambda s, d, size: copy(slot, s, d, size).start())
        for pseudo in (1, 2):
            _for_each_segment(tabs, tile + pseudo, lambda s, d, size: copy(slot, s, d, size).wait())


def _dispatch(tabs, h, pos, n_slots):
    n = h.shape[0]
    tm = ROW_TILE
    grid_spec = pltpu.PrefetchScalarGridSpec(
        num_scalar_prefetch=3,
        grid=(n // tm,),
        in_specs=[pl.BlockSpec((tm, D_MODEL), lambda i, *_: (i, 0)), pl.BlockSpec((SUBLANES, tm), lambda i, *_: (0, i))],
        out_specs=pl.BlockSpec(memory_space=pl.ANY),
        scratch_shapes=[pltpu.VMEM((2, 2 * tm * SLAB, LANES), U32), pltpu.SemaphoreType.DMA((2,))],
    )
    return pl.pallas_call(
        _dispatch_kernel,
        grid_spec=grid_spec,
        out_shape=jax.ShapeDtypeStruct((n_slots * SLAB, LANES), U32),
        compiler_params=_params("arbitrary"),
    )(*tabs, h, pos)


def _expert_kernel(be_ref, nu_ref, xs_ref, wg_ref, wu_ref, wd_ref, ys_ref, wgb_ref, wub_ref, wdb_ref):
    i = pl.program_id(0)

    @pl.when((i == 0) | (be_ref[i] != be_ref[jnp.maximum(i - 1, 0)]))
    def _():
        wgb_ref[...] = wg_ref[...].astype(BF16)
        wub_ref[...] = wu_ref[...].astype(BF16)
        wdb_ref[...] = wd_ref[...].astype(BF16)

    @pl.when(i < nu_ref[0])
    def _():
        x = _from_slabs(xs_ref).astype(BF16)
        hg = _dot(x, wgb_ref[...])
        hu = _dot(x, wub_ref[...])
        hdn = hg * _sigmoid(hg) * hu
        _to_slabs(_dot(hdn.astype(BF16), wdb_ref[...]).astype(BF16).astype(F32), ys_ref)

    @pl.when(i >= nu_ref[0])
    def _():
        ys_ref[...] = jnp.zeros_like(ys_ref)


def _experts(block_expert, n_used, xs, w_gate, w_up, w_down, layer):
    blk = MOE_ROWS * SLAB
    row = lambda i, be, nu: (i, 0)
    used_row = lambda i, be, nu: (jnp.minimum(i, jnp.maximum(nu[0] - 1, 0)), 0)
    wsel = lambda i, be, nu: (layer, be[i], 0, 0)
    grid_spec = pltpu.PrefetchScalarGridSpec(
        num_scalar_prefetch=2,
        grid=(xs.shape[0] // blk,),
        in_specs=[pl.BlockSpec((blk, LANES), used_row),
                  pl.BlockSpec((None, None, D_MODEL, EXPERT_FF), wsel),
                  pl.BlockSpec((None, None, D_MODEL, EXPERT_FF), wsel),
                  pl.BlockSpec((None, None, EXPERT_FF, D_MODEL), wsel)],
        out_specs=pl.BlockSpec((blk, LANES), row),
        scratch_shapes=[pltpu.VMEM((D_MODEL, EXPERT_FF), BF16), pltpu.VMEM((D_MODEL, EXPERT_FF), BF16),
                        pltpu.VMEM((EXPERT_FF, D_MODEL), BF16)],
    )
    return pl.pallas_call(
        _expert_kernel,
        grid_spec=grid_spec,
        out_shape=jax.ShapeDtypeStruct(xs.shape, U32),
        compiler_params=_params("arbitrary"),
    )(block_expert, n_used, xs, w_gate, w_up, w_down)


def _combine_kernel(cnt_ref, src_ref, dst_ref, h_ref, route_ref, ys_ref, g_ref, b_ref, o_ref, sort_ref, sem):
    tile = pl.program_id(0)
    last = pl.num_programs(0) - 1
    tm = h_ref.shape[0]
    slot = tile % 2
    tabs = (cnt_ref, src_ref, dst_ref)

    def copy(k, s, d, size):
        return pltpu.make_async_copy(_slab_rows(ys_ref, d, size), _slab_rows(sort_ref.at[k], s, size), sem.at[k])

    @pl.when(tile == 0)
    def _():
        _for_each_segment(tabs, tile, lambda s, d, size: copy(slot, s, d, size).start())

    @pl.when(tile < last)
    def _():
        _for_each_segment(tabs, tile + 1, lambda s, d, size: copy(1 - slot, s, d, size).start())

    pltpu.make_async_copy(ys_ref.at[pl.ds(0, 2 * tm * SLAB)], sort_ref.at[slot], sem.at[slot]).wait()
    ysorted = _from_slabs(sort_ref.at[slot]).astype(BF16)
    r = route_ref[...]
    j = lax.broadcasted_iota(jnp.int32, (2 * tm, 2 * tm), 1).astype(F32)
    where = jnp.concatenate([r[:, 4:5], r[:, 5:6]], axis=0)
    picked = _dot(jnp.where(j == where, 1.0, 0.0).astype(BF16), ysorted)
    ffn = picked[:tm] * r[:, 2:3] + picked[tm:] * r[:, 3:4]
    o_ref[...] = _layer_norm(ALPHA * h_ref[...] + ffn, g_ref[...], b_ref[...])


def _combine(tabs, h, route, ys, ln_g, ln_b):
    n = h.shape[0]
    tm = ROW_TILE
    row = lambda i, *_: (i, 0)
    const = lambda i, *_: (0, 0)
    grid_spec = pltpu.PrefetchScalarGridSpec(
        num_scalar_prefetch=3,
        grid=(n // tm,),
        in_specs=[pl.BlockSpec((tm, D_MODEL), row), pl.BlockSpec((tm, LANES), row), pl.BlockSpec(memory_space=pl.ANY),
                  pl.BlockSpec((1, D_MODEL), const), pl.BlockSpec((1, D_MODEL), const)],
        out_specs=pl.BlockSpec((tm, D_MODEL), row),
        scratch_shapes=[pltpu.VMEM((2, 2 * tm * SLAB, LANES), U32), pltpu.SemaphoreType.DMA((2,))],
    )
    return pl.pallas_call(
        _combine_kernel,
        grid_spec=grid_spec,
        out_shape=jax.ShapeDtypeStruct((n, D_MODEL), F32),
        compiler_params=_params("arbitrary"),
    )(*tabs, h, route, ys, ln_g.reshape(1, D_MODEL), ln_b.reshape(1, D_MODEL))


def _moe(h, route, pos, tab, w_gate, w_up, w_down, ln_g, ln_b, layer):
    n = h.shape[0]
    n_slots = n * 2 + N_EXPERTS * MOE_ROWS
    tabs, block_expert, n_used = _segment_tables(tab, n_slots // MOE_ROWS)
    xs = _dispatch(tabs, h, pos, n_slots)
    ys = _experts(block_expert, n_used, xs, w_gate, w_up, w_down, layer)
    return _combine(tabs, h, route, ys, ln_g, ln_b)


def _qkv_kernel(x_ref, w_ref, o1_ref, o4_ref, o16_ref, acc_ref):
    acc = _dot(x_ref[...].astype(BF16), w_ref[...])
    o1_ref[...] = acc.astype(BF16)
    nlc = acc_ref.shape[0]
    for lc in range(nlc):
        acc_ref[lc] = acc[:, lc * LANES:(lc + 1) * LANES]
    for o_ref, dil in ((o4_ref, DILATIONS[1]), (o16_ref, DILATIONS[2])):
        rows = o_ref.shape[0]
        for r in range(dil):
            for lc in range(nlc):
                col = (r * nlc + lc) * LANES
                o_ref[:, col:col + LANES] = acc_ref[lc, pl.ds(r, rows, stride=dil), :].astype(BF16)


def _qkv_proj(h, w_qkv):
    n = h.shape[0]
    tm = ROW_TILE
    wd = 3 * D_MODEL
    d4, d16 = DILATIONS[1], DILATIONS[2]
    row = lambda i: (i, 0)
    return pl.pallas_call(
        _qkv_kernel,
        grid=(n // tm,),
        in_specs=[pl.BlockSpec((tm, D_MODEL), row), pl.BlockSpec((D_MODEL, wd), lambda i: (0, 0))],
        out_specs=[pl.BlockSpec((tm, wd), row), pl.BlockSpec((tm // d4, d4 * wd), row),
                   pl.BlockSpec((tm // d16, d16 * wd), row)],
        out_shape=[jax.ShapeDtypeStruct((n, wd), BF16), jax.ShapeDtypeStruct((n // d4, d4 * wd), BF16),
                   jax.ShapeDtypeStruct((n // d16, d16 * wd), BF16)],
        scratch_shapes=[pltpu.VMEM((wd // LANES, tm, LANES), F32)],
        compiler_params=_params("arbitrary"),
    )(h, w_qkv.astype(BF16))


ATTN_QBLOCKS = 2


def _attn_kernel(q_ref, kc_ref, vc_ref, kp_ref, vp_ref, o_ref, lse_ref):
    t = ATTN_STEPS
    qi = lax.broadcasted_iota(jnp.int32, (t, 2 * t), 0)
    kj = lax.broadcasted_iota(jnp.int32, (t, 2 * t), 1)
    dist = t + qi - kj
    band = (dist >= 0) & (dist <= t)
    lane = lax.broadcasted_iota(jnp.int32, (t, LANES), 1)
    upper = lane >= ATTN_DIM
    ones_kv = jnp.ones((2 * t, LANES), BF16)
    lanes_per_head = LANES // ATTN_HEADS
    scale = jnp.asarray(ATTN_DIM ** -0.5, BF16)
    for qb in range(ATTN_QBLOCKS):
        rows = slice(qb * t, (qb + 1) * t)
        valid = band & ((kj >= t) | (pl.program_id(2) > 0)) if qb == 0 else band
        scores, vals = [], []
        for hp in range(ATTN_HEADS // 2):
            cs = slice(hp * LANES, (hp + 1) * LANES)
            q2 = q_ref[rows, cs] * scale
            if qb == 0:
                k2 = jnp.concatenate([kp_ref[:, cs], kc_ref[0:t, cs]], axis=0)
                v2 = jnp.concatenate([vp_ref[:, cs], vc_ref[0:t, cs]], axis=0)
            else:
                k2 = kc_ref[(qb - 1) * t:(qb + 1) * t, cs]
                v2 = vc_ref[(qb - 1) * t:(qb + 1) * t, cs]
            vals.append(jnp.concatenate([v2, ones_kv], axis=1))
            for sub in range(2):
                mine = upper if sub else jnp.logical_not(upper)
                qm = jnp.where(mine, q2, jnp.zeros_like(q2))
                scores.append(lax.dot_general(qm, k2, NT_DIMS, preferred_element_type=F32))
        maxes, probs = [], []
        for s in scores:
            s = jnp.where(valid, s, NEG)
            m = jnp.max(s, axis=-1, keepdims=True)
            maxes.append(m)
            probs.append(jnp.exp(s - m).astype(BF16))
        both = [_dot(p, vals[head // 2]) for head, p in enumerate(probs)]
        nums = [r[:, :LANES] for r in both]
        dens = [r[:, LANES:] for r in both]
        lse_blk = jnp.zeros((t, LANES), F32)
        for head in range(ATTN_HEADS):
            lse_blk = jnp.where(lane // lanes_per_head == head, maxes[head] + jnp.log(dens[head]), lse_blk)
        for hp in range(ATTN_HEADS // 2):
            lo, hi = 2 * hp, 2 * hp + 1
            o_ref[rows, hp * LANES:(hp + 1) * LANES] = jnp.where(upper, nums[hi] / dens[hi], nums[lo] / dens[lo])
        lse_ref[rows, :] = lse_blk


def _attn_branch(qkv_view, bsz, seq, dil):
    length = seq // dil
    t = ATTN_STEPS
    tq = t * ATTN_QBLOCKS
    qkv_v = qkv_view.reshape(bsz, length, dil * 3 * D_MODEL)
    cur = lambda part: (lambda b, r, i: (b, i, 3 * r + part))
    prev = lambda part: (lambda b, r, i: (b, jnp.maximum(i * ATTN_QBLOCKS - 1, 0), 3 * r + part))
    blk = (None, tq, D_MODEL)
    pblk = (None, t, D_MODEL)
    o, lse = pl.pallas_call(
        _attn_kernel,
        grid=(bsz, dil, length // tq),
        in_specs=[pl.BlockSpec(blk, cur(0)), pl.BlockSpec(blk, cur(1)), pl.BlockSpec(blk, cur(2)),
                  pl.BlockSpec(pblk, prev(1)), pl.BlockSpec(pblk, prev(2))],
        out_specs=[pl.BlockSpec(blk, lambda b, r, i: (b, i, r)), pl.BlockSpec((None, tq, LANES), lambda b, r, i: (b, i, r))],
        out_shape=[jax.ShapeDtypeStruct((bsz, length, dil * D_MODEL), F32),
                   jax.ShapeDtypeStruct((bsz, length, dil * LANES), F32)],
        compiler_params=_params("arbitrary", "arbitrary", "arbitrary"),
    )(qkv_v, qkv_v, qkv_v, qkv_v, qkv_v)
    return o.reshape(bsz * length, dil * D_MODEL), lse.reshape(bsz * length, dil * LANES)


def kernel(x, w_in_ab, conv_qkv, gdn_a_log, gdn_dt_bias, gdn_norm, s5_lam_re, s5_lam_im, s5_log_dt, s5_b_re, s5_b_im, s5_c_re, s5_c_im, s5_d, s5_w_glu, w_out_ab, w_qkv_c, w_out_c, ln_mix_g, ln_mix_b, router_group_w, router_group_b, router_expert_w, router_expert_b, moe_w_gate, moe_w_up, moe_w_down, ln_ffn_g, ln_ffn_b):
    bsz, seq, d = x.shape
    n = bsz * seq
    x2 = x.reshape(n, d)
    tm = ROW_TILE

    w_in = w_in_ab[0]
    nq = 4 * GDN_W
    ba_cols = jnp.zeros((d, LANES), F32).at[:, :2 * GDN_HEADS].set(w_in[:, nq:nq + 2 * GDN_HEADS])
    w_cat = jnp.concatenate([w_in[:, :nq], w_in[:, nq + 2 * GDN_HEADS:], ba_cols], axis=1).astype(BF16)
    qkv, z, u, ba = _inproj(x2, w_cat)
    ya = _gdn(qkv, z, ba, conv_qkv[0], gdn_a_log[0], gdn_dt_bias[0], gdn_norm[0], bsz, seq)
    yb = _s5(u.reshape(bsz, seq, S5_W), s5_lam_re[0], s5_lam_im[0], s5_log_dt[0], s5_b_re[0], s5_b_im[0],
             s5_c_re[0], s5_c_im[0], s5_d[0], s5_w_glu[0]).reshape(n, S5_W)
    router0 = (router_group_w[0], router_group_b[0], router_expert_w[0], router_expert_b[0])
    h, route, pos, tab = _mix_call(_mix_ab_kernel, ((ya, tm), (yb, tm)), x2, w_out_ab[0], ln_mix_g[0], ln_mix_b[0], router0)
    h = _moe(h, route, pos, tab, moe_w_gate, moe_w_up, moe_w_down, ln_ffn_g[0], ln_ffn_b[0], 0)

    acts = []
    for qkv_view, dil in zip(_qkv_proj(h, w_qkv_c[0]), DILATIONS):
        acts.append(_attn_branch(qkv_view, bsz, seq, dil))
    lanes_per_head = LANES // ATTN_HEADS
    expand = (jnp.arange(LANES)[:, None] == (jnp.arange(D_MODEL)[None, :] // ATTN_DIM) * lanes_per_head).astype(BF16)
    router1 = (router_group_w[1], router_group_b[1], router_expert_w[1], router_expert_b[1])
    d4, d16 = DILATIONS[1], DILATIONS[2]
    (o1, l1), (o4, l4), (o16, l16) = acts
    h, route, pos, tab = _mix_call(
        _mix_c_kernel, ((o1, tm), (o4, tm // d4), (o16, tm // d16), (l1, tm), (l4, tm // d4), (l16, tm // d16)),
        h, w_out_c[0], ln_mix_g[1], ln_mix_b[1], router1, extra=(expand,),
        scratch=(pltpu.VMEM((D_MODEL // LANES, tm, LANES), F32), pltpu.VMEM((D_MODEL // LANES, tm, LANES), F32),
                 pltpu.VMEM((1, tm, LANES), F32), pltpu.VMEM((1, tm, LANES), F32)))
    h = _moe(h, route, pos, tab, moe_w_gate, moe_w_up, moe_w_down, ln_ffn_g[1], ln_ffn_b[1], 1)
    return h.reshape(bsz, seq, d)
```

```python
import functools
import math

import jax
import jax.numpy as jnp
from jax import lax
from jax.experimental import pallas as pl
from jax.experimental.pallas import tpu as pltpu

F32 = jnp.float32
BF16 = jnp.bfloat16
HI = lax.Precision.HIGHEST

D_MODEL = 1024
DEPTH = 2
ALPHA = (2 * DEPTH) ** 0.25
LN_EPS = 1e-5

GDN_HEADS = 4
GDN_DIM = 128
GDN_W = GDN_HEADS * GDN_DIM
CONV_K = 4
GDN_CHUNK = 64
GDN_BATCH = 4

S5_W = 512
S5_GROUP_CH = 16
S5_GROUPS = 32
S5_STATE = 64
S5_SUPER = 4
S5_SUPER_CH = S5_W // S5_SUPER
S5_SUPER_ST = S5_GROUPS * S5_STATE // S5_SUPER

ATTN_HEADS = 16
ATTN_DIM = 64
ATTN_STEPS = 128
DILATIONS = (1, 4, 16)

MOE_GROUPS = 4
EXPERTS_PER_GROUP = 8
N_EXPERTS = 32
EXPERT_FF = 512
EXPERT_LANE0 = MOE_GROUPS

LANES = 128
SUBLANES = 8
VMEM_LIMIT = 56 * 1024 * 1024

ROW_TILE = 512
S5_STEPS = 128
MOE_ROWS = 512
GATHER_TILE = 256

NEG = -1e30
NT_DIMS = (((1,), (1,)), ((), ()))
TN_DIMS = (((0,), (0,)), ((), ()))


def _params(*sem):
    return pltpu.CompilerParams(dimension_semantics=sem, vmem_limit_bytes=VMEM_LIMIT)


def _sigmoid(x):
    return 1.0 / (1.0 + jnp.exp(-x))


def _softplus(x):
    return jnp.maximum(x, 0.0) + jnp.log(1.0 + jnp.exp(-jnp.abs(x)))


def _layer_norm(r, g, b):
    mu = jnp.mean(r, axis=-1, keepdims=True)
    c = r - mu
    var = jnp.mean(c * c, axis=-1, keepdims=True)
    return c * lax.rsqrt(var + LN_EPS) * g + b


def _dot(a, b, precision=None):
    return jnp.dot(a, b, precision=precision, preferred_element_type=F32)


def _mm(a, b, dims=None):
    a, b = a.astype(BF16), b.astype(BF16)
    if dims is None:
        return jnp.dot(a, b, preferred_element_type=F32)
    return lax.dot_general(a, b, dims, preferred_element_type=F32)


def _split_bf16(x):
    hi = x.astype(BF16)
    return hi, (x - hi.astype(F32)).astype(BF16)


IN_COLS = 3 * GDN_W + GDN_W + S5_W + LANES


def _inproj_kernel(x_ref, w_ref, qkv_ref, z_ref, u_ref, ba_ref):
    x = x_ref[...].astype(BF16)
    qkv_ref[...] = _dot(x, w_ref[:, 0:1536])
    z_ref[...] = _dot(x, w_ref[:, 1536:2048])
    u_ref[...] = _dot(x, w_ref[:, 2048:2560])
    ba_ref[...] = _dot(x, w_ref[:, 2560:2688])


def _inproj(x2, w_cat):
    n = x2.shape[0]
    tm = ROW_TILE
    row = lambda i: (i, 0)
    return pl.pallas_call(
        _inproj_kernel,
        grid=(n // tm,),
        in_specs=[pl.BlockSpec((tm, D_MODEL), row), pl.BlockSpec((D_MODEL, IN_COLS), lambda i: (0, 0))],
        out_specs=[pl.BlockSpec((tm, 3 * GDN_W), row), pl.BlockSpec((tm, GDN_W), row),
                   pl.BlockSpec((tm, S5_W), row), pl.BlockSpec((tm, LANES), row)],
        out_shape=[jax.ShapeDtypeStruct((n, 3 * GDN_W), F32), jax.ShapeDtypeStruct((n, GDN_W), F32),
                   jax.ShapeDtypeStruct((n, S5_W), F32), jax.ShapeDtypeStruct((n, LANES), F32)],
        compiler_params=_params("arbitrary"),
    )(x2, w_cat)


def _l2norm(t):
    return t * lax.rsqrt(jnp.sum(t * t, axis=-1, keepdims=True) + 1e-6)


def _short_conv(x, taps):
    out = x * taps[CONV_K - 1:CONV_K, :]
    for j in range(1, CONV_K):
        out = out + pltpu.roll(x, j, axis=0) * taps[CONV_K - 1 - j:CONV_K - j, :]
    return out


def _each(fn, *lists):
    return [fn(*args) for args in zip(*lists)]


def _unit_lower_inverse(lows, same16, cross32, cross64, eye):
    ps = [jnp.where(same16, low, 0.0) for low in lows]
    ts = [eye - p for p in ps]
    for _ in range(3):
        ps = _each(lambda p: _mm(p, p), ps)
        ts = _each(lambda t, p: _mm(t, eye + p), ts, ps)
    for cross in (cross32, cross64):
        tc = _each(lambda t, low: _mm(t, jnp.where(cross, low, 0.0)), ts, lows)
        ts = _each(lambda t, x: t - _mm(x, t), ts, tc)
    return ts


def _gdn_kernel(qkv_ref, z_ref, ba_ref, convw_ref, alog_ref, dtb_ref, normw_ref, tri_ref, o_ref,
                state_ref, tail_ref):
    c = GDN_CHUNK

    @pl.when(pl.program_id(1) == 0)
    def _():
        state_ref[...] = jnp.zeros_like(state_ref)
        tail_ref[...] = jnp.zeros_like(tail_ref)

    w = convw_ref[...]
    ri = lax.broadcasted_iota(jnp.int32, (c, c), 0)
    ci = lax.broadcasted_iota(jnp.int32, (c, c), 1)
    causal = ri >= ci
    strict = ri > ci
    eye = (ri == ci).astype(F32)
    same16 = (ri // 16) == (ci // 16)
    same32 = (ri // 32) == (ci // 32)
    cross32 = same32 & jnp.logical_not(same16)
    cross64 = jnp.logical_not(same32)
    normw = normw_ref[...]
    a_neg = -jnp.exp(alog_ref[...])
    head_rows = (lax.broadcasted_iota(jnp.int32, (SUBLANES, LANES), 1)
                 == lax.broadcasted_iota(jnp.int32, (SUBLANES, LANES), 0) + GDN_HEADS).astype(F32)

    chains, qs, ks, kbs, vbs, kws, decays, egcs, kdecs, glasts = [], [], [], [], [], [], [], [], [], []
    for bi in range(GDN_BATCH):
        x = qkv_ref[bi]
        head = _short_conv(jnp.concatenate([tail_ref[bi], x[:SUBLANES]], axis=0), w)[SUBLANES:]
        conv = jnp.concatenate([head, _short_conv(x, w)[SUBLANES:]], axis=0)
        tail_ref[bi] = x[c - SUBLANES:, :]
        qkv = conv * _sigmoid(conv)

        ba = ba_ref[bi]
        beta_all = _sigmoid(ba)
        g_all = a_neg * _softplus(ba + dtb_ref[...])
        gc_all = _dot(tri_ref[...], g_all, HI)
        egc_all = jnp.exp(gc_all)
        gc_rows = lax.dot_general(head_rows, gc_all, NT_DIMS, precision=HI, preferred_element_type=F32)

        for h in range(GDN_HEADS):
            q = _l2norm(qkv[:, h * GDN_DIM:(h + 1) * GDN_DIM]) * (GDN_DIM ** -0.5)
            k = _l2norm(qkv[:, GDN_W + h * GDN_DIM:GDN_W + (h + 1) * GDN_DIM])
            v = qkv[:, 2 * GDN_W + h * GDN_DIM:2 * GDN_W + (h + 1) * GDN_DIM]
            gl = GDN_HEADS + h
            beta = beta_all[:, h:h + 1]
            gc = gc_all[:, gl:gl + 1]
            egc = egc_all[:, gl:gl + 1]
            gc_last = gc_all[c - 1:c, gl:gl + 1]
            kb = k * beta
            chains.append((bi, h))
            qs.append(q)
            ks.append(k)
            kbs.append(kb)
            vbs.append(v * beta)
            kws.append(kb * egc)
            decays.append(jnp.exp(jnp.where(causal, gc - gc_rows[h:h + 1, :], NEG)))
            egcs.append(egc)
            kdecs.append(k * jnp.exp(gc_last - gc))
            glasts.append(jnp.exp(gc_last))

    kk = _each(lambda kb, q, k: _mm(jnp.concatenate([kb, q], axis=0), k, NT_DIMS), kbs, qs, ks)
    lows = _each(lambda a, dec: jnp.where(strict, a[:c] * dec, 0.0), kk, decays)
    qks = _each(lambda a, dec: jnp.where(causal, a[c:] * dec, 0.0), kk, decays)
    ts = _unit_lower_inverse(lows, same16, cross32, cross64, eye)
    sols = _each(lambda t, vb, kw: _mm(t, jnp.concatenate([vb, kw], axis=1)), ts, vbs, kws)
    states = [state_ref[bi, h] for bi, h in chains]
    wss = _each(lambda sol, q, egc, s: _mm(jnp.concatenate([sol[:, GDN_DIM:], q * egc], axis=0), s),
                sols, qs, egcs, states)
    v_news = _each(lambda sol, ws: sol[:, :GDN_DIM] - ws[:c], sols, wss)
    outs = _each(lambda ws, qk, v_new: ws[c:] + _mm(qk, v_new), wss, qks, v_news)
    upds = _each(lambda kd, v_new: _mm(kd, v_new, TN_DIMS), kdecs, v_news)
    for (bi, h), s, gl, upd, o in zip(chains, states, glasts, upds, outs):
        hs = slice(h * GDN_DIM, (h + 1) * GDN_DIM)
        state_ref[bi, h] = s * gl + upd
        zh = z_ref[bi, :, hs]
        o_ref[bi, :, hs] = o * lax.rsqrt(jnp.mean(o * o, axis=-1, keepdims=True) + 1e-6) * normw * (zh * _sigmoid(zh))


def _gdn(qkv, z, ba, conv_w, a_log, dt_bias, norm_w, bsz, seq):
    c = GDN_CHUNK
    nb = GDN_BATCH
    convw = jnp.zeros((SUBLANES, 3 * GDN_W), F32).at[:CONV_K].set(conv_w)
    alog = jnp.zeros((1, LANES), F32).at[0, GDN_HEADS:2 * GDN_HEADS].set(a_log)
    dtb = jnp.zeros((1, LANES), F32).at[0, GDN_HEADS:2 * GDN_HEADS].set(dt_bias)
    tri = jnp.tril(jnp.ones((c, c), F32))
    row = lambda b, i: (b, i, 0)
    const = lambda b, i: (0, 0)
    out = pl.pallas_call(
        _gdn_kernel,
        grid=(bsz // nb, seq // c),
        in_specs=[pl.BlockSpec((nb, c, 3 * GDN_W), row), pl.BlockSpec((nb, c, GDN_W), row),
                  pl.BlockSpec((nb, c, LANES), row),
                  pl.BlockSpec((SUBLANES, 3 * GDN_W), const), pl.BlockSpec((1, LANES), const),
                  pl.BlockSpec((1, LANES), const), pl.BlockSpec((1, GDN_DIM), const), pl.BlockSpec((c, c), const)],
        out_specs=pl.BlockSpec((nb, c, GDN_W), row),
        out_shape=jax.ShapeDtypeStruct((bsz, seq, GDN_W), F32),
        scratch_shapes=[pltpu.VMEM((nb, GDN_HEADS, GDN_DIM, GDN_DIM), F32),
                        pltpu.VMEM((nb, SUBLANES, 3 * GDN_W), F32)],
        compiler_params=_params("arbitrary", "arbitrary"),
    )(qkv.reshape(bsz, seq, 3 * GDN_W), z.reshape(bsz, seq, GDN_W), ba.reshape(bsz, seq, LANES),
      convw, alog, dtb, norm_w.reshape(1, GDN_DIM), tri)
    return out.reshape(bsz * seq, GDN_W)


def _gelu_tanh(x):
    return x * (0.5 * (1.0 + jnp.tanh(math.sqrt(2.0 / math.pi) * (x + 0.044715 * (x * x * x)))))


def _s5_kernel(u_ref, wb_ref, a_ref, wc_ref, d_ref, wglu_ref, y_ref, utb_ref, ytb_ref, *rest, steps):
    bu_refs, h_ref = rest[:-1], rest[-1]

    @pl.when(pl.program_id(0) == 0)
    def _():
        h_ref[...] = jnp.zeros_like(h_ref)

    nb = h_ref.shape[0]
    nlc = S5_W // LANES
    for b in range(nb):
        for lc in range(nlc):
            utb_ref[lc, pl.ds(b, steps, stride=nb), :] = u_ref[b, :, lc * LANES:(lc + 1) * LANES]
    u = jnp.concatenate([utb_ref[lc] for lc in range(nlc)], axis=1)
    ub = u.astype(BF16)
    st = S5_SUPER_ST
    for j in range(S5_SUPER):
        bu_refs[j][...] = _dot(ub[:, S5_SUPER_CH * j:S5_SUPER_CH * (j + 1)], wb_ref[j])

    ys = []
    for j in range(S5_SUPER):
        bu_ref = bu_refs[j]
        re = slice(2 * st * j, 2 * st * j + st)
        im = slice(2 * st * j + st, 2 * st * (j + 1))
        ar = jnp.broadcast_to(a_ref[0:1, st * j:st * (j + 1)], (nb, st))
        ai = jnp.broadcast_to(a_ref[1:2, st * j:st * (j + 1)], (nb, st))
        hr, hi = h_ref[:, re], h_ref[:, im]
        for t in range(steps):
            rows = slice(t * nb, (t + 1) * nb)
            hr, hi = (ar * hr - ai * hi + bu_ref[rows, 0:st], ar * hi + ai * hr + bu_ref[rows, st:2 * st])
            bu_ref[rows, 0:st] = hr
            bu_ref[rows, st:2 * st] = hi
        h_ref[:, re] = hr
        h_ref[:, im] = hi
        ys.append(_dot(bu_ref[...].astype(BF16), wc_ref[j]))

    y = jnp.concatenate(ys, axis=1)
    y = _gelu_tanh(y + d_ref[...] * u)
    y = y * _sigmoid(_dot(y.astype(BF16), wglu_ref[...]))
    for lc in range(nlc):
        ytb_ref[lc] = y[:, lc * LANES:(lc + 1) * LANES]
    for b in range(nb):
        for lc in range(nlc):
            y_ref[b, :, lc * LANES:(lc + 1) * LANES] = ytb_ref[lc, pl.ds(b, steps, stride=nb), :]


def _s5(u3, lam_re, lam_im, log_dt, b_re, b_im, c_re, c_im, d_skip, w_glu):
    bsz, seq, _ = u3.shape
    dt = jnp.exp(log_dt)[:, None]
    mag = jnp.exp(lam_re * dt)
    a_re, a_im = mag * jnp.cos(lam_im * dt), mag * jnp.sin(lam_im * dt)
    den = lam_re * lam_re + lam_im * lam_im
    nr, ni = a_re - 1.0, a_im
    cr = (nr * lam_re + ni * lam_im) / den
    ci = (ni * lam_re - nr * lam_im) / den
    bb_re = cr[..., None] * b_re - ci[..., None] * b_im
    bb_im = cr[..., None] * b_im + ci[..., None] * b_re
    gps = S5_GROUPS // S5_SUPER
    eye = jnp.eye(gps, dtype=F32)

    def in_blockdiag(t):
        t = t.reshape(S5_SUPER, gps, S5_STATE, S5_GROUP_CH)
        return jnp.einsum('jgph,gk->jghkp', t, eye).reshape(S5_SUPER, S5_SUPER_CH, S5_SUPER_ST)

    def out_blockdiag(t):
        t = t.reshape(S5_SUPER, gps, S5_GROUP_CH, S5_STATE)
        return jnp.einsum('jgkp,gm->jgpmk', t, eye).reshape(S5_SUPER, S5_SUPER_ST, S5_SUPER_CH)

    wb = jnp.concatenate([in_blockdiag(bb_re), in_blockdiag(bb_im)], axis=2).astype(BF16)
    wc = jnp.concatenate([out_blockdiag(c_re), -out_blockdiag(c_im)], axis=1).astype(BF16)
    nstate = S5_GROUPS * S5_STATE
    a = jnp.stack([a_re.reshape(nstate), a_im.reshape(nstate)], axis=0)

    steps = S5_STEPS
    rows = steps * bsz
    blk = lambda i: (0, i, 0)
    const2 = lambda i: (0, 0)
    const3 = lambda i: (0, 0, 0)
    return pl.pallas_call(
        functools.partial(_s5_kernel, steps=steps),
        grid=(seq // steps,),
        in_specs=[pl.BlockSpec((bsz, steps, S5_W), blk), pl.BlockSpec(wb.shape, const3),
                  pl.BlockSpec((2, nstate), const2), pl.BlockSpec(wc.shape, const3),
                  pl.BlockSpec((1, S5_W), const2), pl.BlockSpec((S5_W, S5_W), const2)],
        out_specs=pl.BlockSpec((bsz, steps, S5_W), blk),
        out_shape=jax.ShapeDtypeStruct((bsz, seq, S5_W), F32),
        scratch_shapes=[pltpu.VMEM((S5_W // LANES, rows, LANES), F32), pltpu.VMEM((S5_W // LANES, rows, LANES), F32),
                        *[pltpu.VMEM((rows, 2 * S5_SUPER_ST), F32) for _ in range(S5_SUPER)],
                        pltpu.VMEM((bsz, 2 * nstate), F32)],
        compiler_params=_params("arbitrary"),
    )(u3, wb, a, wc, d_skip.reshape(1, S5_W), w_glu.astype(BF16))


ROUTE_PARTS = 2


def _row_parts(tm):
    rows = tm // ROUTE_PARTS
    return [slice(k * rows, (k + 1) * rows) for k in range(ROUTE_PARTS)]


def _route(hs, wr_ref, br_ref, tri_ref, run_ref, route_ref, pos_ref, tab_ref):
    @pl.when(pl.program_id(0) == 0)
    def _():
        run_ref[...] = jnp.zeros_like(run_ref)

    rows = hs[0].shape[0]
    splits = [_split_bf16(h) for h in hs]
    both = [_dot(hi, wr_ref[...]) for hi, _ in splits]
    cross = [_dot(lo, wr_ref[:, :LANES]) for _, lo in splits]
    logits = [b[:, :LANES] + b[:, LANES:] + c + br_ref[...] for b, c in zip(both, cross)]
    lane = lax.broadcasted_iota(jnp.int32, (rows, LANES), 1)
    lanef = lane.astype(F32)
    big = float(LANES)

    def rmax(x):
        return jnp.max(x, axis=-1, keepdims=True)

    def rmin(x):
        return jnp.min(x, axis=-1, keepdims=True)

    def rsum(x):
        return jnp.sum(x, axis=-1, keepdims=True)

    gmask = lane < MOE_GROUPS
    gmax = [rmax(jnp.where(gmask, l, NEG)) for l in logits]
    gsum = [rsum(jnp.where(gmask, jnp.exp(l - m), 0.0)) for l, m in zip(logits, gmax)]
    gidx = [rmin(jnp.where(gmask & (l == m), lanef, big)) for l, m in zip(logits, gmax)]

    lows = [EXPERT_LANE0 + EXPERTS_PER_GROUP * g for g in gidx]
    emask = [(lanef >= lo) & (lanef < lo + EXPERTS_PER_GROUP) for lo in lows]
    el = [jnp.where(em, l, NEG) for em, l in zip(emask, logits)]
    t1 = [rmax(e) for e in el]
    i1 = [rmin(jnp.where(em & (e == t), lanef, big)) for em, e, t in zip(emask, el, t1)]
    emask2 = [em & (lanef != i) for em, i in zip(emask, i1)]
    el2 = [jnp.where(em, l, NEG) for em, l in zip(emask2, logits)]
    t2 = [rmax(e) for e in el2]
    i2 = [rmin(jnp.where(em & (e == t), lanef, big)) for em, e, t in zip(emask2, el2, t2)]
    e2 = [jnp.exp(b - a) for a, b in zip(t1, t2)]
    gate1 = [(1.0 / gs) / (1.0 + e) for gs, e in zip(gsum, e2)]
    gate2 = [(1.0 / gs) * e / (1.0 + e) for gs, e in zip(gsum, e2)]

    oh1 = [lanef == i for i in i1]
    oh2 = [lanef == i for i in i2]
    oh = jnp.concatenate([jnp.where(a | b, 1.0, 0.0) for a, b in zip(oh1, oh2)], axis=0)
    cnt = jnp.sum(oh, axis=0, keepdims=True)
    ei = lax.broadcasted_iota(jnp.int32, (LANES, LANES), 0)
    ej = lax.broadcasted_iota(jnp.int32, (LANES, LANES), 1)
    start = _dot(jnp.broadcast_to(cnt, (SUBLANES, LANES)), (ei < ej).astype(F32), HI)[0:1]
    where = _dot(tri_ref[...], oh.astype(BF16)) + start
    wheres = [where[k * rows:(k + 1) * rows] for k in range(len(hs))]
    pos1 = [rsum(jnp.where(o, w, 0.0)) for o, w in zip(oh1, wheres)]
    pos2 = [rsum(jnp.where(o, w, 0.0)) for o, w in zip(oh2, wheres)]

    run = run_ref[...]
    srow = lax.broadcasted_iota(jnp.int32, (SUBLANES, LANES), 0)
    slane = lax.broadcasted_iota(jnp.int32, (SUBLANES, LANES), 1)
    tab_ref[...] = jnp.where(srow == 0, cnt, jnp.where(srow == 1, run, jnp.where(srow == 2, start, 0.0)))
    run_ref[...] = run + cnt

    routes = []
    for g1, g2, p1, p2 in zip(gate1, gate2, pos1, pos2):
        route = jnp.where(lane == 2, g1, 0.0)
        route = jnp.where(lane == 3, g2, route)
        route = jnp.where(lane == 4, p1, route)
        routes.append(jnp.where(lane == 5, p2, route))
    route = jnp.concatenate(routes, axis=0)
    route_ref[...] = route
    pick = (slane == srow + 4).astype(F32)
    pos_ref[...] = lax.dot_general(pick, route, NT_DIMS, precision=HI, preferred_element_type=F32)


def _mix_ab_kernel(ya_ref, yb_ref, x_ref, w_ref, g_ref, b_ref, wr_ref, br_ref, tri_ref,
                   h_ref, route_ref, pos_ref, tab_ref, run_ref):
    parts = _row_parts(x_ref.shape[0])
    mixes = [_dot(ya_ref[rows, :].astype(BF16), w_ref[0:GDN_W, :]) + _dot(yb_ref[rows, :].astype(BF16), w_ref[GDN_W:, :])
             for rows in parts]
    hs = [_layer_norm(ALPHA * x_ref[rows, :] + mix, g_ref[...], b_ref[...]) for rows, mix in zip(parts, mixes)]
    for rows, h in zip(parts, hs):
        h_ref[rows, :] = h
    _route(hs, wr_ref, br_ref, tri_ref, run_ref, route_ref, pos_ref, tab_ref)


def _to_natural(view_ref, nat_ref, dil):
    rows = view_ref.shape[0]
    nlc = nat_ref.shape[0]
    for r in range(dil):
        for lc in range(nlc):
            col = (r * nlc + lc) * LANES
            nat_ref[lc, pl.ds(r, rows, stride=dil), :] = view_ref[:, col:col + LANES]
    return jnp.concatenate([nat_ref[lc] for lc in range(nlc)], axis=1)


def _mix_c_kernel(o1_ref, o4_ref, o16_ref, l1_ref, l4_ref, l16_ref, expand_ref, x_ref, w_ref, g_ref, b_ref,
                  wr_ref, br_ref, tri_ref, h_ref, route_ref, pos_ref, tab_ref, run_ref,
                  on4_ref, on16_ref, ln4_ref, ln16_ref):
    o4 = _to_natural(o4_ref, on4_ref, DILATIONS[1])
    o16 = _to_natural(o16_ref, on16_ref, DILATIONS[2])
    l2 = _to_natural(l4_ref, ln4_ref, DILATIONS[1])
    l3 = _to_natural(l16_ref, ln16_ref, DILATIONS[2])
    l1 = l1_ref[...]
    m = jnp.maximum(jnp.maximum(l1, l2), l3)
    e1, e2, e3 = jnp.exp(l1 - m), jnp.exp(l2 - m), jnp.exp(l3 - m)
    inv = 1.0 / (e1 + e2 + e3)
    ex = expand_ref[...]

    def spread(wt):
        hi, lo = _split_bf16(wt)
        return _dot(hi, ex) + _dot(lo, ex)

    o = spread(e1 * inv) * o1_ref[...] + spread(e2 * inv) * o4 + spread(e3 * inv) * o16
    parts = _row_parts(x_ref.shape[0])
    mixes = [_dot(o[rows].astype(BF16), w_ref[...]) for rows in parts]
    hs = [_layer_norm(ALPHA * x_ref[rows, :] + mix, g_ref[...], b_ref[...]) for rows, mix in zip(parts, mixes)]
    for rows, h in zip(parts, hs):
        h_ref[rows, :] = h
    _route(hs, wr_ref, br_ref, tri_ref, run_ref, route_ref, pos_ref, tab_ref)


def _router_operands(wg, bg, we, be):
    wr = jnp.zeros((D_MODEL, LANES), F32).at[:, :MOE_GROUPS].set(wg).at[:, EXPERT_LANE0:EXPERT_LANE0 + N_EXPERTS].set(we)
    wr_hi = wr.astype(BF16)
    wr_lo = (wr - wr_hi.astype(F32)).astype(BF16)
    br = jnp.zeros((1, LANES), F32).at[0, :MOE_GROUPS].set(bg).at[0, EXPERT_LANE0:EXPERT_LANE0 + N_EXPERTS].set(be)
    tri = jnp.tril(jnp.ones((ROW_TILE, ROW_TILE), F32), -1).astype(BF16)
    return jnp.concatenate([wr_hi, wr_lo], axis=1), br, tri


def _mix_call(kernel_fn, acts, x2, w_out, ln_g, ln_b, router, extra=(), scratch=()):
    n = x2.shape[0]
    tm = ROW_TILE
    wr, br, tri = _router_operands(*router)
    row = lambda i: (i, 0)
    const = lambda i: (0, 0)
    in_specs = [pl.BlockSpec((rows, a.shape[1]), row) for a, rows in acts]
    in_specs += [pl.BlockSpec(e.shape, const) for e in extra]
    in_specs += [pl.BlockSpec((tm, D_MODEL), row), pl.BlockSpec(w_out.shape, const),
                 pl.BlockSpec((1, D_MODEL), const), pl.BlockSpec((1, D_MODEL), const),
                 pl.BlockSpec((D_MODEL, 2 * LANES), const), pl.BlockSpec((1, LANES), const),
                 pl.BlockSpec((tm, tm), const)]
    return pl.pallas_call(
        kernel_fn,
        grid=(n // tm,),
        in_specs=in_specs,
        out_specs=[pl.BlockSpec((tm, D_MODEL), row), pl.BlockSpec((tm, LANES), row),
                   pl.BlockSpec((SUBLANES, tm), lambda i: (0, i)),
                   pl.BlockSpec((None, SUBLANES, LANES), lambda i: (i, 0, 0))],
        out_shape=[jax.ShapeDtypeStruct((n, D_MODEL), F32), jax.ShapeDtypeStruct((n, LANES), F32),
                   jax.ShapeDtypeStruct((SUBLANES, n), F32), jax.ShapeDtypeStruct((n // tm, SUBLANES, LANES), F32)],
        scratch_shapes=[pltpu.VMEM((1, LANES), F32), *scratch],
        compiler_params=_params("arbitrary"),
    )(*[a for a, _ in acts], *extra, x2, w_out.astype(BF16), ln_g.reshape(1, D_MODEL), ln_b.reshape(1, D_MODEL),
      wr, br, tri)


SLAB = D_MODEL // LANES // 2
U32 = jnp.uint32
HIGH_HALF = 0xFFFF0000
SEG_SIZES = tuple(1 << k for k in range(ROW_TILE.bit_length() - 1, -1, -1))
SEG_RARE = 64


def _segment_tables(tab, n_blocks):
    lanes = slice(EXPERT_LANE0, EXPERT_LANE0 + N_EXPERTS)
    cnt = tab[:, 0, lanes].astype(jnp.int32)
    run = tab[:, 1, lanes].astype(jnp.int32)
    src = tab[:, 2, lanes].astype(jnp.int32)
    counts = run[-1] + cnt[-1]
    padded = (counts + MOE_ROWS - 1) // MOE_ROWS * MOE_ROWS
    padded_end = jnp.cumsum(padded)
    dst = (padded_end - padded)[None, :] + run
    tail = jnp.arange(N_EXPERTS, dtype=jnp.int32) * MOE_ROWS + padded_end[-1]
    tail_cnt = jnp.where(tail < n_blocks * MOE_ROWS, MOE_ROWS, 0)
    cnt = jnp.concatenate([cnt, (padded - counts)[None, :], tail_cnt[None, :]], axis=0)
    src = jnp.concatenate([src, jnp.zeros((2, N_EXPERTS), jnp.int32)], axis=0)
    dst = jnp.concatenate([dst, (padded_end - padded + counts)[None, :], tail[None, :]], axis=0)
    block_start = jnp.arange(n_blocks, dtype=jnp.int32) * MOE_ROWS
    block_expert = jnp.minimum(jnp.sum((padded_end[None, :] <= block_start[:, None]).astype(jnp.int32), axis=1),
                               N_EXPERTS - 1)
    n_used = (padded_end[-1:] // MOE_ROWS).astype(jnp.int32)
    return (cnt.reshape(-1), src.reshape(-1), dst.reshape(-1)), block_expert, n_used


def _slab_rows(ref, start, size):
    return ref.at[pl.ds(pl.multiple_of(start * SLAB, SLAB), size * SLAB)]


def _for_each_segment(tabs, tile, fn):
    cnt_ref, src_ref, dst_ref = tabs

    def per_expert(e, carry):
        idx = tile * N_EXPERTS + e
        c, s, d = cnt_ref[idx], src_ref[idx], dst_ref[idx]

        def pieces(sizes):
            for size in sizes:
                @pl.when((c & size) != 0)
                def _():
                    off = c & (-2 * size)
                    fn(s + off, d + off, size)

        @pl.when(c >= SEG_RARE)
        def _():
            pieces([size for size in SEG_SIZES if size >= SEG_RARE])

        pieces([size for size in SEG_SIZES if size < SEG_RARE])
        return carry

    lax.fori_loop(0, N_EXPERTS, per_expert, 0)


def _to_slabs(val, ref):
    for lc in range(SLAB):
        hi = pltpu.bitcast(val[:, lc * LANES:(lc + 1) * LANES], U32)
        lo = pltpu.bitcast(val[:, (lc + SLAB) * LANES:(lc + SLAB + 1) * LANES], U32)
        ref[pl.ds(lc, val.shape[0], stride=SLAB), :] = (hi & jnp.uint32(HIGH_HALF)) | (lo >> 16)


def _from_slabs(ref):
    rows = ref.shape[0] // SLAB
    words = [ref[pl.ds(lc, rows, stride=SLAB), :] for lc in range(SLAB)]
    return jnp.concatenate([pltpu.bitcast(w & jnp.uint32(HIGH_HALF), F32) for w in words]
                           + [pltpu.bitcast(w << 16, F32) for w in words], axis=1)


def _dispatch_kernel(cnt_ref, src_ref, dst_ref, h_ref, pos_ref, xs_ref, sort_ref, sem):
    tile = pl.program_id(0)
    last = pl.num_programs(0) - 1
    tm = h_ref.shape[0]
    slot = tile % 2
    tabs = (cnt_ref, src_ref, dst_ref)

    def copy(k, s, d, size):
        return pltpu.make_async_copy(_slab_rows(sort_ref.at[k], s, size), _slab_rows(xs_ref, d, size), sem.at[k])

    j = lax.broadcasted_iota(jnp.int32, (2 * tm, tm), 0).astype(F32)
    pos = pos_ref[...]
    perm = jnp.where((j == pos[0:1, :]) | (j == pos[1:2, :]), 1.0, 0.0).astype(BF16)
    _to_slabs(_dot(perm, h_ref[...].astype(BF16)), sort_ref.at[slot])
    _for_each_segment(tabs, tile, lambda s, d, size: copy(slot, s, d, size).start())

    def wait_tile(k):
        pltpu.make_async_copy(sort_ref.at[k], xs_ref.at[pl.ds(0, 2 * tm * SLAB)], sem.at[k]).wait()

    @pl.when(tile > 0)
    def _():
        wait_tile(1 - slot)

    @pl.when(tile == last)
    def _():
        wait_tile(slot)
        sort_ref[slot, 0:MOE_ROWS * SLAB, :] = jnp.zeros((MOE_ROWS * SLAB, LANES), U32)
        for pseudo in (1, 2):
            _for_each_segment(tabs, tile + pseudo, lambda s, d, size: copy(slot, s, d, size).start())
        for pseudo in (1, 2):
            _for_each_segment(tabs, tile + pseudo, lambda s, d, size: copy(slot, s, d, size).wait())


def _dispatch(tabs, h, pos, n_slots):
    n = h.shape[0]
    tm = ROW_TILE
    grid_spec = pltpu.PrefetchScalarGridSpec(
        num_scalar_prefetch=3,
        grid=(n // tm,),
        in_specs=[pl.BlockSpec((tm, D_MODEL), lambda i, *_: (i, 0)), pl.BlockSpec((SUBLANES, tm), lambda i, *_: (0, i))],
        out_specs=pl.BlockSpec(memory_space=pl.ANY),
        scratch_shapes=[pltpu.VMEM((2, 2 * tm * SLAB, LANES), U32), pltpu.SemaphoreType.DMA((2,))],
    )
    return pl.pallas_call(
        _dispatch_kernel,
        grid_spec=grid_spec,
        out_shape=jax.ShapeDtypeStruct((n_slots * SLAB, LANES), U32),
        compiler_params=_params("arbitrary"),
    )(*tabs, h, pos)


def _expert_kernel(be_ref, nu_ref, xs_ref, wg_ref, wu_ref, wd_ref, ys_ref, wgb_ref, wub_ref, wdb_ref):
    i = pl.program_id(0)

    @pl.when((i == 0) | (be_ref[i] != be_ref[jnp.maximum(i - 1, 0)]))
    def _():
        wgb_ref[...] = wg_ref[...].astype(BF16)
        wub_ref[...] = wu_ref[...].astype(BF16)
        wdb_ref[...] = wd_ref[...].astype(BF16)

    @pl.when(i < nu_ref[0])
    def _():
        x = _from_slabs(xs_ref).astype(BF16)
        hg = _dot(x, wgb_ref[...])
        hu = _dot(x, wub_ref[...])
        hdn = hg * _sigmoid(hg) * hu
        _to_slabs(_dot(hdn.astype(BF16), wdb_ref[...]).astype(BF16).astype(F32), ys_ref)

    @pl.when(i >= nu_ref[0])
    def _():
        ys_ref[...] = jnp.zeros_like(ys_ref)


def _experts(block_expert, n_used, xs, w_gate, w_up, w_down, layer):
    blk = MOE_ROWS * SLAB
    row = lambda i, be, nu: (i, 0)
    used_row = lambda i, be, nu: (jnp.minimum(i, jnp.maximum(nu[0] - 1, 0)), 0)
    wsel = lambda i, be, nu: (layer, be[i], 0, 0)
    grid_spec = pltpu.PrefetchScalarGridSpec(
        num_scalar_prefetch=2,
        grid=(xs.shape[0] // blk,),
        in_specs=[pl.BlockSpec((blk, LANES), used_row),
                  pl.BlockSpec((None, None, D_MODEL, EXPERT_FF), wsel),
                  pl.BlockSpec((None, None, D_MODEL, EXPERT_FF), wsel),
                  pl.BlockSpec((None, None, EXPERT_FF, D_MODEL), wsel)],
        out_specs=pl.BlockSpec((blk, LANES), row),
        scratch_shapes=[pltpu.VMEM((D_MODEL, EXPERT_FF), BF16), pltpu.VMEM((D_MODEL, EXPERT_FF), BF16),
                        pltpu.VMEM((EXPERT_FF, D_MODEL), BF16)],
    )
    return pl.pallas_call(
        _expert_kernel,
        grid_spec=grid_spec,
        out_shape=jax.ShapeDtypeStruct(xs.shape, U32),
        compiler_params=_params("arbitrary"),
    )(block_expert, n_used, xs, w_gate, w_up, w_down)


def _combine_kernel(cnt_ref, src_ref, dst_ref, h_ref, route_ref, ys_ref, g_ref, b_ref, o_ref, sort_ref, sem):
    tile = pl.program_id(0)
    last = pl.num_programs(0) - 1
    tm = h_ref.shape[0]
    slot = tile % 2
    tabs = (cnt_ref, src_ref, dst_ref)

    def copy(k, s, d, size):
        return pltpu.make_async_copy(_slab_rows(ys_ref, d, size), _slab_rows(sort_ref.at[k], s, size), sem.at[k])

    @pl.when(tile == 0)
    def _():
        _for_each_segment(tabs, tile, lambda s, d, size: copy(slot, s, d, size).start())

    @pl.when(tile < last)
    def _():
        _for_each_segment(tabs, tile + 1, lambda s, d, size: copy(1 - slot, s, d, size).start())

    pltpu.make_async_copy(ys_ref.at[pl.ds(0, 2 * tm * SLAB)], sort_ref.at[slot], sem.at[slot]).wait()
    ysorted = _from_slabs(sort_ref.at[slot]).astype(BF16)
    r = route_ref[...]
    j = lax.broadcasted_iota(jnp.int32, (2 * tm, 2 * tm), 1).astype(F32)
    where = jnp.concatenate([r[:, 4:5], r[:, 5:6]], axis=0)
    picked = _dot(jnp.where(j == where, 1.0, 0.0).astype(BF16), ysorted)
    ffn = picked[:tm] * r[:, 2:3] + picked[tm:] * r[:, 3:4]
    o_ref[...] = _layer_norm(ALPHA * h_ref[...] + ffn, g_ref[...], b_ref[...])


def _combine(tabs, h, route, ys, ln_g, ln_b):
    n = h.shape[0]
    tm = ROW_TILE
    row = lambda i, *_: (i, 0)
    const = lambda i, *_: (0, 0)
    grid_spec = pltpu.PrefetchScalarGridSpec(
        num_scalar_prefetch=3,
        grid=(n // tm,),
        in_specs=[pl.BlockSpec((tm, D_MODEL), row), pl.BlockSpec((tm, LANES), row), pl.BlockSpec(memory_space=pl.ANY),
                  pl.BlockSpec((1, D_MODEL), const), pl.BlockSpec((1, D_MODEL), const)],
        out_specs=pl.BlockSpec((tm, D_MODEL), row),
        scratch_shapes=[pltpu.VMEM((2, 2 * tm * SLAB, LANES), U32), pltpu.SemaphoreType.DMA((2,))],
    )
    return pl.pallas_call(
        _combine_kernel,
        grid_spec=grid_spec,
        out_shape=jax.ShapeDtypeStruct((n, D_MODEL), F32),
        compiler_params=_params("arbitrary"),
    )(*tabs, h, route, ys, ln_g.reshape(1, D_MODEL), ln_b.reshape(1, D_MODEL))


def _moe(h, route, pos, tab, w_gate, w_up, w_down, ln_g, ln_b, layer):
    n = h.shape[0]
    n_slots = n * 2 + N_EXPERTS * MOE_ROWS
    tabs, block_expert, n_used = _segment_tables(tab, n_slots // MOE_ROWS)
    xs = _dispatch(tabs, h, pos, n_slots)
    ys = _experts(block_expert, n_used, xs, w_gate, w_up, w_down, layer)
    return _combine(tabs, h, route, ys, ln_g, ln_b)


def _qkv_kernel(x_ref, w_ref, o1_ref, o4_ref, o16_ref, acc_ref):
    acc = _dot(x_ref[...].astype(BF16), w_ref[...])
    o1_ref[...] = acc.astype(BF16)
    nlc = acc_ref.shape[0]
    for lc in range(nlc):
        acc_ref[lc] = acc[:, lc * LANES:(lc + 1) * LANES]
    for o_ref, dil in ((o4_ref, DILATIONS[1]), (o16_ref, DILATIONS[2])):
        rows = o_ref.shape[0]
        for r in range(dil):
            for lc in range(nlc):
                col = (r * nlc + lc) * LANES
                o_ref[:, col:col + LANES] = acc_ref[lc, pl.ds(r, rows, stride=dil), :].astype(BF16)


def _qkv_proj(h, w_qkv):
    n = h.shape[0]
    tm = ROW_TILE
    wd = 3 * D_MODEL
    d4, d16 = DILATIONS[1], DILATIONS[2]
    row = lambda i: (i, 0)
    return pl.pallas_call(
        _qkv_kernel,
        grid=(n // tm,),
        in_specs=[pl.BlockSpec((tm, D_MODEL), row), pl.BlockSpec((D_MODEL, wd), lambda i: (0, 0))],
        out_specs=[pl.BlockSpec((tm, wd), row), pl.BlockSpec((tm // d4, d4 * wd), row),
                   pl.BlockSpec((tm // d16, d16 * wd), row)],
        out_shape=[jax.ShapeDtypeStruct((n, wd), BF16), jax.ShapeDtypeStruct((n // d4, d4 * wd), BF16),
                   jax.ShapeDtypeStruct((n // d16, d16 * wd), BF16)],
        scratch_shapes=[pltpu.VMEM((wd // LANES, tm, LANES), F32)],
        compiler_params=_params("arbitrary"),
    )(h, w_qkv.astype(BF16))


ATTN_QBLOCKS = 2


def _attn_kernel(q_ref, kc_ref, vc_ref, kp_ref, vp_ref, o_ref, lse_ref):
    t = ATTN_STEPS
    qi = lax.broadcasted_iota(jnp.int32, (t, 2 * t), 0)
    kj = lax.broadcasted_iota(jnp.int32, (t, 2 * t), 1)
    dist = t + qi - kj
    band = (dist >= 0) & (dist <= t)
    lane = lax.broadcasted_iota(jnp.int32, (t, LANES), 1)
    upper = lane >= ATTN_DIM
    ones_kv = jnp.ones((2 * t, LANES), BF16)
    lanes_per_head = LANES // ATTN_HEADS
    scale = jnp.asarray(ATTN_DIM ** -0.5, BF16)
    for qb in range(ATTN_QBLOCKS):
        rows = slice(qb * t, (qb + 1) * t)
        valid = band & ((kj >= t) | (pl.program_id(2) > 0)) if qb == 0 else band
        scores, vals = [], []
        for hp in range(ATTN_HEADS // 2):
            cs = slice(hp * LANES, (hp + 1) * LANES)
            q2 = q_ref[rows, cs] * scale
            if qb == 0:
                k2 = jnp.concatenate([kp_ref[:, cs], kc_ref[0:t, cs]], axis=0)
                v2 = jnp.concatenate([vp_ref[:, cs], vc_ref[0:t, cs]], axis=0)
            else:
                k2 = kc_ref[(qb - 1) * t:(qb + 1) * t, cs]
                v2 = vc_ref[(qb - 1) * t:(qb + 1) * t, cs]
            vals.append(jnp.concatenate([v2, ones_kv], axis=1))
            for sub in range(2):
                mine = upper if sub else jnp.logical_not(upper)
                qm = jnp.where(mine, q2, jnp.zeros_like(q2))
                scores.append(lax.dot_general(qm, k2, NT_DIMS, preferred_element_type=F32))
        maxes, probs = [], []
        for s in scores:
            s = jnp.where(valid, s, NEG)
            m = jnp.max(s, axis=-1, keepdims=True)
            maxes.append(m)
            probs.append(jnp.exp(s - m).astype(BF16))
        both = [_dot(p, vals[head // 2]) for head, p in enumerate(probs)]
        nums = [r[:, :LANES] for r in both]
        dens = [r[:, LANES:] for r in both]
        lse_blk = jnp.zeros((t, LANES), F32)
        for head in range(ATTN_HEADS):
            lse_blk = jnp.where(lane // lanes_per_head == head, maxes[head] + jnp.log(dens[head]), lse_blk)
        for hp in range(ATTN_HEADS // 2):
            lo, hi = 2 * hp, 2 * hp + 1
            o_ref[rows, hp * LANES:(hp + 1) * LANES] = jnp.where(upper, nums[hi] / dens[hi], nums[lo] / dens[lo])
        lse_ref[rows, :] = lse_blk


def _attn_branch(qkv_view, bsz, seq, dil):
    length = seq // dil
    t = ATTN_STEPS
    tq = t * ATTN_QBLOCKS
    qkv_v = qkv_view.reshape(bsz, length, dil * 3 * D_MODEL)
    cur = lambda part: (lambda b, r, i: (b, i, 3 * r + part))
    prev = lambda part: (lambda b, r, i: (b, jnp.maximum(i * ATTN_QBLOCKS - 1, 0), 3 * r + part))
    blk = (None, tq, D_MODEL)
    pblk = (None, t, D_MODEL)
    o, lse = pl.pallas_call(
        _attn_kernel,
        grid=(bsz, dil, length // tq),
        in_specs=[pl.BlockSpec(blk, cur(0)), pl.BlockSpec(blk, cur(1)), pl.BlockSpec(blk, cur(2)),
                  pl.BlockSpec(pblk, prev(1)), pl.BlockSpec(pblk, prev(2))],
        out_specs=[pl.BlockSpec(blk, lambda b, r, i: (b, i, r)), pl.BlockSpec((None, tq, LANES), lambda b, r, i: (b, i, r))],
        out_shape=[jax.ShapeDtypeStruct((bsz, length, dil * D_MODEL), F32),
                   jax.ShapeDtypeStruct((bsz, length, dil * LANES), F32)],
        compiler_params=_params("arbitrary", "arbitrary", "arbitrary"),
    )(qkv_v, qkv_v, qkv_v, qkv_v, qkv_v)
    return o.reshape(bsz * length, dil * D_MODEL), lse.reshape(bsz * length, dil * LANES)


def kernel(x, w_in_ab, conv_qkv, gdn_a_log, gdn_dt_bias, gdn_norm, s5_lam_re, s5_lam_im, s5_log_dt, s5_b_re, s5_b_im, s5_c_re, s5_c_im, s5_d, s5_w_glu, w_out_ab, w_qkv_c, w_out_c, ln_mix_g, ln_mix_b, router_group_w, router_group_b, router_expert_w, router_expert_b, moe_w_gate, moe_w_up, moe_w_down, ln_ffn_g, ln_ffn_b):
    bsz, seq, d = x.shape
    n = bsz * seq
    x2 = x.reshape(n, d)
    tm = ROW_TILE

    w_in = w_in_ab[0]
    nq = 4 * GDN_W
    ba_cols = jnp.zeros((d, LANES), F32).at[:, :2 * GDN_HEADS].set(w_in[:, nq:nq + 2 * GDN_HEADS])
    w_cat = jnp.concatenate([w_in[:, :nq], w_in[:, nq + 2 * GDN_HEADS:], ba_cols], axis=1).astype(BF16)
    qkv, z, u, ba = _inproj(x2, w_cat)
    ya = _gdn(qkv, z, ba, conv_qkv[0], gdn_a_log[0], gdn_dt_bias[0], gdn_norm[0], bsz, seq)
    yb = _s5(u.reshape(bsz, seq, S5_W), s5_lam_re[0], s5_lam_im[0], s5_log_dt[0], s5_b_re[0], s5_b_im[0],
             s5_c_re[0], s5_c_im[0], s5_d[0], s5_w_glu[0]).reshape(n, S5_W)
    router0 = (router_group_w[0], router_group_b[0], router_expert_w[0], router_expert_b[0])
    h, route, pos, tab = _mix_call(_mix_ab_kernel, ((ya, tm), (yb, tm)), x2, w_out_ab[0], ln_mix_g[0], ln_mix_b[0], router0)
    h = _moe(h, route, pos, tab, moe_w_gate, moe_w_up, moe_w_down, ln_ffn_g[0], ln_ffn_b[0], 0)

    acts = []
    for qkv_view, dil in zip(_qkv_proj(h, w_qkv_c[0]), DILATIONS):
        acts.append(_attn_branch(qkv_view, bsz, seq, dil))
    lanes_per_head = LANES // ATTN_HEADS
    expand = (jnp.arange(LANES)[:, None] == (jnp.arange(D_MODEL)[None, :] // ATTN_DIM) * lanes_per_head).astype(BF16)
    router1 = (router_group_w[1], router_group_b[1], router_expert_w[1], router_expert_b[1])
    d4, d16 = DILATIONS[1], DILATIONS[2]
    (o1, l1), (o4, l4), (o16, l16) = acts
    h, route, pos, tab = _mix_call(
        _mix_c_kernel, ((o1, tm), (o4, tm // d4), (o16, tm // d16), (l1, tm), (l4, tm // d4), (l16, tm // d16)),
        h, w_out_c[0], ln_mix_g[1], ln_mix_b[1], router1, extra=(expand,),
        scratch=(pltpu.VMEM((D_MODEL // LANES, tm, LANES), F32), pltpu.VMEM((D_MODEL // LANES, tm, LANES), F32),
                 pltpu.VMEM((1, tm, LANES), F32), pltpu.VMEM((1, tm, LANES), F32)))
    h = _moe(h, route, pos, tab, moe_w_gate, moe_w_up, moe_w_down, ln_ffn_g[1], ln_ffn_b[1], 1)
    return h.reshape(bsz, seq, d)
```

```python
import functools
import math

import jax
import jax.numpy as jnp
from jax import lax
from jax.experimental import pallas as pl
from jax.experimental.pallas import tpu as pltpu

F32 = jnp.float32
BF16 = jnp.bfloat16
HI = lax.Precision.HIGHEST

D_MODEL = 1024
DEPTH = 2
ALPHA = (2 * DEPTH) ** 0.25
LN_EPS = 1e-5

GDN_HEADS = 4
GDN_DIM = 128
GDN_W = GDN_HEADS * GDN_DIM
CONV_K = 4
GDN_CHUNK = 64
GDN_BATCH = 4

S5_W = 512
S5_GROUP_CH = 16
S5_GROUPS = 32
S5_STATE = 64
S5_SUPER = 4
S5_SUPER_CH = S5_W // S5_SUPER
S5_SUPER_ST = S5_GROUPS * S5_STATE // S5_SUPER

ATTN_HEADS = 16
ATTN_DIM = 64
ATTN_STEPS = 128
DILATIONS = (1, 4, 16)

MOE_GROUPS = 4
EXPERTS_PER_GROUP = 8
N_EXPERTS = 32
EXPERT_FF = 512
EXPERT_LANE0 = MOE_GROUPS

LANES = 128
SUBLANES = 8
VMEM_LIMIT = 56 * 1024 * 1024

ROW_TILE = 512
S5_STEPS = 128
MOE_ROWS = 512
GATHER_TILE = 256

NEG = -1e30
NT_DIMS = (((1,), (1,)), ((), ()))
TN_DIMS = (((0,), (0,)), ((), ()))


def _params(*sem):
    return pltpu.CompilerParams(dimension_semantics=sem, vmem_limit_bytes=VMEM_LIMIT)


def _sigmoid(x):
    return 1.0 / (1.0 + jnp.exp(-x))


def _softplus(x):
    return jnp.maximum(x, 0.0) + jnp.log(1.0 + jnp.exp(-jnp.abs(x)))


def _layer_norm(r, g, b):
    mu = jnp.mean(r, axis=-1, keepdims=True)
    c = r - mu
    var = jnp.mean(c * c, axis=-1, keepdims=True)
    return c * lax.rsqrt(var + LN_EPS) * g + b


def _dot(a, b, precision=None):
    return jnp.dot(a, b, precision=precision, preferred_element_type=F32)


def _mm(a, b, dims=None):
    a, b = a.astype(BF16), b.astype(BF16)
    if dims is None:
        return jnp.dot(a, b, preferred_element_type=F32)
    return lax.dot_general(a, b, dims, preferred_element_type=F32)


def _split_bf16(x):
    hi = x.astype(BF16)
    return hi, (x - hi.astype(F32)).astype(BF16)


IN_COLS = 3 * GDN_W + GDN_W + S5_W + LANES


def _inproj_kernel(x_ref, w_ref, qkv_ref, z_ref, u_ref, ba_ref):
    x = x_ref[...].astype(BF16)
    qkv_ref[...] = _dot(x, w_ref[:, 0:1536])
    z_ref[...] = _dot(x, w_ref[:, 1536:2048])
    u_ref[...] = _dot(x, w_ref[:, 2048:2560])
    ba_ref[...] = _dot(x, w_ref[:, 2560:2688])


def _inproj(x2, w_cat):
    n = x2.shape[0]
    tm = ROW_TILE
    row = lambda i: (i, 0)
    return pl.pallas_call(
        _inproj_kernel,
        grid=(n // tm,),
        in_specs=[pl.BlockSpec((tm, D_MODEL), row), pl.BlockSpec((D_MODEL, IN_COLS), lambda i: (0, 0))],
        out_specs=[pl.BlockSpec((tm, 3 * GDN_W), row), pl.BlockSpec((tm, GDN_W), row),
                   pl.BlockSpec((tm, S5_W), row), pl.BlockSpec((tm, LANES), row)],
        out_shape=[jax.ShapeDtypeStruct((n, 3 * GDN_W), F32), jax.ShapeDtypeStruct((n, GDN_W), F32),
                   jax.ShapeDtypeStruct((n, S5_W), F32), jax.ShapeDtypeStruct((n, LANES), F32)],
        compiler_params=_params("arbitrary"),
    )(x2, w_cat)


def _l2norm(t):
    return t * lax.rsqrt(jnp.sum(t * t, axis=-1, keepdims=True) + 1e-6)


def _short_conv(x, taps):
    out = x * taps[CONV_K - 1:CONV_K, :]
    for j in range(1, CONV_K):
        out = out + pltpu.roll(x, j, axis=0) * taps[CONV_K - 1 - j:CONV_K - j, :]
    return out


def _each(fn, *lists):
    return [fn(*args) for args in zip(*lists)]


def _unit_lower_inverse(lows, same16, cross32, cross64, eye):
    ps = [jnp.where(same16, low, 0.0) for low in lows]
    ts = [eye - p for p in ps]
    for _ in range(3):
        ps = _each(lambda p: _mm(p, p), ps)
        ts = _each(lambda t, p: _mm(t, eye + p), ts, ps)
    for cross in (cross32, cross64):
        tc = _each(lambda t, low: _mm(t, jnp.where(cross, low, 0.0)), ts, lows)
        ts = _each(lambda t, x: t - _mm(x, t), ts, tc)
    return ts


def _gdn_kernel(qkv_ref, z_ref, ba_ref, convw_ref, alog_ref, dtb_ref, normw_ref, tri_ref, o_ref,
                state_ref, tail_ref):
    c = GDN_CHUNK

    @pl.when(pl.program_id(1) == 0)
    def _():
        state_ref[...] = jnp.zeros_like(state_ref)
        tail_ref[...] = jnp.zeros_like(tail_ref)

    w = convw_ref[...]
    ri = lax.broadcasted_iota(jnp.int32, (c, c), 0)
    ci = lax.broadcasted_iota(jnp.int32, (c, c), 1)
    causal = ri >= ci
    strict = ri > ci
    eye = (ri == ci).astype(F32)
    same16 = (ri // 16) == (ci // 16)
    same32 = (ri // 32) == (ci // 32)
    cross32 = same32 & jnp.logical_not(same16)
    cross64 = jnp.logical_not(same32)
    normw = normw_ref[...]
    a_neg = -jnp.exp(alog_ref[...])
    head_rows = (lax.broadcasted_iota(jnp.int32, (SUBLANES, LANES), 1)
                 == lax.broadcasted_iota(jnp.int32, (SUBLANES, LANES), 0) + GDN_HEADS).astype(F32)

    chains, qs, ks, kbs, vbs, kws, decays, egcs, kdecs, glasts = [], [], [], [], [], [], [], [], [], []
    for bi in range(GDN_BATCH):
        x = qkv_ref[bi]
        head = _short_conv(jnp.concatenate([tail_ref[bi], x[:SUBLANES]], axis=0), w)[SUBLANES:]
        conv = jnp.concatenate([head, _short_conv(x, w)[SUBLANES:]], axis=0)
        tail_ref[bi] = x[c - SUBLANES:, :]
        qkv = conv * _sigmoid(conv)

        ba = ba_ref[bi]
        beta_all = _sigmoid(ba)
        g_all = a_neg * _softplus(ba + dtb_ref[...])
        gc_all = _dot(tri_ref[...], g_all, HI)
        egc_all = jnp.exp(gc_all)
        gc_rows = lax.dot_general(head_rows, gc_all, NT_DIMS, precision=HI, preferred_element_type=F32)

        for h in range(GDN_HEADS):
            q = _l2norm(qkv[:, h * GDN_DIM:(h + 1) * GDN_DIM]) * (GDN_DIM ** -0.5)
            k = _l2norm(qkv[:, GDN_W + h * GDN_DIM:GDN_W + (h + 1) * GDN_DIM])
            v = qkv[:, 2 * GDN_W + h * GDN_DIM:2 * GDN_W + (h + 1) * GDN_DIM]
            gl = GDN_HEADS + h
            beta = beta_all[:, h:h + 1]
            gc = gc_all[:, gl:gl + 1]
            egc = egc_all[:, gl:gl + 1]
            gc_last = gc_all[c - 1:c, gl:gl + 1]
            kb = k * beta
            chains.append((bi, h))
            qs.append(q)
            ks.append(k)
            kbs.append(kb)
            vbs.append(v * beta)
            kws.append(kb * egc)
            decays.append(jnp.exp(jnp.where(causal, gc - gc_rows[h:h + 1, :], NEG)))
            egcs.append(egc)
            kdecs.append(k * jnp.exp(gc_last - gc))
            glasts.append(jnp.exp(gc_last))

    kk = _each(lambda kb, q, k: _mm(jnp.concatenate([kb, q], axis=0), k, NT_DIMS), kbs, qs, ks)
    lows = _each(lambda a, dec: jnp.where(strict, a[:c] * dec, 0.0), kk, decays)
    qks = _each(lambda a, dec: jnp.where(causal, a[c:] * dec, 0.0), kk, decays)
    ts = _unit_lower_inverse(lows, same16, cross32, cross64, eye)
    sols = _each(lambda t, vb, kw: _mm(t, jnp.concatenate([vb, kw], axis=1)), ts, vbs, kws)
    states = [state_ref[bi, h] for bi, h in chains]
    wss = _each(lambda sol, q, egc, s: _mm(jnp.concatenate([sol[:, GDN_DIM:], q * egc], axis=0), s),
                sols, qs, egcs, states)
    v_news = _each(lambda sol, ws: sol[:, :GDN_DIM] - ws[:c], sols, wss)
    outs = _each(lambda ws, qk, v_new: ws[c:] + _mm(qk, v_new), wss, qks, v_news)
    upds = _each(lambda kd, v_new: _mm(kd, v_new, TN_DIMS), kdecs, v_news)
    for (bi, h), s, gl, upd, o in zip(chains, states, glasts, upds, outs):
        hs = slice(h * GDN_DIM, (h + 1) * GDN_DIM)
        state_ref[bi, h] = s * gl + upd
        zh = z_ref[bi, :, hs]
        o_ref[bi, :, hs] = o * lax.rsqrt(jnp.mean(o * o, axis=-1, keepdims=True) + 1e-6) * normw * (zh * _sigmoid(zh))


def _gdn(qkv, z, ba, conv_w, a_log, dt_bias, norm_w, bsz, seq):
    c = GDN_CHUNK
    nb = GDN_BATCH
    convw = jnp.zeros((SUBLANES, 3 * GDN_W), F32).at[:CONV_K].set(conv_w)
    alog = jnp.zeros((1, LANES), F32).at[0, GDN_HEADS:2 * GDN_HEADS].set(a_log)
    dtb = jnp.zeros((1, LANES), F32).at[0, GDN_HEADS:2 * GDN_HEADS].set(dt_bias)
    tri = jnp.tril(jnp.ones((c, c), F32))
    row = lambda b, i: (b, i, 0)
    const = lambda b, i: (0, 0)
    out = pl.pallas_call(
        _gdn_kernel,
        grid=(bsz // nb, seq // c),
        in_specs=[pl.BlockSpec((nb, c, 3 * GDN_W), row), pl.BlockSpec((nb, c, GDN_W), row),
                  pl.BlockSpec((nb, c, LANES), row),
                  pl.BlockSpec((SUBLANES, 3 * GDN_W), const), pl.BlockSpec((1, LANES), const),
                  pl.BlockSpec((1, LANES), const), pl.BlockSpec((1, GDN_DIM), const), pl.BlockSpec((c, c), const)],
        out_specs=pl.BlockSpec((nb, c, GDN_W), row),
        out_shape=jax.ShapeDtypeStruct((bsz, seq, GDN_W), F32),
        scratch_shapes=[pltpu.VMEM((nb, GDN_HEADS, GDN_DIM, GDN_DIM), F32),
                        pltpu.VMEM((nb, SUBLANES, 3 * GDN_W), F32)],
        compiler_params=_params("arbitrary", "arbitrary"),
    )(qkv.reshape(bsz, seq, 3 * GDN_W), z.reshape(bsz, seq, GDN_W), ba.reshape(bsz, seq, LANES),
      convw, alog, dtb, norm_w.reshape(1, GDN_DIM), tri)
    return out.reshape(bsz * seq, GDN_W)


def _gelu_tanh(x):
    return x * (0.5 * (1.0 + jnp.tanh(math.sqrt(2.0 / math.pi) * (x + 0.044715 * (x * x * x)))))


def _s5_kernel(u_ref, wb_ref, a_ref, wc_ref, d_ref, wglu_ref, y_ref, utb_ref, ytb_ref, *rest, steps):
    bu_refs, h_ref = rest[:-1], rest[-1]

    @pl.when(pl.program_id(0) == 0)
    def _():
        h_ref[...] = jnp.zeros_like(h_ref)

    nb = h_ref.shape[0]
    nlc = S5_W // LANES
    for b in range(nb):
        for lc in range(nlc):
            utb_ref[lc, pl.ds(b, steps, stride=nb), :] = u_ref[b, :, lc * LANES:(lc + 1) * LANES]
    u = jnp.concatenate([utb_ref[lc] for lc in range(nlc)], axis=1)
    ub = u.astype(BF16)
    st = S5_SUPER_ST
    for j in range(S5_SUPER):
        bu_refs[j][...] = _dot(ub[:, S5_SUPER_CH * j:S5_SUPER_CH * (j + 1)], wb_ref[j])

    ys = []
    for j in range(S5_SUPER):
        bu_ref = bu_refs[j]
        re = slice(2 * st * j, 2 * st * j + st)
        im = slice(2 * st * j + st, 2 * st * (j + 1))
        ar = jnp.broadcast_to(a_ref[0:1, st * j:st * (j + 1)], (nb, st))
        ai = jnp.broadcast_to(a_ref[1:2, st * j:st * (j + 1)], (nb, st))
        hr, hi = h_ref[:, re], h_ref[:, im]
        for t in range(steps):
            rows = slice(t * nb, (t + 1) * nb)
            hr, hi = (ar * hr - ai * hi + bu_ref[rows, 0:st], ar * hi + ai * hr + bu_ref[rows, st:2 * st])
            bu_ref[rows, 0:st] = hr
            bu_ref[rows, st:2 * st] = hi
        h_ref[:, re] = hr
        h_ref[:, im] = hi
        ys.append(_dot(bu_ref[...].astype(BF16), wc_ref[j]))

    y = jnp.concatenate(ys, axis=1)
    y = _gelu_tanh(y + d_ref[...] * u)
    y = y * _sigmoid(_dot(y.astype(BF16), wglu_ref[...]))
    for lc in range(nlc):
        ytb_ref[lc] = y[:, lc * LANES:(lc + 1) * LANES]
    for b in range(nb):
        for lc in range(nlc):
            y_ref[b, :, lc * LANES:(lc + 1) * LANES] = ytb_ref[lc, pl.ds(b, steps, stride=nb), :]


def _s5(u3, lam_re, lam_im, log_dt, b_re, b_im, c_re, c_im, d_skip, w_glu):
    bsz, seq, _ = u3.shape
    dt = jnp.exp(log_dt)[:, None]
    mag = jnp.exp(lam_re * dt)
    a_re, a_im = mag * jnp.cos(lam_im * dt), mag * jnp.sin(lam_im * dt)
    den = lam_re * lam_re + lam_im * lam_im
    nr, ni = a_re - 1.0, a_im
    cr = (nr * lam_re + ni * lam_im) / den
    ci = (ni * lam_re - nr * lam_im) / den
    bb_re = cr[..., None] * b_re - ci[..., None] * b_im
    bb_im = cr[..., None] * b_im + ci[..., None] * b_re
    gps = S5_GROUPS // S5_SUPER
    eye = jnp.eye(gps, dtype=F32)

    def in_blockdiag(t):
        t = t.reshape(S5_SUPER, gps, S5_STATE, S5_GROUP_CH)
        return jnp.einsum('jgph,gk->jghkp', t, eye).reshape(S5_SUPER, S5_SUPER_CH, S5_SUPER_ST)

    def out_blockdiag(t):
        t = t.reshape(S5_SUPER, gps, S5_GROUP_CH, S5_STATE)
        return jnp.einsum('jgkp,gm->jgpmk', t, eye).reshape(S5_SUPER, S5_SUPER_ST, S5_SUPER_CH)

    wb = jnp.concatenate([in_blockdiag(bb_re), in_blockdiag(bb_im)], axis=2).astype(BF16)
    wc = jnp.concatenate([out_blockdiag(c_re), -out_blockdiag(c_im)], axis=1).astype(BF16)
    nstate = S5_GROUPS * S5_STATE
    a = jnp.stack([a_re.reshape(nstate), a_im.reshape(nstate)], axis=0)

    steps = S5_STEPS
    rows = steps * bsz
    blk = lambda i: (0, i, 0)
    const2 = lambda i: (0, 0)
    const3 = lambda i: (0, 0, 0)
    return pl.pallas_call(
        functools.partial(_s5_kernel, steps=steps),
        grid=(seq // steps,),
        in_specs=[pl.BlockSpec((bsz, steps, S5_W), blk), pl.BlockSpec(wb.shape, const3),
                  pl.BlockSpec((2, nstate), const2), pl.BlockSpec(wc.shape, const3),
                  pl.BlockSpec((1, S5_W), const2), pl.BlockSpec((S5_W, S5_W), const2)],
        out_specs=pl.BlockSpec((bsz, steps, S5_W), blk),
        out_shape=jax.ShapeDtypeStruct((bsz, seq, S5_W), F32),
        scratch_shapes=[pltpu.VMEM((S5_W // LANES, rows, LANES), F32), pltpu.VMEM((S5_W // LANES, rows, LANES), F32),
                        *[pltpu.VMEM((rows, 2 * S5_SUPER_ST), F32) for _ in range(S5_SUPER)],
                        pltpu.VMEM((bsz, 2 * nstate), F32)],
        compiler_params=_params("arbitrary"),
    )(u3, wb, a, wc, d_skip.reshape(1, S5_W), w_glu.astype(BF16))


ROUTE_PARTS = 2


def _row_parts(tm):
    rows = tm // ROUTE_PARTS
    return [slice(k * rows, (k + 1) * rows) for k in range(ROUTE_PARTS)]


def _route(hs, wr_ref, br_ref, tri_ref, run_ref, route_ref, pos_ref, tab_ref):
    @pl.when(pl.program_id(0) == 0)
    def _():
        run_ref[...] = jnp.zeros_like(run_ref)

    rows = hs[0].shape[0]
    splits = [_split_bf16(h) for h in hs]
    both = [_dot(hi, wr_ref[...]) for hi, _ in splits]
    cross = [_dot(lo, wr_ref[:, :LANES]) for _, lo in splits]
    logits = [b[:, :LANES] + b[:, LANES:] + c + br_ref[...] for b, c in zip(both, cross)]
    lane = lax.broadcasted_iota(jnp.int32, (rows, LANES), 1)
    lanef = lane.astype(F32)
    big = float(LANES)

    def rmax(x):
        return jnp.max(x, axis=-1, keepdims=True)

    def rmin(x):
        return jnp.min(x, axis=-1, keepdims=True)

    def rsum(x):
        return jnp.sum(x, axis=-1, keepdims=True)

    gmask = lane < MOE_GROUPS
    gmax = [rmax(jnp.where(gmask, l, NEG)) for l in logits]
    gsum = [rsum(jnp.where(gmask, jnp.exp(l - m), 0.0)) for l, m in zip(logits, gmax)]
    gidx = [rmin(jnp.where(gmask & (l == m), lanef, big)) for l, m in zip(logits, gmax)]

    lows = [EXPERT_LANE0 + EXPERTS_PER_GROUP * g for g in gidx]
    emask = [(lanef >= lo) & (lanef < lo + EXPERTS_PER_GROUP) for lo in lows]
    el = [jnp.where(em, l, NEG) for em, l in zip(emask, logits)]
    t1 = [rmax(e) for e in el]
    i1 = [rmin(jnp.where(em & (e == t), lanef, big)) for em, e, t in zip(emask, el, t1)]
    emask2 = [em & (lanef != i) for em, i in zip(emask, i1)]
    el2 = [jnp.where(em, l, NEG) for em, l in zip(emask2, logits)]
    t2 = [rmax(e) for e in el2]
    i2 = [rmin(jnp.where(em & (e == t), lanef, big)) for em, e, t in zip(emask2, el2, t2)]
    e2 = [jnp.exp(b - a) for a, b in zip(t1, t2)]
    gate1 = [(1.0 / gs) / (1.0 + e) for gs, e in zip(gsum, e2)]
    gate2 = [(1.0 / gs) * e / (1.0 + e) for gs, e in zip(gsum, e2)]

    oh1 = [lanef == i for i in i1]
    oh2 = [lanef == i for i in i2]
    oh = jnp.concatenate([jnp.where(a | b, 1.0, 0.0) for a, b in zip(oh1, oh2)], axis=0)
    cnt = jnp.sum(oh, axis=0, keepdims=True)
    ei = lax.broadcasted_iota(jnp.int32, (LANES, LANES), 0)
    ej = lax.broadcasted_iota(jnp.int32, (LANES, LANES), 1)
    start = _dot(jnp.broadcast_to(cnt, (SUBLANES, LANES)), (ei < ej).astype(F32), HI)[0:1]
    where = _dot(tri_ref[...], oh.astype(BF16)) + start
    wheres = [where[k * rows:(k + 1) * rows] for k in range(len(hs))]
    pos1 = [rsum(jnp.where(o, w, 0.0)) for o, w in zip(oh1, wheres)]
    pos2 = [rsum(jnp.where(o, w, 0.0)) for o, w in zip(oh2, wheres)]

    run = run_ref[...]
    srow = lax.broadcasted_iota(jnp.int32, (SUBLANES, LANES), 0)
    slane = lax.broadcasted_iota(jnp.int32, (SUBLANES, LANES), 1)
    tab_ref[...] = jnp.where(srow == 0, cnt, jnp.where(srow == 1, run, jnp.where(srow == 2, start, 0.0)))
    run_ref[...] = run + cnt

    routes = []
    for g1, g2, p1, p2 in zip(gate1, gate2, pos1, pos2):
        route = jnp.where(lane == 2, g1, 0.0)
        route = jnp.where(lane == 3, g2, route)
        route = jnp.where(lane == 4, p1, route)
        routes.append(jnp.where(lane == 5, p2, route))
    route = jnp.concatenate(routes, axis=0)
    route_ref[...] = route
    pick = (slane == srow + 4).astype(F32)
    pos_ref[...] = lax.dot_general(pick, route, NT_DIMS, precision=HI, preferred_element_type=F32)


def _mix_ab_kernel(ya_ref, yb_ref, x_ref, w_ref, g_ref, b_ref, wr_ref, br_ref, tri_ref,
                   h_ref, route_ref, pos_ref, tab_ref, run_ref):
    parts = _row_parts(x_ref.shape[0])
    mixes = [_dot(ya_ref[rows, :].astype(BF16), w_ref[0:GDN_W, :]) + _dot(yb_ref[rows, :].astype(BF16), w_ref[GDN_W:, :])
             for rows in parts]
    hs = [_layer_norm(ALPHA * x_ref[rows, :] + mix, g_ref[...], b_ref[...]) for rows, mix in zip(parts, mixes)]
    for rows, h in zip(parts, hs):
        h_ref[rows, :] = h
    _route(hs, wr_ref, br_ref, tri_ref, run_ref, route_ref, pos_ref, tab_ref)


def _to_natural(view_ref, nat_ref, dil):
    rows = view_ref.shape[0]
    nlc = nat_ref.shape[0]
    for r in range(dil):
        for lc in range(nlc):
            col = (r * nlc + lc) * LANES
            nat_ref[lc, pl.ds(r, rows, stride=dil), :] = view_ref[:, col:col + LANES]
    return jnp.concatenate([nat_ref[lc] for lc in range(nlc)], axis=1)


def _mix_c_kernel(o1_ref, o4_ref, o16_ref, l1_ref, l4_ref, l16_ref, expand_ref, x_ref, w_ref, g_ref, b_ref,
                  wr_ref, br_ref, tri_ref, h_ref, route_ref, pos_ref, tab_ref, run_ref,
                  on4_ref, on16_ref, ln4_ref, ln16_ref):
    o4 = _to_natural(o4_ref, on4_ref, DILATIONS[1])
    o16 = _to_natural(o16_ref, on16_ref, DILATIONS[2])
    l2 = _to_natural(l4_ref, ln4_ref, DILATIONS[1])
    l3 = _to_natural(l16_ref, ln16_ref, DILATIONS[2])
    l1 = l1_ref[...]
    m = jnp.maximum(jnp.maximum(l1, l2), l3)
    e1, e2, e3 = jnp.exp(l1 - m), jnp.exp(l2 - m), jnp.exp(l3 - m)
    inv = 1.0 / (e1 + e2 + e3)
    ex = expand_ref[...]

    def spread(wt):
        hi, lo = _split_bf16(wt)
        return _dot(hi, ex) + _dot(lo, ex)

    o = spread(e1 * inv) * o1_ref[...] + spread(e2 * inv) * o4 + spread(e3 * inv) * o16
    parts = _row_parts(x_ref.shape[0])
    mixes = [_dot(o[rows].astype(BF16), w_ref[...]) for rows in parts]
    hs = [_layer_norm(ALPHA * x_ref[rows, :] + mix, g_ref[...], b_ref[...]) for rows, mix in zip(parts, mixes)]
    for rows, h in zip(parts, hs):
        h_ref[rows, :] = h
    _route(hs, wr_ref, br_ref, tri_ref, run_ref, route_ref, pos_ref, tab_ref)


def _router_operands(wg, bg, we, be):
    wr = jnp.zeros((D_MODEL, LANES), F32).at[:, :MOE_GROUPS].set(wg).at[:, EXPERT_LANE0:EXPERT_LANE0 + N_EXPERTS].set(we)
    wr_hi = wr.astype(BF16)
    wr_lo = (wr - wr_hi.astype(F32)).astype(BF16)
    br = jnp.zeros((1, LANES), F32).at[0, :MOE_GROUPS].set(bg).at[0, EXPERT_LANE0:EXPERT_LANE0 + N_EXPERTS].set(be)
    tri = jnp.tril(jnp.ones((ROW_TILE, ROW_TILE), F32), -1).astype(BF16)
    return jnp.concatenate([wr_hi, wr_lo], axis=1), br, tri


def _mix_call(kernel_fn, acts, x2, w_out, ln_g, ln_b, router, extra=(), scratch=()):
    n = x2.shape[0]
    tm = ROW_TILE
    wr, br, tri = _router_operands(*router)
    row = lambda i: (i, 0)
    const = lambda i: (0, 0)
    in_specs = [pl.BlockSpec((rows, a.shape[1]), row) for a, rows in acts]
    in_specs += [pl.BlockSpec(e.shape, const) for e in extra]
    in_specs += [pl.BlockSpec((tm, D_MODEL), row), pl.BlockSpec(w_out.shape, const),
                 pl.BlockSpec((1, D_MODEL), const), pl.BlockSpec((1, D_MODEL), const),
                 pl.BlockSpec((D_MODEL, 2 * LANES), const), pl.BlockSpec((1, LANES), const),
                 pl.BlockSpec((tm, tm), const)]
    return pl.pallas_call(
        kernel_fn,
        grid=(n // tm,),
        in_specs=in_specs,
        out_specs=[pl.BlockSpec((tm, D_MODEL), row), pl.BlockSpec((tm, LANES), row),
                   pl.BlockSpec((SUBLANES, tm), lambda i: (0, i)),
                   pl.BlockSpec((None, SUBLANES, LANES), lambda i: (i, 0, 0))],
        out_shape=[jax.ShapeDtypeStruct((n, D_MODEL), F32), jax.ShapeDtypeStruct((n, LANES), F32),
                   jax.ShapeDtypeStruct((SUBLANES, n), F32), jax.ShapeDtypeStruct((n // tm, SUBLANES, LANES), F32)],
        scratch_shapes=[pltpu.VMEM((1, LANES), F32), *scratch],
        compiler_params=_params("arbitrary"),
    )(*[a for a, _ in acts], *extra, x2, w_out.astype(BF16), ln_g.reshape(1, D_MODEL), ln_b.reshape(1, D_MODEL),
      wr, br, tri)


SLAB = D_MODEL // LANES
SEG_SIZES = tuple(1 << k for k in range(ROW_TILE.bit_length() - 1, -1, -1))
SEG_RARE = 64


def _segment_tables(tab, n_blocks):
    lanes = slice(EXPERT_LANE0, EXPERT_LANE0 + N_EXPERTS)
    cnt = tab[:, 0, lanes].astype(jnp.int32)
    run = tab[:, 1, lanes].astype(jnp.int32)
    src = tab[:, 2, lanes].astype(jnp.int32)
    counts = run[-1] + cnt[-1]
    padded = (counts + MOE_ROWS - 1) // MOE_ROWS * MOE_ROWS
    padded_end = jnp.cumsum(padded)
    dst = (padded_end - padded)[None, :] + run
    tail = jnp.arange(N_EXPERTS, dtype=jnp.int32) * MOE_ROWS + padded_end[-1]
    tail_cnt = jnp.where(tail < n_blocks * MOE_ROWS, MOE_ROWS, 0)
    cnt = jnp.concatenate([cnt, (padded - counts)[None, :], tail_cnt[None, :]], axis=0)
    src = jnp.concatenate([src, jnp.zeros((2, N_EXPERTS), jnp.int32)], axis=0)
    dst = jnp.concatenate([dst, (padded_end - padded + counts)[None, :], tail[None, :]], axis=0)
    block_start = jnp.arange(n_blocks, dtype=jnp.int32) * MOE_ROWS
    block_expert = jnp.minimum(jnp.sum((padded_end[None, :] <= block_start[:, None]).astype(jnp.int32), axis=1),
                               N_EXPERTS - 1)
    n_used = (padded_end[-1:] // MOE_ROWS).astype(jnp.int32)
    return (cnt.reshape(-1), src.reshape(-1), dst.reshape(-1)), block_expert, n_used


def _slab_rows(ref, start, size):
    return ref.at[pl.ds(pl.multiple_of(start * SLAB, SLAB), size * SLAB)]


def _for_each_segment(tabs, tile, fn):
    cnt_ref, src_ref, dst_ref = tabs

    def per_expert(e, carry):
        idx = tile * N_EXPERTS + e
        c, s, d = cnt_ref[idx], src_ref[idx], dst_ref[idx]

        def pieces(sizes):
            for size in sizes:
                @pl.when((c & size) != 0)
                def _():
                    off = c & (-2 * size)
                    fn(s + off, d + off, size)

        @pl.when(c >= SEG_RARE)
        def _():
            pieces([size for size in SEG_SIZES if size >= SEG_RARE])

        pieces([size for size in SEG_SIZES if size < SEG_RARE])
        return carry

    lax.fori_loop(0, N_EXPERTS, per_expert, 0)


def _to_slabs(val, ref):
    for lc in range(SLAB):
        ref[pl.ds(lc, val.shape[0], stride=SLAB), :] = val[:, lc * LANES:(lc + 1) * LANES]


def _from_slabs(ref):
    rows = ref.shape[0] // SLAB
    return jnp.concatenate([ref[pl.ds(lc, rows, stride=SLAB), :] for lc in range(SLAB)], axis=1)


def _dispatch_kernel(cnt_ref, src_ref, dst_ref, h_ref, pos_ref, xs_ref, sort_ref, sem):
    tile = pl.program_id(0)
    last = pl.num_programs(0) - 1
    tm = h_ref.shape[0]
    slot = tile % 2
    tabs = (cnt_ref, src_ref, dst_ref)

    def copy(k, s, d, size):
        return pltpu.make_async_copy(_slab_rows(sort_ref.at[k], s, size), _slab_rows(xs_ref, d, size), sem.at[k])

    j = lax.broadcasted_iota(jnp.int32, (2 * tm, tm), 0).astype(F32)
    pos = pos_ref[...]
    perm = jnp.where((j == pos[0:1, :]) | (j == pos[1:2, :]), 1.0, 0.0).astype(BF16)
    _to_slabs(_dot(perm, h_ref[...].astype(BF16)), sort_ref.at[slot])
    _for_each_segment(tabs, tile, lambda s, d, size: copy(slot, s, d, size).start())

    def wait_tile(k):
        pltpu.make_async_copy(sort_ref.at[k], xs_ref.at[pl.ds(0, 2 * tm * SLAB)], sem.at[k]).wait()

    @pl.when(tile > 0)
    def _():
        wait_tile(1 - slot)

    @pl.when(tile == last)
    def _():
        wait_tile(slot)
        sort_ref[slot, 0:MOE_ROWS * SLAB, :] = jnp.zeros((MOE_ROWS * SLAB, LANES), F32)
        for pseudo in (1, 2):
            _for_each_segment(tabs, tile + pseudo, lambda s, d, size: copy(slot, s, d, size).start())
        for pseudo in (1, 2):
            _for_each_segment(tabs, tile + pseudo, lambda s, d, size: copy(slot, s, d, size).wait())


def _dispatch(tabs, h, pos, n_slots):
    n = h.shape[0]
    tm = ROW_TILE
    grid_spec = pltpu.PrefetchScalarGridSpec(
        num_scalar_prefetch=3,
        grid=(n // tm,),
        in_specs=[pl.BlockSpec((tm, D_MODEL), lambda i, *_: (i, 0)), pl.BlockSpec((SUBLANES, tm), lambda i, *_: (0, i))],
        out_specs=pl.BlockSpec(memory_space=pl.ANY),
        scratch_shapes=[pltpu.VMEM((2, 2 * tm * SLAB, LANES), F32), pltpu.SemaphoreType.DMA((2,))],
    )
    return pl.pallas_call(
        _dispatch_kernel,
        grid_spec=grid_spec,
        out_shape=jax.ShapeDtypeStruct((n_slots * SLAB, LANES), F32),
        compiler_params=_params("arbitrary"),
    )(*tabs, h, pos)


def _expert_kernel(be_ref, nu_ref, xs_ref, wg_ref, wu_ref, wd_ref, ys_ref, wgb_ref, wub_ref, wdb_ref):
    i = pl.program_id(0)

    @pl.when((i == 0) | (be_ref[i] != be_ref[jnp.maximum(i - 1, 0)]))
    def _():
        wgb_ref[...] = wg_ref[...].astype(BF16)
        wub_ref[...] = wu_ref[...].astype(BF16)
        wdb_ref[...] = wd_ref[...].astype(BF16)

    @pl.when(i < nu_ref[0])
    def _():
        x = _from_slabs(xs_ref).astype(BF16)
        hg = _dot(x, wgb_ref[...])
        hu = _dot(x, wub_ref[...])
        hdn = hg * _sigmoid(hg) * hu
        _to_slabs(_dot(hdn.astype(BF16), wdb_ref[...]).astype(BF16).astype(F32), ys_ref)

    @pl.when(i >= nu_ref[0])
    def _():
        ys_ref[...] = jnp.zeros_like(ys_ref)


def _experts(block_expert, n_used, xs, w_gate, w_up, w_down, layer):
    blk = MOE_ROWS * SLAB
    row = lambda i, be, nu: (i, 0)
    used_row = lambda i, be, nu: (jnp.minimum(i, jnp.maximum(nu[0] - 1, 0)), 0)
    wsel = lambda i, be, nu: (layer, be[i], 0, 0)
    grid_spec = pltpu.PrefetchScalarGridSpec(
        num_scalar_prefetch=2,
        grid=(xs.shape[0] // blk,),
        in_specs=[pl.BlockSpec((blk, LANES), used_row),
                  pl.BlockSpec((None, None, D_MODEL, EXPERT_FF), wsel),
                  pl.BlockSpec((None, None, D_MODEL, EXPERT_FF), wsel),
                  pl.BlockSpec((None, None, EXPERT_FF, D_MODEL), wsel)],
        out_specs=pl.BlockSpec((blk, LANES), row),
        scratch_shapes=[pltpu.VMEM((D_MODEL, EXPERT_FF), BF16), pltpu.VMEM((D_MODEL, EXPERT_FF), BF16),
                        pltpu.VMEM((EXPERT_FF, D_MODEL), BF16)],
    )
    return pl.pallas_call(
        _expert_kernel,
        grid_spec=grid_spec,
        out_shape=jax.ShapeDtypeStruct(xs.shape, F32),
        compiler_params=_params("arbitrary"),
    )(block_expert, n_used, xs, w_gate, w_up, w_down)


def _combine_kernel(cnt_ref, src_ref, dst_ref, h_ref, route_ref, ys_ref, g_ref, b_ref, o_ref, sort_ref, sem):
    tile = pl.program_id(0)
    last = pl.num_programs(0) - 1
    tm = h_ref.shape[0]
    slot = tile % 2
    tabs = (cnt_ref, src_ref, dst_ref)

    def copy(k, s, d, size):
        return pltpu.make_async_copy(_slab_rows(ys_ref, d, size), _slab_rows(sort_ref.at[k], s, size), sem.at[k])

    @pl.when(tile == 0)
    def _():
        _for_each_segment(tabs, tile, lambda s, d, size: copy(slot, s, d, size).start())

    @pl.when(tile < last)
    def _():
        _for_each_segment(tabs, tile + 1, lambda s, d, size: copy(1 - slot, s, d, size).start())

    pltpu.make_async_copy(ys_ref.at[pl.ds(0, 2 * tm * SLAB)], sort_ref.at[slot], sem.at[slot]).wait()
    ysorted = _from_slabs(sort_ref.at[slot]).astype(BF16)
    r = route_ref[...]
    j = lax.broadcasted_iota(jnp.int32, (2 * tm, 2 * tm), 1).astype(F32)
    where = jnp.concatenate([r[:, 4:5], r[:, 5:6]], axis=0)
    picked = _dot(jnp.where(j == where, 1.0, 0.0).astype(BF16), ysorted)
    ffn = picked[:tm] * r[:, 2:3] + picked[tm:] * r[:, 3:4]
    o_ref[...] = _layer_norm(ALPHA * h_ref[...] + ffn, g_ref[...], b_ref[...])


def _combine(tabs, h, route, ys, ln_g, ln_b):
    n = h.shape[0]
    tm = ROW_TILE
    row = lambda i, *_: (i, 0)
    const = lambda i, *_: (0, 0)
    grid_spec = pltpu.PrefetchScalarGridSpec(
        num_scalar_prefetch=3,
        grid=(n // tm,),
        in_specs=[pl.BlockSpec((tm, D_MODEL), row), pl.BlockSpec((tm, LANES), row), pl.BlockSpec(memory_space=pl.ANY),
                  pl.BlockSpec((1, D_MODEL), const), pl.BlockSpec((1, D_MODEL), const)],
        out_specs=pl.BlockSpec((tm, D_MODEL), row),
        scratch_shapes=[pltpu.VMEM((2, 2 * tm * SLAB, LANES), F32), pltpu.SemaphoreType.DMA((2,))],
    )
    return pl.pallas_call(
        _combine_kernel,
        grid_spec=grid_spec,
        out_shape=jax.ShapeDtypeStruct((n, D_MODEL), F32),
        compiler_params=_params("arbitrary"),
    )(*tabs, h, route, ys, ln_g.reshape(1, D_MODEL), ln_b.reshape(1, D_MODEL))


def _moe(h, route, pos, tab, w_gate, w_up, w_down, ln_g, ln_b, layer):
    n = h.shape[0]
    n_slots = n * 2 + N_EXPERTS * MOE_ROWS
    tabs, block_expert, n_used = _segment_tables(tab, n_slots // MOE_ROWS)
    xs = _dispatch(tabs, h, pos, n_slots)
    ys = _experts(block_expert, n_used, xs, w_gate, w_up, w_down, layer)
    return _combine(tabs, h, route, ys, ln_g, ln_b)


def _qkv_kernel(x_ref, w_ref, o1_ref, o4_ref, o16_ref, acc_ref):
    acc = _dot(x_ref[...].astype(BF16), w_ref[...])
    o1_ref[...] = acc.astype(BF16)
    nlc = acc_ref.shape[0]
    for lc in range(nlc):
        acc_ref[lc] = acc[:, lc * LANES:(lc + 1) * LANES]
    for o_ref, dil in ((o4_ref, DILATIONS[1]), (o16_ref, DILATIONS[2])):
        rows = o_ref.shape[0]
        for r in range(dil):
            for lc in range(nlc):
                col = (r * nlc + lc) * LANES
                o_ref[:, col:col + LANES] = acc_ref[lc, pl.ds(r, rows, stride=dil), :].astype(BF16)


def _qkv_proj(h, w_qkv):
    n = h.shape[0]
    tm = ROW_TILE
    wd = 3 * D_MODEL
    d4, d16 = DILATIONS[1], DILATIONS[2]
    row = lambda i: (i, 0)
    return pl.pallas_call(
        _qkv_kernel,
        grid=(n // tm,),
        in_specs=[pl.BlockSpec((tm, D_MODEL), row), pl.BlockSpec((D_MODEL, wd), lambda i: (0, 0))],
        out_specs=[pl.BlockSpec((tm, wd), row), pl.BlockSpec((tm // d4, d4 * wd), row),
                   pl.BlockSpec((tm // d16, d16 * wd), row)],
        out_shape=[jax.ShapeDtypeStruct((n, wd), BF16), jax.ShapeDtypeStruct((n // d4, d4 * wd), BF16),
                   jax.ShapeDtypeStruct((n // d16, d16 * wd), BF16)],
        scratch_shapes=[pltpu.VMEM((wd // LANES, tm, LANES), F32)],
        compiler_params=_params("arbitrary"),
    )(h, w_qkv.astype(BF16))


ATTN_QBLOCKS = 2


def _attn_kernel(q_ref, kc_ref, vc_ref, kp_ref, vp_ref, o_ref, lse_ref):
    t = ATTN_STEPS
    qi = lax.broadcasted_iota(jnp.int32, (t, 2 * t), 0)
    kj = lax.broadcasted_iota(jnp.int32, (t, 2 * t), 1)
    dist = t + qi - kj
    band = (dist >= 0) & (dist <= t)
    lane = lax.broadcasted_iota(jnp.int32, (t, LANES), 1)
    upper = lane >= ATTN_DIM
    ones_kv = jnp.ones((2 * t, LANES), BF16)
    lanes_per_head = LANES // ATTN_HEADS
    scale = jnp.asarray(ATTN_DIM ** -0.5, BF16)
    for qb in range(ATTN_QBLOCKS):
        rows = slice(qb * t, (qb + 1) * t)
        valid = band & ((kj >= t) | (pl.program_id(2) > 0)) if qb == 0 else band
        scores, vals = [], []
        for hp in range(ATTN_HEADS // 2):
            cs = slice(hp * LANES, (hp + 1) * LANES)
            q2 = q_ref[rows, cs] * scale
            if qb == 0:
                k2 = jnp.concatenate([kp_ref[:, cs], kc_ref[0:t, cs]], axis=0)
                v2 = jnp.concatenate([vp_ref[:, cs], vc_ref[0:t, cs]], axis=0)
            else:
                k2 = kc_ref[(qb - 1) * t:(qb + 1) * t, cs]
                v2 = vc_ref[(qb - 1) * t:(qb + 1) * t, cs]
            vals.append(jnp.concatenate([v2, ones_kv], axis=1))
            for sub in range(2):
                mine = upper if sub else jnp.logical_not(upper)
                qm = jnp.where(mine, q2, jnp.zeros_like(q2))
                scores.append(lax.dot_general(qm, k2, NT_DIMS, preferred_element_type=F32))
        maxes, probs = [], []
        for s in scores:
            s = jnp.where(valid, s, NEG)
            m = jnp.max(s, axis=-1, keepdims=True)
            maxes.append(m)
            probs.append(jnp.exp(s - m).astype(BF16))
        both = [_dot(p, vals[head // 2]) for head, p in enumerate(probs)]
        nums = [r[:, :LANES] for r in both]
        dens = [r[:, LANES:] for r in both]
        lse_blk = jnp.zeros((t, LANES), F32)
        for head in range(ATTN_HEADS):
            lse_blk = jnp.where(lane // lanes_per_head == head, maxes[head] + jnp.log(dens[head]), lse_blk)
        for hp in range(ATTN_HEADS // 2):
            lo, hi = 2 * hp, 2 * hp + 1
            o_ref[rows, hp * LANES:(hp + 1) * LANES] = jnp.where(upper, nums[hi] / dens[hi], nums[lo] / dens[lo])
        lse_ref[rows, :] = lse_blk


def _attn_branch(qkv_view, bsz, seq, dil):
    length = seq // dil
    t = ATTN_STEPS
    tq = t * ATTN_QBLOCKS
    qkv_v = qkv_view.reshape(bsz, length, dil * 3 * D_MODEL)
    cur = lambda part: (lambda b, r, i: (b, i, 3 * r + part))
    prev = lambda part: (lambda b, r, i: (b, jnp.maximum(i * ATTN_QBLOCKS - 1, 0), 3 * r + part))
    blk = (None, tq, D_MODEL)
    pblk = (None, t, D_MODEL)
    o, lse = pl.pallas_call(
        _attn_kernel,
        grid=(bsz, dil, length // tq),
        in_specs=[pl.BlockSpec(blk, cur(0)), pl.BlockSpec(blk, cur(1)), pl.BlockSpec(blk, cur(2)),
                  pl.BlockSpec(pblk, prev(1)), pl.BlockSpec(pblk, prev(2))],
        out_specs=[pl.BlockSpec(blk, lambda b, r, i: (b, i, r)), pl.BlockSpec((None, tq, LANES), lambda b, r, i: (b, i, r))],
        out_shape=[jax.ShapeDtypeStruct((bsz, length, dil * D_MODEL), F32),
                   jax.ShapeDtypeStruct((bsz, length, dil * LANES), F32)],
        compiler_params=_params("arbitrary", "arbitrary", "arbitrary"),
    )(qkv_v, qkv_v, qkv_v, qkv_v, qkv_v)
    return o.reshape(bsz * length, dil * D_MODEL), lse.reshape(bsz * length, dil * LANES)


def kernel(x, w_in_ab, conv_qkv, gdn_a_log, gdn_dt_bias, gdn_norm, s5_lam_re, s5_lam_im, s5_log_dt, s5_b_re, s5_b_im, s5_c_re, s5_c_im, s5_d, s5_w_glu, w_out_ab, w_qkv_c, w_out_c, ln_mix_g, ln_mix_b, router_group_w, router_group_b, router_expert_w, router_expert_b, moe_w_gate, moe_w_up, moe_w_down, ln_ffn_g, ln_ffn_b):
    bsz, seq, d = x.shape
    n = bsz * seq
    x2 = x.reshape(n, d)
    tm = ROW_TILE

    w_in = w_in_ab[0]
    nq = 4 * GDN_W
    ba_cols = jnp.zeros((d, LANES), F32).at[:, :2 * GDN_HEADS].set(w_in[:, nq:nq + 2 * GDN_HEADS])
    w_cat = jnp.concatenate([w_in[:, :nq], w_in[:, nq + 2 * GDN_HEADS:], ba_cols], axis=1).astype(BF16)
    qkv, z, u, ba = _inproj(x2, w_cat)
    ya = _gdn(qkv, z, ba, conv_qkv[0], gdn_a_log[0], gdn_dt_bias[0], gdn_norm[0], bsz, seq)
    yb = _s5(u.reshape(bsz, seq, S5_W), s5_lam_re[0], s5_lam_im[0], s5_log_dt[0], s5_b_re[0], s5_b_im[0],
             s5_c_re[0], s5_c_im[0], s5_d[0], s5_w_glu[0]).reshape(n, S5_W)
    router0 = (router_group_w[0], router_group_b[0], router_expert_w[0], router_expert_b[0])
    h, route, pos, tab = _mix_call(_mix_ab_kernel, ((ya, tm), (yb, tm)), x2, w_out_ab[0], ln_mix_g[0], ln_mix_b[0], router0)
    h = _moe(h, route, pos, tab, moe_w_gate, moe_w_up, moe_w_down, ln_ffn_g[0], ln_ffn_b[0], 0)

    acts = []
    for qkv_view, dil in zip(_qkv_proj(h, w_qkv_c[0]), DILATIONS):
        acts.append(_attn_branch(qkv_view, bsz, seq, dil))
    lanes_per_head = LANES // ATTN_HEADS
    expand = (jnp.arange(LANES)[:, None] == (jnp.arange(D_MODEL)[None, :] // ATTN_DIM) * lanes_per_head).astype(BF16)
    router1 = (router_group_w[1], router_group_b[1], router_expert_w[1], router_expert_b[1])
    d4, d16 = DILATIONS[1], DILATIONS[2]
    (o1, l1), (o4, l4), (o16, l16) = acts
    h, route, pos, tab = _mix_call(
        _mix_c_kernel, ((o1, tm), (o4, tm // d4), (o16, tm // d16), (l1, tm), (l4, tm // d4), (l16, tm // d16)),
        h, w_out_c[0], ln_mix_g[1], ln_mix_b[1], router1, extra=(expand,),
        scratch=(pltpu.VMEM((D_MODEL // LANES, tm, LANES), F32), pltpu.VMEM((D_MODEL // LANES, tm, LANES), F32),
                 pltpu.VMEM((1, tm, LANES), F32), pltpu.VMEM((1, tm, LANES), F32)))
    h = _moe(h, route, pos, tab, moe_w_gate, moe_w_up, moe_w_down, ln_ffn_g[1], ln_ffn_b[1], 1)
    return h.reshape(bsz, seq, d)
```

```python
import functools
import math

import jax
import jax.numpy as jnp
from jax import lax
from jax.experimental import pallas as pl
from jax.experimental.pallas import tpu as pltpu

F32 = jnp.float32
BF16 = jnp.bfloat16
HI = lax.Precision.HIGHEST

D_MODEL = 1024
DEPTH = 2
ALPHA = (2 * DEPTH) ** 0.25
LN_EPS = 1e-5

GDN_HEADS = 4
GDN_DIM = 128
GDN_W = GDN_HEADS * GDN_DIM
CONV_K = 4
GDN_CHUNK = 64
GDN_BATCH = 8

S5_W = 512
S5_GROUP_CH = 16
S5_GROUPS = 32
S5_STATE = 64
S5_SUPER = 4
S5_SUPER_CH = S5_W // S5_SUPER
S5_SUPER_ST = S5_GROUPS * S5_STATE // S5_SUPER

ATTN_HEADS = 16
ATTN_DIM = 64
ATTN_STEPS = 128
DILATIONS = (1, 4, 16)

MOE_GROUPS = 4
EXPERTS_PER_GROUP = 8
N_EXPERTS = 32
EXPERT_FF = 512
EXPERT_LANE0 = MOE_GROUPS

LANES = 128
SUBLANES = 8
VMEM_LIMIT = 56 * 1024 * 1024

ROW_TILE = 512
S5_STEPS = 128
MOE_ROWS = 512
GATHER_TILE = 256

NEG = -1e30
NT_DIMS = (((1,), (1,)), ((), ()))
TN_DIMS = (((0,), (0,)), ((), ()))


def _params(*sem):
    return pltpu.CompilerParams(dimension_semantics=sem, vmem_limit_bytes=VMEM_LIMIT)


def _sigmoid(x):
    return 1.0 / (1.0 + jnp.exp(-x))


def _softplus(x):
    return jnp.maximum(x, 0.0) + jnp.log(1.0 + jnp.exp(-jnp.abs(x)))


def _layer_norm(r, g, b):
    mu = jnp.mean(r, axis=-1, keepdims=True)
    c = r - mu
    var = jnp.mean(c * c, axis=-1, keepdims=True)
    return c * lax.rsqrt(var + LN_EPS) * g + b


def _dot(a, b, precision=None):
    return jnp.dot(a, b, precision=precision, preferred_element_type=F32)


def _mm(a, b, dims=None):
    a, b = a.astype(BF16), b.astype(BF16)
    if dims is None:
        return jnp.dot(a, b, preferred_element_type=F32)
    return lax.dot_general(a, b, dims, preferred_element_type=F32)


def _split_bf16(x):
    hi = x.astype(BF16)
    return hi, (x - hi.astype(F32)).astype(BF16)


IN_COLS = 3 * GDN_W + GDN_W + S5_W + LANES


def _inproj_kernel(x_ref, w_ref, qkv_ref, z_ref, u_ref, ba_ref):
    x = x_ref[...].astype(BF16)
    qkv_ref[...] = _dot(x, w_ref[:, 0:1536])
    z_ref[...] = _dot(x, w_ref[:, 1536:2048])
    u_ref[...] = _dot(x, w_ref[:, 2048:2560])
    ba_ref[...] = _dot(x, w_ref[:, 2560:2688])


def _inproj(x2, w_cat):
    n = x2.shape[0]
    tm = ROW_TILE
    row = lambda i: (i, 0)
    return pl.pallas_call(
        _inproj_kernel,
        grid=(n // tm,),
        in_specs=[pl.BlockSpec((tm, D_MODEL), row), pl.BlockSpec((D_MODEL, IN_COLS), lambda i: (0, 0))],
        out_specs=[pl.BlockSpec((tm, 3 * GDN_W), row), pl.BlockSpec((tm, GDN_W), row),
                   pl.BlockSpec((tm, S5_W), row), pl.BlockSpec((tm, LANES), row)],
        out_shape=[jax.ShapeDtypeStruct((n, 3 * GDN_W), F32), jax.ShapeDtypeStruct((n, GDN_W), F32),
                   jax.ShapeDtypeStruct((n, S5_W), F32), jax.ShapeDtypeStruct((n, LANES), F32)],
        compiler_params=_params("arbitrary"),
    )(x2, w_cat)


def _l2norm(t):
    return t * lax.rsqrt(jnp.sum(t * t, axis=-1, keepdims=True) + 1e-6)


def _short_conv(x, taps):
    out = x * taps[CONV_K - 1:CONV_K, :]
    for j in range(1, CONV_K):
        out = out + pltpu.roll(x, j, axis=0) * taps[CONV_K - 1 - j:CONV_K - j, :]
    return out


def _each(fn, *lists):
    return [fn(*args) for args in zip(*lists)]


def _unit_lower_inverse(lows, same16, cross32, cross64, eye):
    ps = [jnp.where(same16, low, 0.0) for low in lows]
    ts = [eye - p for p in ps]
    for _ in range(3):
        ps = _each(lambda p: _mm(p, p), ps)
        ts = _each(lambda t, p: _mm(t, eye + p), ts, ps)
    for cross in (cross32, cross64):
        tc = _each(lambda t, low: _mm(t, jnp.where(cross, low, 0.0)), ts, lows)
        ts = _each(lambda t, x: t - _mm(x, t), ts, tc)
    return ts


def _gdn_kernel(qkv_ref, z_ref, ba_ref, convw_ref, alog_ref, dtb_ref, normw_ref, tri_ref, o_ref,
                state_ref, tail_ref):
    c = GDN_CHUNK

    @pl.when(pl.program_id(1) == 0)
    def _():
        state_ref[...] = jnp.zeros_like(state_ref)
        tail_ref[...] = jnp.zeros_like(tail_ref)

    w = convw_ref[...]
    ri = lax.broadcasted_iota(jnp.int32, (c, c), 0)
    ci = lax.broadcasted_iota(jnp.int32, (c, c), 1)
    causal = ri >= ci
    strict = ri > ci
    eye = (ri == ci).astype(F32)
    same16 = (ri // 16) == (ci // 16)
    same32 = (ri // 32) == (ci // 32)
    cross32 = same32 & jnp.logical_not(same16)
    cross64 = jnp.logical_not(same32)
    normw = normw_ref[...]
    a_neg = -jnp.exp(alog_ref[...])
    head_rows = (lax.broadcasted_iota(jnp.int32, (SUBLANES, LANES), 1)
                 == lax.broadcasted_iota(jnp.int32, (SUBLANES, LANES), 0) + GDN_HEADS).astype(F32)

    chains, qs, ks, kbs, vbs, kws, decays, egcs, kdecs, glasts = [], [], [], [], [], [], [], [], [], []
    for bi in range(GDN_BATCH):
        x = qkv_ref[bi]
        head = _short_conv(jnp.concatenate([tail_ref[bi], x[:SUBLANES]], axis=0), w)[SUBLANES:]
        conv = jnp.concatenate([head, _short_conv(x, w)[SUBLANES:]], axis=0)
        tail_ref[bi] = x[c - SUBLANES:, :]
        qkv = conv * _sigmoid(conv)

        ba = ba_ref[bi]
        beta_all = _sigmoid(ba)
        g_all = a_neg * _softplus(ba + dtb_ref[...])
        gc_all = _dot(tri_ref[...], g_all, HI)
        egc_all = jnp.exp(gc_all)
        gc_rows = lax.dot_general(head_rows, gc_all, NT_DIMS, precision=HI, preferred_element_type=F32)

        for h in range(GDN_HEADS):
            q = _l2norm(qkv[:, h * GDN_DIM:(h + 1) * GDN_DIM]) * (GDN_DIM ** -0.5)
            k = _l2norm(qkv[:, GDN_W + h * GDN_DIM:GDN_W + (h + 1) * GDN_DIM])
            v = qkv[:, 2 * GDN_W + h * GDN_DIM:2 * GDN_W + (h + 1) * GDN_DIM]
            gl = GDN_HEADS + h
            beta = beta_all[:, h:h + 1]
            gc = gc_all[:, gl:gl + 1]
            egc = egc_all[:, gl:gl + 1]
            gc_last = gc_all[c - 1:c, gl:gl + 1]
            kb = k * beta
            chains.append((bi, h))
            qs.append(q)
            ks.append(k)
            kbs.append(kb)
            vbs.append(v * beta)
            kws.append(kb * egc)
            decays.append(jnp.exp(jnp.where(causal, gc - gc_rows[h:h + 1, :], NEG)))
            egcs.append(egc)
            kdecs.append(k * jnp.exp(gc_last - gc))
            glasts.append(jnp.exp(gc_last))

    kk = _each(lambda kb, q, k: _mm(jnp.concatenate([kb, q], axis=0), k, NT_DIMS), kbs, qs, ks)
    lows = _each(lambda a, dec: jnp.where(strict, a[:c] * dec, 0.0), kk, decays)
    qks = _each(lambda a, dec: jnp.where(causal, a[c:] * dec, 0.0), kk, decays)
    ts = _unit_lower_inverse(lows, same16, cross32, cross64, eye)
    sols = _each(lambda t, vb, kw: _mm(t, jnp.concatenate([vb, kw], axis=1)), ts, vbs, kws)
    states = [state_ref[bi, h] for bi, h in chains]
    wss = _each(lambda sol, q, egc, s: _mm(jnp.concatenate([sol[:, GDN_DIM:], q * egc], axis=0), s),
                sols, qs, egcs, states)
    v_news = _each(lambda sol, ws: sol[:, :GDN_DIM] - ws[:c], sols, wss)
    outs = _each(lambda ws, qk, v_new: ws[c:] + _mm(qk, v_new), wss, qks, v_news)
    upds = _each(lambda kd, v_new: _mm(kd, v_new, TN_DIMS), kdecs, v_news)
    for (bi, h), s, gl, upd, o in zip(chains, states, glasts, upds, outs):
        hs = slice(h * GDN_DIM, (h + 1) * GDN_DIM)
        state_ref[bi, h] = s * gl + upd
        zh = z_ref[bi, :, hs]
        o_ref[bi, :, hs] = o * lax.rsqrt(jnp.mean(o * o, axis=-1, keepdims=True) + 1e-6) * normw * (zh * _sigmoid(zh))


def _gdn(qkv, z, ba, conv_w, a_log, dt_bias, norm_w, bsz, seq):
    c = GDN_CHUNK
    nb = GDN_BATCH
    convw = jnp.zeros((SUBLANES, 3 * GDN_W), F32).at[:CONV_K].set(conv_w)
    alog = jnp.zeros((1, LANES), F32).at[0, GDN_HEADS:2 * GDN_HEADS].set(a_log)
    dtb = jnp.zeros((1, LANES), F32).at[0, GDN_HEADS:2 * GDN_HEADS].set(dt_bias)
    tri = jnp.tril(jnp.ones((c, c), F32))
    row = lambda b, i: (b, i, 0)
    const = lambda b, i: (0, 0)
    out = pl.pallas_call(
        _gdn_kernel,
        grid=(bsz // nb, seq // c),
        in_specs=[pl.BlockSpec((nb, c, 3 * GDN_W), row), pl.BlockSpec((nb, c, GDN_W), row),
                  pl.BlockSpec((nb, c, LANES), row),
                  pl.BlockSpec((SUBLANES, 3 * GDN_W), const), pl.BlockSpec((1, LANES), const),
                  pl.BlockSpec((1, LANES), const), pl.BlockSpec((1, GDN_DIM), const), pl.BlockSpec((c, c), const)],
        out_specs=pl.BlockSpec((nb, c, GDN_W), row),
        out_shape=jax.ShapeDtypeStruct((bsz, seq, GDN_W), F32),
        scratch_shapes=[pltpu.VMEM((nb, GDN_HEADS, GDN_DIM, GDN_DIM), F32),
                        pltpu.VMEM((nb, SUBLANES, 3 * GDN_W), F32)],
        compiler_params=_params("arbitrary", "arbitrary"),
    )(qkv.reshape(bsz, seq, 3 * GDN_W), z.reshape(bsz, seq, GDN_W), ba.reshape(bsz, seq, LANES),
      convw, alog, dtb, norm_w.reshape(1, GDN_DIM), tri)
    return out.reshape(bsz * seq, GDN_W)


def _gelu_tanh(x):
    return x * (0.5 * (1.0 + jnp.tanh(math.sqrt(2.0 / math.pi) * (x + 0.044715 * (x * x * x)))))


def _s5_kernel(u_ref, wb_ref, a_ref, wc_ref, d_ref, wglu_ref, y_ref, utb_ref, ytb_ref, *rest, steps):
    bu_refs, h_ref = rest[:-1], rest[-1]

    @pl.when(pl.program_id(0) == 0)
    def _():
        h_ref[...] = jnp.zeros_like(h_ref)

    nb = h_ref.shape[0]
    nlc = S5_W // LANES
    for b in range(nb):
        for lc in range(nlc):
            utb_ref[lc, pl.ds(b, steps, stride=nb), :] = u_ref[b, :, lc * LANES:(lc + 1) * LANES]
    u = jnp.concatenate([utb_ref[lc] for lc in range(nlc)], axis=1)
    ub = u.astype(BF16)
    st = S5_SUPER_ST
    for j in range(S5_SUPER):
        bu_refs[j][...] = _dot(ub[:, S5_SUPER_CH * j:S5_SUPER_CH * (j + 1)], wb_ref[j])

    ys = []
    for j in range(S5_SUPER):
        bu_ref = bu_refs[j]
        re = slice(2 * st * j, 2 * st * j + st)
        im = slice(2 * st * j + st, 2 * st * (j + 1))
        ar = jnp.broadcast_to(a_ref[0:1, st * j:st * (j + 1)], (nb, st))
        ai = jnp.broadcast_to(a_ref[1:2, st * j:st * (j + 1)], (nb, st))
        hr, hi = h_ref[:, re], h_ref[:, im]
        for t in range(steps):
            rows = slice(t * nb, (t + 1) * nb)
            hr, hi = (ar * hr - ai * hi + bu_ref[rows, 0:st], ar * hi + ai * hr + bu_ref[rows, st:2 * st])
            bu_ref[rows, 0:st] = hr
            bu_ref[rows, st:2 * st] = hi
        h_ref[:, re] = hr
        h_ref[:, im] = hi
        ys.append(_dot(bu_ref[...].astype(BF16), wc_ref[j]))

    y = jnp.concatenate(ys, axis=1)
    y = _gelu_tanh(y + d_ref[...] * u)
    y = y * _sigmoid(_dot(y.astype(BF16), wglu_ref[...]))
    for lc in range(nlc):
        ytb_ref[lc] = y[:, lc * LANES:(lc + 1) * LANES]
    for b in range(nb):
        for lc in range(nlc):
            y_ref[b, :, lc * LANES:(lc + 1) * LANES] = ytb_ref[lc, pl.ds(b, steps, stride=nb), :]


def _s5(u3, lam_re, lam_im, log_dt, b_re, b_im, c_re, c_im, d_skip, w_glu):
    bsz, seq, _ = u3.shape
    dt = jnp.exp(log_dt)[:, None]
    mag = jnp.exp(lam_re * dt)
    a_re, a_im = mag * jnp.cos(lam_im * dt), mag * jnp.sin(lam_im * dt)
    den = lam_re * lam_re + lam_im * lam_im
    nr, ni = a_re - 1.0, a_im
    cr = (nr * lam_re + ni * lam_im) / den
    ci = (ni * lam_re - nr * lam_im) / den
    bb_re = cr[..., None] * b_re - ci[..., None] * b_im
    bb_im = cr[..., None] * b_im + ci[..., None] * b_re
    gps = S5_GROUPS // S5_SUPER
    eye = jnp.eye(gps, dtype=F32)

    def in_blockdiag(t):
        t = t.reshape(S5_SUPER, gps, S5_STATE, S5_GROUP_CH)
        return jnp.einsum('jgph,gk->jghkp', t, eye).reshape(S5_SUPER, S5_SUPER_CH, S5_SUPER_ST)

    def out_blockdiag(t):
        t = t.reshape(S5_SUPER, gps, S5_GROUP_CH, S5_STATE)
        return jnp.einsum('jgkp,gm->jgpmk', t, eye).reshape(S5_SUPER, S5_SUPER_ST, S5_SUPER_CH)

    wb = jnp.concatenate([in_blockdiag(bb_re), in_blockdiag(bb_im)], axis=2).astype(BF16)
    wc = jnp.concatenate([out_blockdiag(c_re), -out_blockdiag(c_im)], axis=1).astype(BF16)
    nstate = S5_GROUPS * S5_STATE
    a = jnp.stack([a_re.reshape(nstate), a_im.reshape(nstate)], axis=0)

    steps = S5_STEPS
    rows = steps * bsz
    blk = lambda i: (0, i, 0)
    const2 = lambda i: (0, 0)
    const3 = lambda i: (0, 0, 0)
    return pl.pallas_call(
        functools.partial(_s5_kernel, steps=steps),
        grid=(seq // steps,),
        in_specs=[pl.BlockSpec((bsz, steps, S5_W), blk), pl.BlockSpec(wb.shape, const3),
                  pl.BlockSpec((2, nstate), const2), pl.BlockSpec(wc.shape, const3),
                  pl.BlockSpec((1, S5_W), const2), pl.BlockSpec((S5_W, S5_W), const2)],
        out_specs=pl.BlockSpec((bsz, steps, S5_W), blk),
        out_shape=jax.ShapeDtypeStruct((bsz, seq, S5_W), F32),
        scratch_shapes=[pltpu.VMEM((S5_W // LANES, rows, LANES), F32), pltpu.VMEM((S5_W // LANES, rows, LANES), F32),
                        *[pltpu.VMEM((rows, 2 * S5_SUPER_ST), F32) for _ in range(S5_SUPER)],
                        pltpu.VMEM((bsz, 2 * nstate), F32)],
        compiler_params=_params("arbitrary"),
    )(u3, wb, a, wc, d_skip.reshape(1, S5_W), w_glu.astype(BF16))


ROUTE_PARTS = 2


def _row_parts(tm):
    rows = tm // ROUTE_PARTS
    return [slice(k * rows, (k + 1) * rows) for k in range(ROUTE_PARTS)]


def _route(hs, wr_ref, br_ref, tri_ref, run_ref, route_ref, pos_ref, tab_ref):
    @pl.when(pl.program_id(0) == 0)
    def _():
        run_ref[...] = jnp.zeros_like(run_ref)

    rows = hs[0].shape[0]
    splits = [_split_bf16(h) for h in hs]
    both = [_dot(hi, wr_ref[...]) for hi, _ in splits]
    cross = [_dot(lo, wr_ref[:, :LANES]) for _, lo in splits]
    logits = [b[:, :LANES] + b[:, LANES:] + c + br_ref[...] for b, c in zip(both, cross)]
    lane = lax.broadcasted_iota(jnp.int32, (rows, LANES), 1)
    lanef = lane.astype(F32)
    big = float(LANES)

    def rmax(x):
        return jnp.max(x, axis=-1, keepdims=True)

    def rmin(x):
        return jnp.min(x, axis=-1, keepdims=True)

    def rsum(x):
        return jnp.sum(x, axis=-1, keepdims=True)

    gmask = lane < MOE_GROUPS
    gmax = [rmax(jnp.where(gmask, l, NEG)) for l in logits]
    gsum = [rsum(jnp.where(gmask, jnp.exp(l - m), 0.0)) for l, m in zip(logits, gmax)]
    gidx = [rmin(jnp.where(gmask & (l == m), lanef, big)) for l, m in zip(logits, gmax)]

    lows = [EXPERT_LANE0 + EXPERTS_PER_GROUP * g for g in gidx]
    emask = [(lanef >= lo) & (lanef < lo + EXPERTS_PER_GROUP) for lo in lows]
    el = [jnp.where(em, l, NEG) for em, l in zip(emask, logits)]
    t1 = [rmax(e) for e in el]
    i1 = [rmin(jnp.where(em & (e == t), lanef, big)) for em, e, t in zip(emask, el, t1)]
    emask2 = [em & (lanef != i) for em, i in zip(emask, i1)]
    el2 = [jnp.where(em, l, NEG) for em, l in zip(emask2, logits)]
    t2 = [rmax(e) for e in el2]
    i2 = [rmin(jnp.where(em & (e == t), lanef, big)) for em, e, t in zip(emask2, el2, t2)]
    e2 = [jnp.exp(b - a) for a, b in zip(t1, t2)]
    gate1 = [(1.0 / gs) / (1.0 + e) for gs, e in zip(gsum, e2)]
    gate2 = [(1.0 / gs) * e / (1.0 + e) for gs, e in zip(gsum, e2)]

    oh1 = [lanef == i for i in i1]
    oh2 = [lanef == i for i in i2]
    oh = jnp.concatenate([jnp.where(a | b, 1.0, 0.0) for a, b in zip(oh1, oh2)], axis=0)
    cnt = jnp.sum(oh, axis=0, keepdims=True)
    ei = lax.broadcasted_iota(jnp.int32, (LANES, LANES), 0)
    ej = lax.broadcasted_iota(jnp.int32, (LANES, LANES), 1)
    start = _dot(jnp.broadcast_to(cnt, (SUBLANES, LANES)), (ei < ej).astype(F32), HI)[0:1]
    where = _dot(tri_ref[...], oh.astype(BF16)) + start
    wheres = [where[k * rows:(k + 1) * rows] for k in range(len(hs))]
    pos1 = [rsum(jnp.where(o, w, 0.0)) for o, w in zip(oh1, wheres)]
    pos2 = [rsum(jnp.where(o, w, 0.0)) for o, w in zip(oh2, wheres)]

    run = run_ref[...]
    srow = lax.broadcasted_iota(jnp.int32, (SUBLANES, LANES), 0)
    slane = lax.broadcasted_iota(jnp.int32, (SUBLANES, LANES), 1)
    tab_ref[...] = jnp.where(srow == 0, cnt, jnp.where(srow == 1, run, jnp.where(srow == 2, start, 0.0)))
    run_ref[...] = run + cnt

    routes = []
    for g1, g2, p1, p2 in zip(gate1, gate2, pos1, pos2):
        route = jnp.where(lane == 2, g1, 0.0)
        route = jnp.where(lane == 3, g2, route)
        route = jnp.where(lane == 4, p1, route)
        routes.append(jnp.where(lane == 5, p2, route))
    route = jnp.concatenate(routes, axis=0)
    route_ref[...] = route
    pick = (slane == srow + 4).astype(F32)
    pos_ref[...] = lax.dot_general(pick, route, NT_DIMS, precision=HI, preferred_element_type=F32)


def _mix_ab_kernel(ya_ref, yb_ref, x_ref, w_ref, g_ref, b_ref, wr_ref, br_ref, tri_ref,
                   h_ref, route_ref, pos_ref, tab_ref, run_ref):
    parts = _row_parts(x_ref.shape[0])
    mixes = [_dot(ya_ref[rows, :].astype(BF16), w_ref[0:GDN_W, :]) + _dot(yb_ref[rows, :].astype(BF16), w_ref[GDN_W:, :])
             for rows in parts]
    hs = [_layer_norm(ALPHA * x_ref[rows, :] + mix, g_ref[...], b_ref[...]) for rows, mix in zip(parts, mixes)]
    for rows, h in zip(parts, hs):
        h_ref[rows, :] = h
    _route(hs, wr_ref, br_ref, tri_ref, run_ref, route_ref, pos_ref, tab_ref)


def _to_natural(view_ref, nat_ref, dil):
    rows = view_ref.shape[0]
    nlc = nat_ref.shape[0]
    for r in range(dil):
        for lc in range(nlc):
            col = (r * nlc + lc) * LANES
            nat_ref[lc, pl.ds(r, rows, stride=dil), :] = view_ref[:, col:col + LANES]
    return jnp.concatenate([nat_ref[lc] for lc in range(nlc)], axis=1)


def _mix_c_kernel(o1_ref, o4_ref, o16_ref, l1_ref, l4_ref, l16_ref, expand_ref, x_ref, w_ref, g_ref, b_ref,
                  wr_ref, br_ref, tri_ref, h_ref, route_ref, pos_ref, tab_ref, run_ref,
                  on4_ref, on16_ref, ln4_ref, ln16_ref):
    o4 = _to_natural(o4_ref, on4_ref, DILATIONS[1])
    o16 = _to_natural(o16_ref, on16_ref, DILATIONS[2])
    l2 = _to_natural(l4_ref, ln4_ref, DILATIONS[1])
    l3 = _to_natural(l16_ref, ln16_ref, DILATIONS[2])
    l1 = l1_ref[...]
    m = jnp.maximum(jnp.maximum(l1, l2), l3)
    e1, e2, e3 = jnp.exp(l1 - m), jnp.exp(l2 - m), jnp.exp(l3 - m)
    inv = 1.0 / (e1 + e2 + e3)
    ex = expand_ref[...]

    def spread(wt):
        hi, lo = _split_bf16(wt)
        return _dot(hi, ex) + _dot(lo, ex)

    o = spread(e1 * inv) * o1_ref[...] + spread(e2 * inv) * o4 + spread(e3 * inv) * o16
    parts = _row_parts(x_ref.shape[0])
    mixes = [_dot(o[rows].astype(BF16), w_ref[...]) for rows in parts]
    hs = [_layer_norm(ALPHA * x_ref[rows, :] + mix, g_ref[...], b_ref[...]) for rows, mix in zip(parts, mixes)]
    for rows, h in zip(parts, hs):
        h_ref[rows, :] = h
    _route(hs, wr_ref, br_ref, tri_ref, run_ref, route_ref, pos_ref, tab_ref)


def _router_operands(wg, bg, we, be):
    wr = jnp.zeros((D_MODEL, LANES), F32).at[:, :MOE_GROUPS].set(wg).at[:, EXPERT_LANE0:EXPERT_LANE0 + N_EXPERTS].set(we)
    wr_hi = wr.astype(BF16)
    wr_lo = (wr - wr_hi.astype(F32)).astype(BF16)
    br = jnp.zeros((1, LANES), F32).at[0, :MOE_GROUPS].set(bg).at[0, EXPERT_LANE0:EXPERT_LANE0 + N_EXPERTS].set(be)
    tri = jnp.tril(jnp.ones((ROW_TILE, ROW_TILE), F32), -1).astype(BF16)
    return jnp.concatenate([wr_hi, wr_lo], axis=1), br, tri


def _mix_call(kernel_fn, acts, x2, w_out, ln_g, ln_b, router, extra=(), scratch=()):
    n = x2.shape[0]
    tm = ROW_TILE
    wr, br, tri = _router_operands(*router)
    row = lambda i: (i, 0)
    const = lambda i: (0, 0)
    in_specs = [pl.BlockSpec((rows, a.shape[1]), row) for a, rows in acts]
    in_specs += [pl.BlockSpec(e.shape, const) for e in extra]
    in_specs += [pl.BlockSpec((tm, D_MODEL), row), pl.BlockSpec(w_out.shape, const),
                 pl.BlockSpec((1, D_MODEL), const), pl.BlockSpec((1, D_MODEL), const),
                 pl.BlockSpec((D_MODEL, 2 * LANES), const), pl.BlockSpec((1, LANES), const),
                 pl.BlockSpec((tm, tm), const)]
    return pl.pallas_call(
        kernel_fn,
        grid=(n // tm,),
        in_specs=in_specs,
        out_specs=[pl.BlockSpec((tm, D_MODEL), row), pl.BlockSpec((tm, LANES), row),
                   pl.BlockSpec((SUBLANES, tm), lambda i: (0, i)),
                   pl.BlockSpec((None, SUBLANES, LANES), lambda i: (i, 0, 0))],
        out_shape=[jax.ShapeDtypeStruct((n, D_MODEL), F32), jax.ShapeDtypeStruct((n, LANES), F32),
                   jax.ShapeDtypeStruct((SUBLANES, n), F32), jax.ShapeDtypeStruct((n // tm, SUBLANES, LANES), F32)],
        scratch_shapes=[pltpu.VMEM((1, LANES), F32), *scratch],
        compiler_params=_params("arbitrary"),
    )(*[a for a, _ in acts], *extra, x2, w_out.astype(BF16), ln_g.reshape(1, D_MODEL), ln_b.reshape(1, D_MODEL),
      wr, br, tri)


SLAB = D_MODEL // LANES
SEG_SIZES = tuple(1 << k for k in range(ROW_TILE.bit_length() - 1, -1, -1))
SEG_RARE = 64


def _segment_tables(tab, n_blocks):
    lanes = slice(EXPERT_LANE0, EXPERT_LANE0 + N_EXPERTS)
    cnt = tab[:, 0, lanes].astype(jnp.int32)
    run = tab[:, 1, lanes].astype(jnp.int32)
    src = tab[:, 2, lanes].astype(jnp.int32)
    counts = run[-1] + cnt[-1]
    padded = (counts + MOE_ROWS - 1) // MOE_ROWS * MOE_ROWS
    padded_end = jnp.cumsum(padded)
    dst = (padded_end - padded)[None, :] + run
    tail = jnp.arange(N_EXPERTS, dtype=jnp.int32) * MOE_ROWS + padded_end[-1]
    tail_cnt = jnp.where(tail < n_blocks * MOE_ROWS, MOE_ROWS, 0)
    cnt = jnp.concatenate([cnt, (padded - counts)[None, :], tail_cnt[None, :]], axis=0)
    src = jnp.concatenate([src, jnp.zeros((2, N_EXPERTS), jnp.int32)], axis=0)
    dst = jnp.concatenate([dst, (padded_end - padded + counts)[None, :], tail[None, :]], axis=0)
    block_start = jnp.arange(n_blocks, dtype=jnp.int32) * MOE_ROWS
    block_expert = jnp.minimum(jnp.sum((padded_end[None, :] <= block_start[:, None]).astype(jnp.int32), axis=1),
                               N_EXPERTS - 1)
    n_used = (padded_end[-1:] // MOE_ROWS).astype(jnp.int32)
    return (cnt.reshape(-1), src.reshape(-1), dst.reshape(-1)), block_expert, n_used


def _slab_rows(ref, start, size):
    return ref.at[pl.ds(pl.multiple_of(start * SLAB, SLAB), size * SLAB)]


def _for_each_segment(tabs, tile, fn):
    cnt_ref, src_ref, dst_ref = tabs

    def per_expert(e, carry):
        idx = tile * N_EXPERTS + e
        c, s, d = cnt_ref[idx], src_ref[idx], dst_ref[idx]

        def pieces(sizes):
            for size in sizes:
                @pl.when((c & size) != 0)
                def _():
                    off = c & (-2 * size)
                    fn(s + off, d + off, size)

        @pl.when(c >= SEG_RARE)
        def _():
            pieces([size for size in SEG_SIZES if size >= SEG_RARE])

        pieces([size for size in SEG_SIZES if size < SEG_RARE])
        return carry

    lax.fori_loop(0, N_EXPERTS, per_expert, 0)


def _to_slabs(val, ref):
    for lc in range(SLAB):
        ref[pl.ds(lc, val.shape[0], stride=SLAB), :] = val[:, lc * LANES:(lc + 1) * LANES]


def _from_slabs(ref):
    rows = ref.shape[0] // SLAB
    return jnp.concatenate([ref[pl.ds(lc, rows, stride=SLAB), :] for lc in range(SLAB)], axis=1)


def _dispatch_kernel(cnt_ref, src_ref, dst_ref, h_ref, pos_ref, xs_ref, sort_ref, sem):
    tile = pl.program_id(0)
    last = pl.num_programs(0) - 1
    tm = h_ref.shape[0]
    slot = tile % 2
    tabs = (cnt_ref, src_ref, dst_ref)

    def copy(k, s, d, size):
        return pltpu.make_async_copy(_slab_rows(sort_ref.at[k], s, size), _slab_rows(xs_ref, d, size), sem.at[k])

    j = lax.broadcasted_iota(jnp.int32, (2 * tm, tm), 0).astype(F32)
    pos = pos_ref[...]
    perm = jnp.where((j == pos[0:1, :]) | (j == pos[1:2, :]), 1.0, 0.0).astype(BF16)
    _to_slabs(_dot(perm, h_ref[...].astype(BF16)), sort_ref.at[slot])
    _for_each_segment(tabs, tile, lambda s, d, size: copy(slot, s, d, size).start())

    def wait_tile(k):
        pltpu.make_async_copy(sort_ref.at[k], xs_ref.at[pl.ds(0, 2 * tm * SLAB)], sem.at[k]).wait()

    @pl.when(tile > 0)
    def _():
        wait_tile(1 - slot)

    @pl.when(tile == last)
    def _():
        wait_tile(slot)
        sort_ref[slot, 0:MOE_ROWS * SLAB, :] = jnp.zeros((MOE_ROWS * SLAB, LANES), F32)
        for pseudo in (1, 2):
            _for_each_segment(tabs, tile + pseudo, lambda s, d, size: copy(slot, s, d, size).start())
        for pseudo in (1, 2):
            _for_each_segment(tabs, tile + pseudo, lambda s, d, size: copy(slot, s, d, size).wait())


def _dispatch(tabs, h, pos, n_slots):
    n = h.shape[0]
    tm = ROW_TILE
    grid_spec = pltpu.PrefetchScalarGridSpec(
        num_scalar_prefetch=3,
        grid=(n // tm,),
        in_specs=[pl.BlockSpec((tm, D_MODEL), lambda i, *_: (i, 0)), pl.BlockSpec((SUBLANES, tm), lambda i, *_: (0, i))],
        out_specs=pl.BlockSpec(memory_space=pl.ANY),
        scratch_shapes=[pltpu.VMEM((2, 2 * tm * SLAB, LANES), F32), pltpu.SemaphoreType.DMA((2,))],
    )
    return pl.pallas_call(
        _dispatch_kernel,
        grid_spec=grid_spec,
        out_shape=jax.ShapeDtypeStruct((n_slots * SLAB, LANES), F32),
        compiler_params=_params("arbitrary"),
    )(*tabs, h, pos)


def _expert_kernel(be_ref, nu_ref, xs_ref, wg_ref, wu_ref, wd_ref, ys_ref, wgb_ref, wub_ref, wdb_ref):
    i = pl.program_id(0)

    @pl.when((i == 0) | (be_ref[i] != be_ref[jnp.maximum(i - 1, 0)]))
    def _():
        wgb_ref[...] = wg_ref[...].astype(BF16)
        wub_ref[...] = wu_ref[...].astype(BF16)
        wdb_ref[...] = wd_ref[...].astype(BF16)

    @pl.when(i < nu_ref[0])
    def _():
        x = _from_slabs(xs_ref).astype(BF16)
        hg = _dot(x, wgb_ref[...])
        hu = _dot(x, wub_ref[...])
        hdn = hg * _sigmoid(hg) * hu
        _to_slabs(_dot(hdn.astype(BF16), wdb_ref[...]).astype(BF16).astype(F32), ys_ref)

    @pl.when(i >= nu_ref[0])
    def _():
        ys_ref[...] = jnp.zeros_like(ys_ref)


def _experts(block_expert, n_used, xs, w_gate, w_up, w_down, layer):
    blk = MOE_ROWS * SLAB
    row = lambda i, be, nu: (i, 0)
    used_row = lambda i, be, nu: (jnp.minimum(i, jnp.maximum(nu[0] - 1, 0)), 0)
    wsel = lambda i, be, nu: (layer, be[i], 0, 0)
    grid_spec = pltpu.PrefetchScalarGridSpec(
        num_scalar_prefetch=2,
        grid=(xs.shape[0] // blk,),
        in_specs=[pl.BlockSpec((blk, LANES), used_row),
                  pl.BlockSpec((None, None, D_MODEL, EXPERT_FF), wsel),
                  pl.BlockSpec((None, None, D_MODEL, EXPERT_FF), wsel),
                  pl.BlockSpec((None, None, EXPERT_FF, D_MODEL), wsel)],
        out_specs=pl.BlockSpec((blk, LANES), row),
        scratch_shapes=[pltpu.VMEM((D_MODEL, EXPERT_FF), BF16), pltpu.VMEM((D_MODEL, EXPERT_FF), BF16),
                        pltpu.VMEM((EXPERT_FF, D_MODEL), BF16)],
    )
    return pl.pallas_call(
        _expert_kernel,
        grid_spec=grid_spec,
        out_shape=jax.ShapeDtypeStruct(xs.shape, F32),
        compiler_params=_params("arbitrary"),
    )(block_expert, n_used, xs, w_gate, w_up, w_down)


def _combine_kernel(cnt_ref, src_ref, dst_ref, h_ref, route_ref, ys_ref, g_ref, b_ref, o_ref, sort_ref, sem):
    tile = pl.program_id(0)
    last = pl.num_programs(0) - 1
    tm = h_ref.shape[0]
    slot = tile % 2
    tabs = (cnt_ref, src_ref, dst_ref)

    def copy(k, s, d, size):
        return pltpu.make_async_copy(_slab_rows(ys_ref, d, size), _slab_rows(sort_ref.at[k], s, size), sem.at[k])

    @pl.when(tile == 0)
    def _():
        _for_each_segment(tabs, tile, lambda s, d, size: copy(slot, s, d, size).start())

    @pl.when(tile < last)
    def _():
        _for_each_segment(tabs, tile + 1, lambda s, d, size: copy(1 - slot, s, d, size).start())

    pltpu.make_async_copy(ys_ref.at[pl.ds(0, 2 * tm * SLAB)], sort_ref.at[slot], sem.at[slot]).wait()
    ysorted = _from_slabs(sort_ref.at[slot]).astype(BF16)
    r = route_ref[...]
    j = lax.broadcasted_iota(jnp.int32, (2 * tm, 2 * tm), 1).astype(F32)
    where = jnp.concatenate([r[:, 4:5], r[:, 5:6]], axis=0)
    picked = _dot(jnp.where(j == where, 1.0, 0.0).astype(BF16), ysorted)
    ffn = picked[:tm] * r[:, 2:3] + picked[tm:] * r[:, 3:4]
    o_ref[...] = _layer_norm(ALPHA * h_ref[...] + ffn, g_ref[...], b_ref[...])


def _combine(tabs, h, route, ys, ln_g, ln_b):
    n = h.shape[0]
    tm = ROW_TILE
    row = lambda i, *_: (i, 0)
    const = lambda i, *_: (0, 0)
    grid_spec = pltpu.PrefetchScalarGridSpec(
        num_scalar_prefetch=3,
        grid=(n // tm,),
        in_specs=[pl.BlockSpec((tm, D_MODEL), row), pl.BlockSpec((tm, LANES), row), pl.BlockSpec(memory_space=pl.ANY),
                  pl.BlockSpec((1, D_MODEL), const), pl.BlockSpec((1, D_MODEL), const)],
        out_specs=pl.BlockSpec((tm, D_MODEL), row),
        scratch_shapes=[pltpu.VMEM((2, 2 * tm * SLAB, LANES), F32), pltpu.SemaphoreType.DMA((2,))],
    )
    return pl.pallas_call(
        _combine_kernel,
        grid_spec=grid_spec,
        out_shape=jax.ShapeDtypeStruct((n, D_MODEL), F32),
        compiler_params=_params("arbitrary"),
    )(*tabs, h, route, ys, ln_g.reshape(1, D_MODEL), ln_b.reshape(1, D_MODEL))


def _moe(h, route, pos, tab, w_gate, w_up, w_down, ln_g, ln_b, layer):
    n = h.shape[0]
    n_slots = n * 2 + N_EXPERTS * MOE_ROWS
    tabs, block_expert, n_used = _segment_tables(tab, n_slots // MOE_ROWS)
    xs = _dispatch(tabs, h, pos, n_slots)
    ys = _experts(block_expert, n_used, xs, w_gate, w_up, w_down, layer)
    return _combine(tabs, h, route, ys, ln_g, ln_b)


def _qkv_kernel(x_ref, w_ref, o1_ref, o4_ref, o16_ref, acc_ref):
    acc = _dot(x_ref[...].astype(BF16), w_ref[...])
    o1_ref[...] = acc.astype(BF16)
    nlc = acc_ref.shape[0]
    for lc in range(nlc):
        acc_ref[lc] = acc[:, lc * LANES:(lc + 1) * LANES]
    for o_ref, dil in ((o4_ref, DILATIONS[1]), (o16_ref, DILATIONS[2])):
        rows = o_ref.shape[0]
        for r in range(dil):
            for lc in range(nlc):
                col = (r * nlc + lc) * LANES
                o_ref[:, col:col + LANES] = acc_ref[lc, pl.ds(r, rows, stride=dil), :].astype(BF16)


def _qkv_proj(h, w_qkv):
    n = h.shape[0]
    tm = ROW_TILE
    wd = 3 * D_MODEL
    d4, d16 = DILATIONS[1], DILATIONS[2]
    row = lambda i: (i, 0)
    return pl.pallas_call(
        _qkv_kernel,
        grid=(n // tm,),
        in_specs=[pl.BlockSpec((tm, D_MODEL), row), pl.BlockSpec((D_MODEL, wd), lambda i: (0, 0))],
        out_specs=[pl.BlockSpec((tm, wd), row), pl.BlockSpec((tm // d4, d4 * wd), row),
                   pl.BlockSpec((tm // d16, d16 * wd), row)],
        out_shape=[jax.ShapeDtypeStruct((n, wd), BF16), jax.ShapeDtypeStruct((n // d4, d4 * wd), BF16),
                   jax.ShapeDtypeStruct((n // d16, d16 * wd), BF16)],
        scratch_shapes=[pltpu.VMEM((wd // LANES, tm, LANES), F32)],
        compiler_params=_params("arbitrary"),
    )(h, w_qkv.astype(BF16))


ATTN_QBLOCKS = 2


def _attn_kernel(q_ref, kc_ref, vc_ref, kp_ref, vp_ref, o_ref, lse_ref):
    t = ATTN_STEPS
    qi = lax.broadcasted_iota(jnp.int32, (t, 2 * t), 0)
    kj = lax.broadcasted_iota(jnp.int32, (t, 2 * t), 1)
    dist = t + qi - kj
    band = (dist >= 0) & (dist <= t)
    lane = lax.broadcasted_iota(jnp.int32, (t, LANES), 1)
    upper = lane >= ATTN_DIM
    ones_kv = jnp.ones((2 * t, LANES), BF16)
    lanes_per_head = LANES // ATTN_HEADS
    scale = jnp.asarray(ATTN_DIM ** -0.5, BF16)
    for qb in range(ATTN_QBLOCKS):
        rows = slice(qb * t, (qb + 1) * t)
        valid = band & ((kj >= t) | (pl.program_id(2) > 0)) if qb == 0 else band
        scores, vals = [], []
        for hp in range(ATTN_HEADS // 2):
            cs = slice(hp * LANES, (hp + 1) * LANES)
            q2 = q_ref[rows, cs] * scale
            if qb == 0:
                k2 = jnp.concatenate([kp_ref[:, cs], kc_ref[0:t, cs]], axis=0)
                v2 = jnp.concatenate([vp_ref[:, cs], vc_ref[0:t, cs]], axis=0)
            else:
                k2 = kc_ref[(qb - 1) * t:(qb + 1) * t, cs]
                v2 = vc_ref[(qb - 1) * t:(qb + 1) * t, cs]
            vals.append(jnp.concatenate([v2, ones_kv], axis=1))
            for sub in range(2):
                mine = upper if sub else jnp.logical_not(upper)
                qm = jnp.where(mine, q2, jnp.zeros_like(q2))
                scores.append(lax.dot_general(qm, k2, NT_DIMS, preferred_element_type=F32))
        maxes, probs = [], []
        for s in scores:
            s = jnp.where(valid, s, NEG)
            m = jnp.max(s, axis=-1, keepdims=True)
            maxes.append(m)
            probs.append(jnp.exp(s - m).astype(BF16))
        both = [_dot(p, vals[head // 2]) for head, p in enumerate(probs)]
        nums = [r[:, :LANES] for r in both]
        dens = [r[:, LANES:] for r in both]
        lse_blk = jnp.zeros((t, LANES), F32)
        for head in range(ATTN_HEADS):
            lse_blk = jnp.where(lane // lanes_per_head == head, maxes[head] + jnp.log(dens[head]), lse_blk)
        for hp in range(ATTN_HEADS // 2):
            lo, hi = 2 * hp, 2 * hp + 1
            o_ref[rows, hp * LANES:(hp + 1) * LANES] = jnp.where(upper, nums[hi] / dens[hi], nums[lo] / dens[lo])
        lse_ref[rows, :] = lse_blk


def _attn_branch(qkv_view, bsz, seq, dil):
    length = seq // dil
    t = ATTN_STEPS
    tq = t * ATTN_QBLOCKS
    qkv_v = qkv_view.reshape(bsz, length, dil * 3 * D_MODEL)
    cur = lambda part: (lambda b, r, i: (b, i, 3 * r + part))
    prev = lambda part: (lambda b, r, i: (b, jnp.maximum(i * ATTN_QBLOCKS - 1, 0), 3 * r + part))
    blk = (None, tq, D_MODEL)
    pblk = (None, t, D_MODEL)
    o, lse = pl.pallas_call(
        _attn_kernel,
        grid=(bsz, dil, length // tq),
        in_specs=[pl.BlockSpec(blk, cur(0)), pl.BlockSpec(blk, cur(1)), pl.BlockSpec(blk, cur(2)),
                  pl.BlockSpec(pblk, prev(1)), pl.BlockSpec(pblk, prev(2))],
        out_specs=[pl.BlockSpec(blk, lambda b, r, i: (b, i, r)), pl.BlockSpec((None, tq, LANES), lambda b, r, i: (b, i, r))],
        out_shape=[jax.ShapeDtypeStruct((bsz, length, dil * D_MODEL), F32),
                   jax.ShapeDtypeStruct((bsz, length, dil * LANES), F32)],
        compiler_params=_params("arbitrary", "arbitrary", "arbitrary"),
    )(qkv_v, qkv_v, qkv_v, qkv_v, qkv_v)
    return o.reshape(bsz * length, dil * D_MODEL), lse.reshape(bsz * length, dil * LANES)


def kernel(x, w_in_ab, conv_qkv, gdn_a_log, gdn_dt_bias, gdn_norm, s5_lam_re, s5_lam_im, s5_log_dt, s5_b_re, s5_b_im, s5_c_re, s5_c_im, s5_d, s5_w_glu, w_out_ab, w_qkv_c, w_out_c, ln_mix_g, ln_mix_b, router_group_w, router_group_b, router_expert_w, router_expert_b, moe_w_gate, moe_w_up, moe_w_down, ln_ffn_g, ln_ffn_b):
    bsz, seq, d = x.shape
    n = bsz * seq
    x2 = x.reshape(n, d)
    tm = ROW_TILE

    w_in = w_in_ab[0]
    nq = 4 * GDN_W
    ba_cols = jnp.zeros((d, LANES), F32).at[:, :2 * GDN_HEADS].set(w_in[:, nq:nq + 2 * GDN_HEADS])
    w_cat = jnp.concatenate([w_in[:, :nq], w_in[:, nq + 2 * GDN_HEADS:], ba_cols], axis=1).astype(BF16)
    qkv, z, u, ba = _inproj(x2, w_cat)
    ya = _gdn(qkv, z, ba, conv_qkv[0], gdn_a_log[0], gdn_dt_bias[0], gdn_norm[0], bsz, seq)
    yb = _s5(u.reshape(bsz, seq, S5_W), s5_lam_re[0], s5_lam_im[0], s5_log_dt[0], s5_b_re[0], s5_b_im[0],
             s5_c_re[0], s5_c_im[0], s5_d[0], s5_w_glu[0]).reshape(n, S5_W)
    router0 = (router_group_w[0], router_group_b[0], router_expert_w[0], router_expert_b[0])
    h, route, pos, tab = _mix_call(_mix_ab_kernel, ((ya, tm), (yb, tm)), x2, w_out_ab[0], ln_mix_g[0], ln_mix_b[0], router0)
    h = _moe(h, route, pos, tab, moe_w_gate, moe_w_up, moe_w_down, ln_ffn_g[0], ln_ffn_b[0], 0)

    acts = []
    for qkv_view, dil in zip(_qkv_proj(h, w_qkv_c[0]), DILATIONS):
        acts.append(_attn_branch(qkv_view, bsz, seq, dil))
    lanes_per_head = LANES // ATTN_HEADS
    expand = (jnp.arange(LANES)[:, None] == (jnp.arange(D_MODEL)[None, :] // ATTN_DIM) * lanes_per_head).astype(BF16)
    router1 = (router_group_w[1], router_group_b[1], router_expert_w[1], router_expert_b[1])
    d4, d16 = DILATIONS[1], DILATIONS[2]
    (o1, l1), (o4, l4), (o16, l16) = acts
    h, route, pos, tab = _mix_call(
        _mix_c_kernel, ((o1, tm), (o4, tm // d4), (o16, tm // d16), (l1, tm), (l4, tm // d4), (l16, tm // d16)),
        h, w_out_c[0], ln_mix_g[1], ln_mix_b[1], router1, extra=(expand,),
        scratch=(pltpu.VMEM((D_MODEL // LANES, tm, LANES), F32), pltpu.VMEM((D_MODEL // LANES, tm, LANES), F32),
                 pltpu.VMEM((1, tm, LANES), F32), pltpu.VMEM((1, tm, LANES), F32)))
    h = _moe(h, route, pos, tab, moe_w_gate, moe_w_up, moe_w_down, ln_ffn_g[1], ln_ffn_b[1], 1)
    return h.reshape(bsz, seq, d)
```
